```python
import math
import jax, jax.numpy as jnp
from jax import lax
import numpy as np

D_MODEL = 1024
BATCH = 16
SEQ = 2048
DEPTH = 1

D_MIX = D_MODEL
ATT_WIDTH = D_MIX // 2
SSD_WIDTH = D_MIX - ATT_WIDTH
HEAD_DIM = 64
N_Q_HEADS = ATT_WIDTH // HEAD_DIM
N_KV_HEADS = 2
Q_PER_KV = N_Q_HEADS // N_KV_HEADS
ROPE_AXIS_DIM = HEAD_DIM // 2
ROPE_THETA = 10000.0
Q_BLOCK = 128
GRID_W = 64
SSD_HEAD_DIM = 64
SSD_HEADS = SSD_WIDTH // SSD_HEAD_DIM
SSD_GROUPS = 2
HEADS_PER_GROUP = SSD_HEADS // SSD_GROUPS
D_STATE = 128
CONV_WIDTH = 5
CONV_CH = SSD_WIDTH + 2 * SSD_GROUPS * D_STATE
CHUNK = 128
N_DIRS = 2
D_FF = 2816
PLE_DIM = 256
N_NORMS = 4
ALPHA = (2.0 * DEPTH) ** 0.25
BETA = (8.0 * DEPTH) ** -0.25
SPLIT_SIZES = [N_Q_HEADS * HEAD_DIM, N_KV_HEADS * HEAD_DIM, N_KV_HEADS * HEAD_DIM,
               SSD_WIDTH, SSD_WIDTH, SSD_GROUPS * D_STATE, SSD_GROUPS * D_STATE,
               N_DIRS * SSD_HEADS]
IN_PROJ_WIDTH = sum(SPLIT_SIZES)
SPLIT_POINTS = [int(v) for v in np.cumsum(SPLIT_SIZES)[:-1]]
LN_EPS = 1e-5
RMS_EPS = 1e-6

kernel_name = "hymba_attn_ssd_macaron_deepnorm_block"


def _layer_norm(x, g, b):
    xf = x.astype(jnp.float32)
    mu = jnp.mean(xf, axis=-1, keepdims=True)
    var = jnp.mean(jnp.square(xf - mu), axis=-1, keepdims=True)
    y = (xf - mu) * lax.rsqrt(var + LN_EPS) * g.astype(jnp.float32) + b.astype(jnp.float32)
    return y.astype(x.dtype)


def _rms(xf, w):
    return xf * lax.rsqrt(jnp.mean(jnp.square(xf), axis=-1, keepdims=True) + RMS_EPS) * w.astype(jnp.float32)


def _swiglu(x, w_in, w_out):
    gate, up = jnp.split(x @ w_in, 2, axis=-1)
    return (jax.nn.silu(gate) * up) @ w_out


def _axial_rope_tables(S):
    rows = S // GRID_W
    row = jnp.repeat(jnp.arange(rows, dtype=jnp.float32), GRID_W)
    col = jnp.tile(jnp.arange(GRID_W, dtype=jnp.float32), rows)
    inv = ROPE_THETA ** (-jnp.arange(0, ROPE_AXIS_DIM, 2, dtype=jnp.float32) / ROPE_AXIS_DIM)
    ang = jnp.concatenate([row[:, None] * inv, col[:, None] * inv], axis=-1)
    return jnp.cos(ang), jnp.sin(ang)


def _apply_rope(xf, cos, sin):
    xr = xf.reshape(*xf.shape[:-1], HEAD_DIM // 2, 2)
    x0, x1 = xr[..., 0], xr[..., 1]
    c = cos[None, :, None, :]
    s = sin[None, :, None, :]
    return jnp.stack([x0 * c - x1 * s, x0 * s + x1 * c], axis=-1).reshape(xf.shape)


def _block_attention(q, k, v):
    B, S = q.shape[0], q.shape[1]
    nblk = S // Q_BLOCK
    scale = HEAD_DIM ** -0.5
    qb = q.reshape(B, nblk, Q_BLOCK, N_KV_HEADS, Q_PER_KV, HEAD_DIM).transpose(1, 0, 2, 3, 4, 5)

    def one_block(qi):
        s = jnp.einsum('bqhgd,bkhd->bhgqk', qi, k, preferred_element_type=jnp.float32) * scale
        pr = jax.nn.softmax(s, axis=-1)
        return jnp.einsum('bhgqk,bkhd->bqhgd', pr.astype(v.dtype), v)

    out = lax.map(one_block, qb)
    return out.transpose(1, 0, 2, 3, 4, 5).reshape(B, S, N_Q_HEADS * HEAD_DIM)


def _ssd_scan(xh, dt, a, bm, cm):
    B, S = xh.shape[0], xh.shape[1]
    nc = S // CHUNK
    G, R, P, N = SSD_GROUPS, HEADS_PER_GROUP, SSD_HEAD_DIM, D_STATE
    xd = (xh * dt[..., None]).reshape(B, nc, CHUNK, G, R, P)
    adt = (dt * a).reshape(B, nc, CHUNK, G, R).transpose(0, 3, 4, 1, 2)
    a_cum = jnp.cumsum(adt, axis=-1)
    b_c = bm.reshape(B, nc, CHUNK, G, N)
    c_c = cm.reshape(B, nc, CHUNK, G, N)
    tril = jnp.tril(jnp.ones((CHUNK, CHUNK), dtype=bool))
    seg = a_cum[..., :, None] - a_cum[..., None, :]
    Lmat = jnp.exp(jnp.where(tril, seg, -jnp.inf))
    cb = jnp.einsum('bclgn,bcsgn->bcgls', c_c, b_c)
    y_diag = jnp.einsum('bcgls,bgrcls,bcsgrp->bclgrp', cb, Lmat, xd)
    decay_states = jnp.exp(a_cum[..., -1:] - a_cum)
    states = jnp.einsum('bcsgn,bgrcs,bcsgrp->bcgrpn', b_c, decay_states, xd)
    chunk_decay = jnp.exp(a_cum[..., -1])

    def step(h, inp):
        st, d = inp
        return h * d[..., None, None] + st, h

    h0 = jnp.zeros((B, G, R, P, N), jnp.float32)
    _, prev = lax.scan(step, h0, (jnp.moveaxis(states, 1, 0), jnp.moveaxis(chunk_decay, 3, 0)))
    prev = jnp.moveaxis(prev, 0, 1)
    y_off = jnp.einsum('bclgn,bcgrpn,bgrcl->bclgrp', c_c, prev, jnp.exp(a_cum))
    return (y_diag + y_off).reshape(B, S, G, R, P)


def setup_inputs(seed: int = 0) -> dict:
    key = jax.random.key(seed)
    ks = jax.random.split(key, 21)
    L = DEPTH
    nrm = jax.random.normal
    x = nrm(ks[0], (BATCH, SEQ, D_MODEL), jnp.float32)
    p = nrm(ks[1], (DEPTH, BATCH, SEQ, PLE_DIM), jnp.float32)
    ln_g = 1.0 + 0.02 * nrm(ks[2], (L, N_NORMS, D_MODEL), jnp.float32)
    ln_b = 0.02 * nrm(ks[3], (L, N_NORMS, D_MODEL), jnp.float32)
    ffn1_w_in = nrm(ks[4], (L, D_MODEL, 2 * D_FF), jnp.float32) * D_MODEL ** -0.5
    ffn1_w_out = nrm(ks[5], (L, D_FF, D_MODEL), jnp.float32) * (D_FF ** -0.5 * BETA)
    w_in = nrm(ks[6], (L, D_MODEL, IN_PROJ_WIDTH), jnp.float32) * D_MODEL ** -0.5
    q_norm = 1.0 + 0.02 * nrm(ks[7], (L, HEAD_DIM), jnp.float32)
    k_norm = 1.0 + 0.02 * nrm(ks[8], (L, HEAD_DIM), jnp.float32)
    conv_w = nrm(ks[9], (L, CONV_WIDTH, CONV_CH), jnp.float32) * CONV_WIDTH ** -0.5
    conv_b = 0.02 * nrm(ks[10], (L, CONV_CH), jnp.float32)
    dt0 = jnp.exp(jax.random.uniform(ks[11], (L, N_DIRS, SSD_HEADS), jnp.float32,
                                     minval=math.log(1e-3), maxval=math.log(1e-1)))
    dt_bias = dt0 + jnp.log(-jnp.expm1(-dt0))
    a_log = jnp.log(jax.random.uniform(ks[12], (L, N_DIRS, SSD_HEADS), jnp.float32, minval=1.0, maxval=16.0))
    d_skip = 1.0 + 0.1 * nrm(ks[13], (L, SSD_HEADS), jnp.float32)
    ssd_norm = 1.0 + 0.02 * nrm(ks[14], (L, SSD_WIDTH), jnp.float32)
    w_out = nrm(ks[15], (L, D_MIX, D_MODEL), jnp.float32) * (D_MIX ** -0.5 * BETA)
    ffn2_w_in = nrm(ks[16], (L, D_MODEL, 2 * D_FF), jnp.float32) * D_MODEL ** -0.5
    ffn2_w_out = nrm(ks[17], (L, D_FF, D_MODEL), jnp.float32) * (D_FF ** -0.5 * BETA)
    ple_w = nrm(ks[18], (L, PLE_DIM, D_MODEL), jnp.float32) * (PLE_DIM ** -0.5 * BETA)
    ple_gate_w = nrm(ks[19], (L, D_MODEL, D_MODEL), jnp.float32) * D_MODEL ** -0.5
    ple_gate_b = 0.02 * nrm(ks[20], (L, D_MODEL), jnp.float32)
    return {"x": x, "p": p, "ln_g": ln_g, "ln_b": ln_b,
            "ffn1_w_in": ffn1_w_in, "ffn1_w_out": ffn1_w_out,
            "w_in": w_in, "q_norm": q_norm, "k_norm": k_norm,
            "conv_w": conv_w, "conv_b": conv_b, "dt_bias": dt_bias, "a_log": a_log,
            "d_skip": d_skip, "ssd_norm": ssd_norm, "w_out": w_out,
            "ffn2_w_in": ffn2_w_in, "ffn2_w_out": ffn2_w_out,
            "ple_w": ple_w, "ple_gate_w": ple_gate_w, "ple_gate_b": ple_gate_b}


def reference(x, p, ln_g, ln_b, ffn1_w_in, ffn1_w_out, w_in, q_norm, k_norm,
              conv_w, conv_b, dt_bias, a_log, d_skip, ssd_norm, w_out,
              ffn2_w_in, ffn2_w_out, ple_w, ple_gate_w, ple_gate_b):
    B, S, _ = x.shape
    cos, sin = _axial_rope_tables(S)
    G, R, P = SSD_GROUPS, HEADS_PER_GROUP, SSD_HEAD_DIM
    for i in range(DEPTH):
        x = _layer_norm(ALPHA * x + 0.5 * _swiglu(x, ffn1_w_in[i], ffn1_w_out[i]), ln_g[i, 0], ln_b[i, 0])

        u = x @ w_in[i]
        q, k, v, z, xs, bm, cm, dt_raw = jnp.split(u, SPLIT_POINTS, axis=-1)

        qf = _rms(q.reshape(B, S, N_Q_HEADS, HEAD_DIM).astype(jnp.float32), q_norm[i])
        kf = _rms(k.reshape(B, S, N_KV_HEADS, HEAD_DIM).astype(jnp.float32), k_norm[i])
        qh = _apply_rope(qf, cos, sin).astype(x.dtype)
        kh = _apply_rope(kf, cos, sin).astype(x.dtype)
        vh = v.reshape(B, S, N_KV_HEADS, HEAD_DIM)
        att_out = _block_attention(qh, kh, vh)

        xbc = jnp.concatenate([xs, bm, cm], axis=-1)
        cw = conv_w[i].astype(xbc.dtype).reshape(CONV_WIDTH, 1, CONV_CH)
        xbc = lax.conv_general_dilated(xbc, cw, window_strides=(1,),
                                       padding=[(CONV_WIDTH // 2, CONV_WIDTH // 2)],
                                       dimension_numbers=('NWC', 'WIO', 'NWC'),
                                       feature_group_count=CONV_CH)
        xbc = jax.nn.silu(xbc.astype(jnp.float32) + conv_b[i].astype(jnp.float32))
        xs_c, bm_c, cm_c = jnp.split(xbc, [SSD_WIDTH, SSD_WIDTH + G * D_STATE], axis=-1)
        xh = xs_c.reshape(B, S, G, R, P)
        bm_c = bm_c.reshape(B, S, G, D_STATE)
        cm_c = cm_c.reshape(B, S, G, D_STATE)
        dt_all = jax.nn.softplus(dt_raw.astype(jnp.float32).reshape(B, S, N_DIRS, G, R)
                                 + dt_bias[i].astype(jnp.float32).reshape(N_DIRS, G, R))
        a_all = -jnp.exp(a_log[i].astype(jnp.float32)).reshape(N_DIRS, G, R)
        y_f = _ssd_scan(xh, dt_all[:, :, 0], a_all[0], bm_c, cm_c)
        y_b = jnp.flip(_ssd_scan(jnp.flip(xh, 1), jnp.flip(dt_all[:, :, 1], 1), a_all[1],
                                 jnp.flip(bm_c, 1), jnp.flip(cm_c, 1)), 1)
        y = y_f + y_b + d_skip[i].astype(jnp.float32).reshape(G, R)[:, :, None] * xh
        y = y.reshape(B, S, SSD_WIDTH) * jax.nn.silu(z.astype(jnp.float32))
        y = _rms(y.reshape(B, S, G, SSD_WIDTH // G), jnp.ones((), jnp.float32)).reshape(B, S, SSD_WIDTH)
        ssd_out = (y * ssd_norm[i].astype(jnp.float32)).astype(x.dtype)

        mix = jnp.concatenate([att_out, ssd_out], axis=-1) @ w_out[i]
        x = _layer_norm(ALPHA * x + mix, ln_g[i, 1], ln_b[i, 1])

        x = _layer_norm(ALPHA * x + 0.5 * _swiglu(x, ffn2_w_in[i], ffn2_w_out[i]), ln_g[i, 2], ln_b[i, 2])

        e = p[i] @ ple_w[i]
        gate = jax.nn.sigmoid(x @ ple_gate_w[i] + ple_gate_b[i])
        x = _layer_norm(ALPHA * x + gate * e, ln_g[i, 3], ln_b[i, 3])
    return x
```

```python
import functools
import math

import numpy as np
import jax
import jax.numpy as jnp
from jax import lax
from jax.experimental import pallas as pl
from jax.experimental.pallas import tpu as pltpu

F32 = jnp.float32
BF16 = jnp.bfloat16

D_MODEL = 1024
SEQ = 2048
DEPTH = 1
HEAD_DIM = 64
N_Q_HEADS = 8
N_KV_HEADS = 2
ROPE_AXIS_DIM = HEAD_DIM // 2
ROPE_THETA = 10000.0
GRID_W = 64
ATT_WIDTH = N_Q_HEADS * HEAD_DIM
SSD_WIDTH = 512
SSD_HEAD_DIM = 64
SSD_HEADS = 8
SSD_GROUPS = 2
D_STATE = 128
CONV_WIDTH = 5
CONV_CH = SSD_WIDTH + 2 * SSD_GROUPS * D_STATE
CHUNK = 128
N_DIRS = 2
D_FF = 2816
PLE_DIM = 256
ALPHA = (2.0 * DEPTH) ** 0.25
LN_EPS = 1e-5
RMS_EPS = 1e-6
LOG2E = 1.4426950408889634

LANES = 128
HALF_LANES = LANES // 2
TOKEN_TILE = 256
Q_TILE = 256
FF_CHUNKS = ((0, 768), (768, 768), (1536, 768), (2304, 512))
DT_PAD = LANES
Q_OFF = 0
K_OFF = Q_OFF + ATT_WIDTH
V_OFF = K_OFF + N_KV_HEADS * LANES
Z_OFF = V_OFF + N_KV_HEADS * LANES
XBC_OFF = Z_OFF + SSD_WIDTH
DT_OFF = XBC_OFF + CONV_CH
IN_COLS = DT_OFF + DT_PAD
VMEM_LIMIT = 56 * 1024 * 1024


def _const_spec(shape):
    nd = len(shape)
    return pl.BlockSpec(shape, lambda *_: (0,) * nd, pipeline_mode=pl.Buffered(1))


def _layer_norm(y, g, b):
    mu = jnp.mean(y, axis=-1, keepdims=True)
    yc = y - mu
    var = jnp.mean(yc * yc, axis=-1, keepdims=True)
    return yc * lax.rsqrt(var + LN_EPS) * g + b


def _sigmoid(x):
    return 1.0 / (1.0 + jnp.exp(-x))


def _swiglu(xb, wg_ref, wu_ref, wo_ref):
    acc = None
    for c0, cw in FF_CHUNKS:
        hg = jnp.dot(xb, wg_ref[:, c0:c0 + cw], preferred_element_type=F32)
        hu = jnp.dot(xb, wu_ref[:, c0:c0 + cw], preferred_element_type=F32)
        act = (hg * _sigmoid(hg) * hu).astype(BF16)
        part = jnp.dot(act, wo_ref[c0:c0 + cw, :], preferred_element_type=F32)
        acc = part if acc is None else acc + part
    return acc


def _split2(x):
    hi = x.astype(BF16)
    lo = (x - hi.astype(F32)).astype(BF16)
    return hi, lo


def _split3(x):
    hi = x.astype(BF16)
    r1 = x - hi.astype(F32)
    mid = r1.astype(BF16)
    lo = (r1 - mid.astype(F32)).astype(BF16)
    return hi, mid, lo


def _norm_rope(t, seg_mean, w, cos, sin):
    hi, lo = _split2(t * t)
    ms = (jnp.dot(hi, seg_mean, preferred_element_type=F32)
          + jnp.dot(lo, seg_mean, preferred_element_type=F32))
    tn = t * lax.rsqrt(ms + RMS_EPS) * w
    return tn * cos + pltpu.roll(tn, HALF_LANES, axis=1) * sin


def _front_kernel(x_ref, wg_ref, wu_ref, wo_ref, g_ref, b_ref, win_ref,
                  cos_ref, sin_ref, qw_ref, kw_ref, mq_ref, mk_ref,
                  x1_ref, q_ref, k_ref, v_ref, z_ref, xbc_ref, dt_ref):
    x = x_ref[...]
    f = _swiglu(x.astype(BF16), wg_ref, wu_ref, wo_ref)
    x1 = _layer_norm(ALPHA * x + 0.5 * f, g_ref[...], b_ref[...])
    x1_ref[...] = x1
    u = jnp.dot(x1.astype(BF16), win_ref[...], preferred_element_type=F32)
    cos = cos_ref[...]
    sin = sin_ref[...]
    mq = mq_ref[...]
    mk = mk_ref[...]
    for j in range(ATT_WIDTH // LANES):
        t = u[:, Q_OFF + j * LANES:Q_OFF + (j + 1) * LANES]
        q_ref[:, j * LANES:(j + 1) * LANES] = _norm_rope(t, mq, qw_ref[...], cos, sin).astype(BF16)
    for j in range(N_KV_HEADS):
        t = u[:, K_OFF + j * LANES:K_OFF + (j + 1) * LANES]
        k_ref[:, j * LANES:(j + 1) * LANES] = _norm_rope(t, mk, kw_ref[...], cos, sin).astype(BF16)
    v_ref[...] = u[:, V_OFF:Z_OFF].astype(BF16)
    z_ref[...] = u[:, Z_OFF:XBC_OFF]
    xbc_ref[...] = u[:, XBC_OFF:DT_OFF]
    dt_ref[...] = u[:, DT_OFF:IN_COLS]


def _front_call(x2d, wg, wu, wo, g, b, wcat, cos_t, sin_t, qw, kw, mq, mk):
    T = x2d.shape[0]
    tm = TOKEN_TILE
    pos_blocks = SEQ // tm
    row = lambda i: (i, 0)
    pos = lambda i: (i % pos_blocks, 0)
    out_shape = (
        jax.ShapeDtypeStruct((T, D_MODEL), F32),
        jax.ShapeDtypeStruct((T, ATT_WIDTH), BF16),
        jax.ShapeDtypeStruct((T, N_KV_HEADS * LANES), BF16),
        jax.ShapeDtypeStruct((T, N_KV_HEADS * LANES), BF16),
        jax.ShapeDtypeStruct((T, SSD_WIDTH), F32),
        jax.ShapeDtypeStruct((T, CONV_CH), F32),
        jax.ShapeDtypeStruct((T, DT_PAD), F32),
    )
    in_specs = [
        pl.BlockSpec((tm, D_MODEL), row),
        _const_spec(wg.shape), _const_spec(wu.shape), _const_spec(wo.shape),
        _const_spec(g.shape), _const_spec(b.shape), _const_spec(wcat.shape),
        pl.BlockSpec((tm, LANES), pos), pl.BlockSpec((tm, LANES), pos),
        _const_spec(qw.shape), _const_spec(kw.shape), _const_spec(mq.shape), _const_spec(mk.shape),
    ]
    out_specs = tuple(pl.BlockSpec((tm, s.shape[1]), row) for s in out_shape)
    return pl.pallas_call(
        _front_kernel, grid=(T // tm,), in_specs=in_specs, out_specs=out_specs, out_shape=out_shape,
        name="front_ffn_inproj",
        compiler_params=pltpu.CompilerParams(dimension_semantics=("arbitrary",),
                                             vmem_limit_bytes=VMEM_LIMIT),
    )(x2d, wg, wu, wo, g, b, wcat, cos_t, sin_t, qw, kw, mq, mk)


def _attn_kernel(q_ref, k_ref, v_ref, o_ref):
    tq = q_ref.shape[0]
    lane = lax.broadcasted_iota(jnp.int32, (1, LANES), 1)
    low = lane < HALF_LANES
    q_per_kv_tiles = (N_Q_HEADS // N_KV_HEADS) * HEAD_DIM // LANES
    for kv in range(N_KV_HEADS):
        kt = k_ref[:, kv * LANES:(kv + 1) * LANES]
        vt = v_ref[:, kv * LANES:(kv + 1) * LANES]
        zero = jnp.zeros_like(vt)
        v_lo = jnp.where(low, vt, zero)
        v_hi = jnp.where(low, zero, vt)
        parts = []
        for jj in range(q_per_kv_tiles):
            j = kv * q_per_kv_tiles + jj
            qt = q_ref[:, j * LANES:(j + 1) * LANES]
            blk = (lane // (HEAD_DIM // 2)) % 2
            zq = jnp.zeros_like(qt)
            parts.append(jnp.where(blk == 0, qt, zq))
            parts.append(jnp.where(blk == 1, qt, zq))
        qs = jnp.concatenate(parts, axis=0)
        s = lax.dot_general(qs, kt, (((1,), (1,)), ((), ())), preferred_element_type=F32)
        m = jnp.max(s, axis=-1, keepdims=True)
        p = jnp.exp2(s - m)
        l = jnp.sum(p, axis=-1, keepdims=True)
        pb = p.astype(BF16)
        inv = 1.0 / l
        for jj in range(q_per_kv_tiles):
            j = kv * q_per_kv_tiles + jj
            ra = (2 * jj) * tq
            rb = (2 * jj + 1) * tq
            oa = jnp.dot(pb[ra:ra + tq], v_lo, preferred_element_type=F32) * inv[ra:ra + tq]
            ob = jnp.dot(pb[rb:rb + tq], v_hi, preferred_element_type=F32) * inv[rb:rb + tq]
            o_ref[:, j * LANES:(j + 1) * LANES] = (oa + ob).astype(BF16)


def _attn_call(q, k, v, batch):
    T = q.shape[0]
    nq = SEQ // Q_TILE
    return pl.pallas_call(
        _attn_kernel, grid=(batch, nq),
        in_specs=[pl.BlockSpec((Q_TILE, ATT_WIDTH), lambda b, i: (b * nq + i, 0)),
                  pl.BlockSpec((SEQ, N_KV_HEADS * LANES), lambda b, i: (b, 0)),
                  pl.BlockSpec((SEQ, N_KV_HEADS * LANES), lambda b, i: (b, 0))],
        out_specs=pl.BlockSpec((Q_TILE, ATT_WIDTH), lambda b, i: (b * nq + i, 0)),
        out_shape=jax.ShapeDtypeStruct((T, ATT_WIDTH), BF16),
        name="attention",
        compiler_params=pltpu.CompilerParams(dimension_semantics=("arbitrary", "arbitrary"),
                                             vmem_limit_bytes=VMEM_LIMIT),
    )(q, k, v)


def _softplus(x):
    return jnp.maximum(x, 0.0) + jnp.log1p(jnp.exp(-jnp.abs(x)))


def _pair_expand(v, c0, c1, low):
    rows = v.shape[0]
    a = jnp.broadcast_to(v[:, c0:c0 + 1], (rows, LANES))
    b = jnp.broadcast_to(v[:, c1:c1 + 1], (rows, LANES))
    return jnp.where(low, a, b)


def _ssd_kernel(xbc_ref, z_ref, dt_ref, cw_ref, cb_ref, dtb_ref, alog_ref, dsk_ref, nw_ref, tri_ref,
                o_ref, xc_ref, y_ref, st_ref):
    S = xbc_ref.shape[0]
    n_chunks = S // CHUNK
    heads_per_group = SSD_HEADS // SSD_GROUPS
    pair_tiles = SSD_WIDTH // LANES
    lane = lax.broadcasted_iota(jnp.int32, (1, LANES), 1)
    low = lane < HALF_LANES

    t_idx = lax.broadcasted_iota(jnp.int32, (S, 1), 0)
    half = CONV_WIDTH // 2
    for c in range(CONV_CH // LANES):
        cols = slice(c * LANES, (c + 1) * LANES)
        xin = xbc_ref[:, cols]
        acc = cb_ref[:, cols] + cw_ref[half:half + 1, cols] * xin
        for j in range(CONV_WIDTH):
            off = j - half
            if off == 0:
                continue
            rolled = pltpu.roll(xin, (-off) % S, axis=0)
            valid = (t_idx >= -off) if off < 0 else (t_idx < S - off)
            acc = acc + cw_ref[j:j + 1, cols] * jnp.where(valid, rolled, 0.0)
        xc_ref[:, cols] = acc * _sigmoid(acc)

    y_ref[...] = jnp.zeros_like(y_ref)
    st_ref[...] = jnp.zeros_like(st_ref)

    a_neg = -jnp.exp(alog_ref[...])
    dtb = dtb_ref[...]
    tril = tri_ref[...]
    r_i = lax.broadcasted_iota(jnp.int32, (CHUNK, CHUNK), 0)
    c_i = lax.broadcasted_iota(jnp.int32, (CHUNK, CHUNK), 1)
    masks = (c_i <= r_i, c_i >= r_i)

    def one_chunk(r0, d):
        rows = pl.ds(r0, CHUNK)
        dt = _softplus(dt_ref[rows, :] + dtb)
        adt = dt * a_neg
        hi, mid, lo = _split3(adt)
        cum_f = (jnp.dot(tril, hi, preferred_element_type=F32)
                 + jnp.dot(tril, mid, preferred_element_type=F32)
                 + jnp.dot(tril, lo, preferred_element_type=F32))
        tot = jnp.sum(adt, axis=0, keepdims=True)
        cum = cum_f if d == 0 else (tot - cum_f) + adt
        cum_t = cum.T
        rem = tot - cum
        bm = xc_ref[rows, SSD_WIDTH:SSD_WIDTH + SSD_GROUPS * D_STATE]
        cm = xc_ref[rows, SSD_WIDTH + SSD_GROUPS * D_STATE:CONV_CH]
        for g in range(SSD_GROUPS):
            b_g = bm[:, g * D_STATE:(g + 1) * D_STATE]
            c_g = cm[:, g * D_STATE:(g + 1) * D_STATE].astype(BF16)
            b_gt = b_g.T.astype(BF16)
            gmat = jnp.dot(c_g, b_gt, preferred_element_type=F32)
            for tt in range(pair_tiles // SSD_GROUPS):
                t = g * (pair_tiles // SSD_GROUPS) + tt
                h0 = d * SSD_HEADS + 2 * t
                h1 = h0 + 1
                cols = slice(t * LANES, (t + 1) * LANES)
                x_t = xc_ref[rows, cols]
                xd = x_t * _pair_expand(dt, h0, h1, low)
                xd_b = xd.astype(BF16)
                st = st_ref[d, :, cols]
                y_off = jnp.dot(c_g, st.astype(BF16), preferred_element_type=F32) \
                    * jnp.exp(_pair_expand(cum, h0, h1, low))
                ys = []
                for h in (h0, h1):
                    seg = jnp.broadcast_to(cum[:, h:h + 1], (CHUNK, CHUNK)) \
                        - jnp.broadcast_to(cum_t[h:h + 1, :], (CHUNK, CHUNK))
                    lmat = jnp.exp(jnp.where(masks[d], seg, -jnp.inf))
                    w = (gmat * lmat).astype(BF16)
                    ys.append(jnp.dot(w, xd_b, preferred_element_type=F32))
                y_diag = jnp.where(low, ys[0], ys[1])
                y_ref[rows, cols] = y_ref[rows, cols] + y_diag + y_off
                xds = (xd * jnp.exp(_pair_expand(rem, h0, h1, low))).astype(BF16)
                tot_e = jnp.where(low, jnp.broadcast_to(tot[:, h0:h0 + 1], (1, LANES)),
                                  jnp.broadcast_to(tot[:, h1:h1 + 1], (1, LANES)))
                st_ref[d, :, cols] = st * jnp.exp(tot_e) + jnp.dot(b_gt, xds, preferred_element_type=F32)

    def body(c, carry):
        one_chunk(pl.multiple_of(c * CHUNK, CHUNK), 0)
        one_chunk(pl.multiple_of((n_chunks - 1 - c) * CHUNK, CHUNK), 1)
        return carry

    lax.fori_loop(0, n_chunks, body, 0)

    group_w = SSD_WIDTH // SSD_GROUPS
    def fin(c, carry):
        rows = pl.ds(pl.multiple_of(c * CHUNK, CHUNK), CHUNK)
        zz = z_ref[rows, :]
        y = (y_ref[rows, :] + dsk_ref[...] * xc_ref[rows, 0:SSD_WIDTH]) * (zz * _sigmoid(zz))
        outs = []
        for g in range(SSD_GROUPS):
            yg = y[:, g * group_w:(g + 1) * group_w]
            ms = jnp.mean(yg * yg, axis=-1, keepdims=True)
            outs.append(yg * lax.rsqrt(ms + RMS_EPS))
        o_ref[rows, :] = (jnp.concatenate(outs, axis=-1) * nw_ref[...]).astype(BF16)
        return carry

    lax.fori_loop(0, n_chunks, fin, 0)


def _ssd_call(xbc, z, dt, cw, cb, dtb, alog, dsk, nw, tri, batch):
    T = xbc.shape[0]
    seq = lambda b: (b, 0)
    return pl.pallas_call(
        _ssd_kernel, grid=(batch,),
        in_specs=[pl.BlockSpec((SEQ, CONV_CH), seq), pl.BlockSpec((SEQ, SSD_WIDTH), seq),
                  pl.BlockSpec((SEQ, DT_PAD), seq),
                  _const_spec(cw.shape), _const_spec(cb.shape), _const_spec(dtb.shape),
                  _const_spec(alog.shape), _const_spec(dsk.shape), _const_spec(nw.shape),
                  _const_spec(tri.shape)],
        out_specs=pl.BlockSpec((SEQ, SSD_WIDTH), seq),
        out_shape=jax.ShapeDtypeStruct((T, SSD_WIDTH), BF16),
        scratch_shapes=[pltpu.VMEM((SEQ, CONV_CH), F32),
                        pltpu.VMEM((SEQ, SSD_WIDTH), F32),
                        pltpu.VMEM((N_DIRS, D_STATE, SSD_WIDTH), F32)],
        name="ssd_bidir",
        compiler_params=pltpu.CompilerParams(dimension_semantics=("arbitrary",),
                                             vmem_limit_bytes=VMEM_LIMIT),
    )(xbc, z, dt, cw, cb, dtb, alog, dsk, nw, tri)


def _back_kernel(x1_ref, att_ref, ssd_ref, p_ref, woa_ref, wos_ref, wg_ref, wu_ref, wo_ref,
                 lng_ref, lnb_ref, wp_ref, wgate_ref, bgate_ref, o_ref):
    mix = (jnp.dot(att_ref[...], woa_ref[...], preferred_element_type=F32)
           + jnp.dot(ssd_ref[...], wos_ref[...], preferred_element_type=F32))
    x2 = _layer_norm(ALPHA * x1_ref[...] + mix, lng_ref[0:1, :], lnb_ref[0:1, :])
    f = _swiglu(x2.astype(BF16), wg_ref, wu_ref, wo_ref)
    x3 = _layer_norm(ALPHA * x2 + 0.5 * f, lng_ref[1:2, :], lnb_ref[1:2, :])
    e = jnp.dot(p_ref[...].astype(BF16), wp_ref[...], preferred_element_type=F32)
    gate = _sigmoid(jnp.dot(x3.astype(BF16), wgate_ref[...], preferred_element_type=F32) + bgate_ref[...])
    o_ref[...] = _layer_norm(ALPHA * x3 + gate * e, lng_ref[2:3, :], lnb_ref[2:3, :])


def _back_call(x1, att, ssd, p2d, woa, wos, wg, wu, wo, lng, lnb, wp, wgate, bgate):
    T = x1.shape[0]
    tm = TOKEN_TILE
    row = lambda i: (i, 0)
    consts = (woa, wos, wg, wu, wo, lng, lnb, wp, wgate, bgate)
    return pl.pallas_call(
        _back_kernel, grid=(T // tm,),
        in_specs=[pl.BlockSpec((tm, D_MODEL), row), pl.BlockSpec((tm, ATT_WIDTH), row),
                  pl.BlockSpec((tm, SSD_WIDTH), row), pl.BlockSpec((tm, PLE_DIM), row)]
                 + [_const_spec(c.shape) for c in consts],
        out_specs=pl.BlockSpec((tm, D_MODEL), row),
        out_shape=jax.ShapeDtypeStruct((T, D_MODEL), F32),
        name="back_outproj_ffn_ple",
        compiler_params=pltpu.CompilerParams(dimension_semantics=("arbitrary",),
                                             vmem_limit_bytes=VMEM_LIMIT),
    )(x1, att, ssd, p2d, *consts)


def _in_proj_columns():
    half_blk = HEAD_DIM // 2
    lanes = np.arange(LANES)
    blk, i = lanes // half_blk, lanes % half_blk
    q_cols = []
    for j in range(ATT_WIDTH // LANES):
        head = 2 * j + blk % 2
        q_cols.append(head * HEAD_DIM + 2 * i + blk // 2)
    k_base = ATT_WIDTH
    k_cols = [k_base + kv * HEAD_DIM + 2 * i + blk // 2 for kv in range(N_KV_HEADS)]
    v_base = k_base + N_KV_HEADS * HEAD_DIM
    v_cols = [v_base + kv * HEAD_DIM + lanes % HEAD_DIM for kv in range(N_KV_HEADS)]
    rest_base = v_base + N_KV_HEADS * HEAD_DIM
    rest = np.arange(rest_base, rest_base + SSD_WIDTH + CONV_CH + N_DIRS * SSD_HEADS)
    return np.concatenate(q_cols + k_cols + v_cols + [rest]).astype(np.int32)


def _rope_tiles():
    rows = SEQ // GRID_W
    row = jnp.repeat(jnp.arange(rows, dtype=F32), GRID_W)
    col = jnp.tile(jnp.arange(GRID_W, dtype=F32), rows)
    inv = ROPE_THETA ** (-jnp.arange(0, ROPE_AXIS_DIM, 2, dtype=F32) / ROPE_AXIS_DIM)
    ang = jnp.concatenate([row[:, None] * inv, col[:, None] * inv], axis=-1)
    cos, sin = jnp.cos(ang), jnp.sin(ang)
    cos_t = jnp.concatenate([cos, cos, cos, cos], axis=-1)
    sin_t = jnp.concatenate([-sin, -sin, sin, sin], axis=-1)
    return cos_t, sin_t


def _norm_tile(w):
    lanes = np.arange(LANES)
    idx = 2 * (lanes % (HEAD_DIM // 2)) + lanes // HALF_LANES
    return w[idx][None, :]


def kernel(x, p, ln_g, ln_b, ffn1_w_in, ffn1_w_out, w_in, q_norm, k_norm, conv_w, conv_b, dt_bias,
           a_log, d_skip, ssd_norm, w_out, ffn2_w_in, ffn2_w_out, ple_w, ple_gate_w, ple_gate_b):
    B, S, D = x.shape
    assert (S, D) == (SEQ, D_MODEL) and ln_g.shape[0] == DEPTH == 1
    T = B * S
    x2d = x.reshape(T, D)
    p2d = p[0].reshape(T, PLE_DIM)

    cos_t, sin_t = _rope_tiles()
    lanes = np.arange(LANES)
    same_head = ((lanes[:, None] // (HEAD_DIM // 2)) % 2) == ((lanes[None, :] // (HEAD_DIM // 2)) % 2)
    mq = jnp.asarray(np.where(same_head, 1.0 / HEAD_DIM, 0.0), BF16)
    mk = jnp.full((LANES, LANES), 1.0 / LANES, BF16)
    tri = jnp.asarray(np.tril(np.ones((CHUNK, CHUNK), np.float32)), BF16)

    i = 0
    wg1, wu1 = ffn1_w_in[i, :, :D_FF].astype(BF16), ffn1_w_in[i, :, D_FF:].astype(BF16)
    wo1 = ffn1_w_out[i].astype(BF16)
    wcat = jnp.concatenate([w_in[i][:, _in_proj_columns()],
                            jnp.zeros((D, DT_PAD - N_DIRS * SSD_HEADS), F32)], axis=1).astype(BF16)
    qw = _norm_tile(q_norm[i]) * (HEAD_DIM ** -0.5 * LOG2E)
    kw = _norm_tile(k_norm[i])
    x1, qh, kh, vh, z, xbc, dtr = _front_call(
        x2d, wg1, wu1, wo1, ln_g[i, 0:1], ln_b[i, 0:1], wcat, cos_t, sin_t, qw, kw, mq, mk)

    att = _attn_call(qh, kh, vh, B)

    pad16 = lambda v: jnp.pad(v.reshape(1, -1), ((0, 0), (0, DT_PAD - N_DIRS * SSD_HEADS)))
    dsk = jnp.repeat(d_skip[i], SSD_HEAD_DIM)[None, :]
    ssd = _ssd_call(xbc, z, dtr, conv_w[i], conv_b[i][None, :], pad16(dt_bias[i]), pad16(a_log[i]),
                    dsk, ssd_norm[i][None, :], tri, B)

    wg2, wu2 = ffn2_w_in[i, :, :D_FF].astype(BF16), ffn2_w_in[i, :, D_FF:].astype(BF16)
    out = _back_call(x1, att, ssd, p2d,
                     w_out[i, :ATT_WIDTH].astype(BF16), w_out[i, ATT_WIDTH:].astype(BF16),
                     wg2, wu2, ffn2_w_out[i].astype(BF16), ln_g[i, 1:4], ln_b[i, 1:4],
                     ple_w[i].astype(BF16), ple_gate_w[i].astype(BF16), ple_gate_b[i][None, :])
    return out.reshape(B, S, D)
```

```python
import functools
import math

import numpy as np
import jax
import jax.numpy as jnp
from jax import lax
from jax.experimental import pallas as pl
from jax.experimental.pallas import tpu as pltpu

F32 = jnp.float32
BF16 = jnp.bfloat16

D_MODEL = 1024
SEQ = 2048
DEPTH = 1
HEAD_DIM = 64
N_Q_HEADS = 8
N_KV_HEADS = 2
ROPE_AXIS_DIM = HEAD_DIM // 2
ROPE_THETA = 10000.0
GRID_W = 64
ATT_WIDTH = N_Q_HEADS * HEAD_DIM
SSD_WIDTH = 512
SSD_HEAD_DIM = 64
SSD_HEADS = 8
SSD_GROUPS = 2
D_STATE = 128
CONV_WIDTH = 5
CONV_CH = SSD_WIDTH + 2 * SSD_GROUPS * D_STATE
CHUNK = 128
N_DIRS = 2
D_FF = 2816
PLE_DIM = 256
ALPHA = (2.0 * DEPTH) ** 0.25
LN_EPS = 1e-5
RMS_EPS = 1e-6
LOG2E = 1.4426950408889634

LANES = 128
HALF_LANES = LANES // 2
TOKEN_TILE = 512
TOKEN_SUB = 256
Q_TILE = 256
FF_CHUNKS = ((0, 768), (768, 768), (1536, 768), (2304, 512))
DT_PAD = LANES
Q_OFF = 0
K_OFF = Q_OFF + ATT_WIDTH
V_OFF = K_OFF + N_KV_HEADS * LANES
Z_OFF = V_OFF + N_KV_HEADS * LANES
XBC_OFF = Z_OFF + SSD_WIDTH
DT_OFF = XBC_OFF + CONV_CH
IN_COLS = DT_OFF + DT_PAD
VMEM_LIMIT = 56 * 1024 * 1024


def _const_spec(shape):
    nd = len(shape)
    return pl.BlockSpec(shape, lambda *_: (0,) * nd, pipeline_mode=pl.Buffered(1))


def _layer_norm(y, g, b):
    mu = jnp.mean(y, axis=-1, keepdims=True)
    yc = y - mu
    var = jnp.mean(yc * yc, axis=-1, keepdims=True)
    return yc * lax.rsqrt(var + LN_EPS) * g + b


def _sigmoid(x):
    return 1.0 / (1.0 + jnp.exp(-x))


def _swiglu(xb, wg_ref, wu_ref, wo_ref):
    acc = None
    for c0, cw in FF_CHUNKS:
        hg = jnp.dot(xb, wg_ref[:, c0:c0 + cw], preferred_element_type=F32)
        hu = jnp.dot(xb, wu_ref[:, c0:c0 + cw], preferred_element_type=F32)
        act = (hg * _sigmoid(hg) * hu).astype(BF16)
        part = jnp.dot(act, wo_ref[c0:c0 + cw, :], preferred_element_type=F32)
        acc = part if acc is None else acc + part
    return acc


def _split2(x):
    hi = x.astype(BF16)
    lo = (x - hi.astype(F32)).astype(BF16)
    return hi, lo


def _split3(x):
    hi = x.astype(BF16)
    r1 = x - hi.astype(F32)
    mid = r1.astype(BF16)
    lo = (r1 - mid.astype(F32)).astype(BF16)
    return hi, mid, lo


def _norm_rope(t, seg_mean, w, cos, sin):
    hi, lo = _split2(t * t)
    ms = (jnp.dot(hi, seg_mean, preferred_element_type=F32)
          + jnp.dot(lo, seg_mean, preferred_element_type=F32))
    tn = t * lax.rsqrt(ms + RMS_EPS) * w
    return tn * cos + pltpu.roll(tn, HALF_LANES, axis=1) * sin


def _front_kernel(x_ref, wg_ref, wu_ref, wo_ref, g_ref, b_ref, win_ref,
                  cos_ref, sin_ref, qw_ref, kw_ref, mq_ref, mk_ref,
                  x1_ref, q_ref, k_ref, v_ref, z_ref, xbc_ref, dt_ref):
    mq = mq_ref[...]
    mk = mk_ref[...]
    for r0 in range(0, x_ref.shape[0], TOKEN_SUB):
        rows = slice(r0, r0 + TOKEN_SUB)
        x = x_ref[rows, :]
        f = _swiglu(x.astype(BF16), wg_ref, wu_ref, wo_ref)
        x1 = _layer_norm(ALPHA * x + 0.5 * f, g_ref[...], b_ref[...])
        x1_ref[rows, :] = x1
        u = jnp.dot(x1.astype(BF16), win_ref[...], preferred_element_type=F32)
        cos = cos_ref[rows, :]
        sin = sin_ref[rows, :]
        for j in range(ATT_WIDTH // LANES):
            t = u[:, Q_OFF + j * LANES:Q_OFF + (j + 1) * LANES]
            q_ref[rows, j * LANES:(j + 1) * LANES] = _norm_rope(t, mq, qw_ref[...], cos, sin).astype(BF16)
        for j in range(N_KV_HEADS):
            t = u[:, K_OFF + j * LANES:K_OFF + (j + 1) * LANES]
            k_ref[rows, j * LANES:(j + 1) * LANES] = _norm_rope(t, mk, kw_ref[...], cos, sin).astype(BF16)
        v_ref[rows, :] = u[:, V_OFF:Z_OFF].astype(BF16)
        z_ref[rows, :] = u[:, Z_OFF:XBC_OFF]
        xbc_ref[rows, :] = u[:, XBC_OFF:DT_OFF]
        dt_ref[rows, :] = u[:, DT_OFF:IN_COLS]


def _front_call(x2d, wg, wu, wo, g, b, wcat, cos_t, sin_t, qw, kw, mq, mk):
    T = x2d.shape[0]
    tm = TOKEN_TILE
    pos_blocks = SEQ // tm
    row = lambda i: (i, 0)
    pos = lambda i: (i % pos_blocks, 0)
    out_shape = (
        jax.ShapeDtypeStruct((T, D_MODEL), F32),
        jax.ShapeDtypeStruct((T, ATT_WIDTH), BF16),
        jax.ShapeDtypeStruct((T, N_KV_HEADS * LANES), BF16),
        jax.ShapeDtypeStruct((T, N_KV_HEADS * LANES), BF16),
        jax.ShapeDtypeStruct((T, SSD_WIDTH), F32),
        jax.ShapeDtypeStruct((T, CONV_CH), F32),
        jax.ShapeDtypeStruct((T, DT_PAD), F32),
    )
    in_specs = [
        pl.BlockSpec((tm, D_MODEL), row),
        _const_spec(wg.shape), _const_spec(wu.shape), _const_spec(wo.shape),
        _const_spec(g.shape), _const_spec(b.shape), _const_spec(wcat.shape),
        pl.BlockSpec((tm, LANES), pos), pl.BlockSpec((tm, LANES), pos),
        _const_spec(qw.shape), _const_spec(kw.shape), _const_spec(mq.shape), _const_spec(mk.shape),
    ]
    out_specs = tuple(pl.BlockSpec((tm, s.shape[1]), row) for s in out_shape)
    return pl.pallas_call(
        _front_kernel, grid=(T // tm,), in_specs=in_specs, out_specs=out_specs, out_shape=out_shape,
        name="front_ffn_inproj",
        compiler_params=pltpu.CompilerParams(dimension_semantics=("arbitrary",),
                                             vmem_limit_bytes=VMEM_LIMIT),
    )(x2d, wg, wu, wo, g, b, wcat, cos_t, sin_t, qw, kw, mq, mk)


def _attn_kernel(q_ref, k_ref, v_ref, o_ref):
    tq = q_ref.shape[0]
    lane = lax.broadcasted_iota(jnp.int32, (1, LANES), 1)
    low = lane < HALF_LANES
    q_per_kv_tiles = (N_Q_HEADS // N_KV_HEADS) * HEAD_DIM // LANES
    for kv in range(N_KV_HEADS):
        kt = k_ref[:, kv * LANES:(kv + 1) * LANES]
        vt = v_ref[:, kv * LANES:(kv + 1) * LANES]
        zero = jnp.zeros_like(vt)
        v_lo = jnp.where(low, vt, zero)
        v_hi = jnp.where(low, zero, vt)
        parts = []
        for jj in range(q_per_kv_tiles):
            j = kv * q_per_kv_tiles + jj
            qt = q_ref[:, j * LANES:(j + 1) * LANES]
            blk = (lane // (HEAD_DIM // 2)) % 2
            zq = jnp.zeros_like(qt)
            parts.append(jnp.where(blk == 0, qt, zq))
            parts.append(jnp.where(blk == 1, qt, zq))
        qs = jnp.concatenate(parts, axis=0)
        s = lax.dot_general(qs, kt, (((1,), (1,)), ((), ())), preferred_element_type=F32)
        m = jnp.max(s, axis=-1, keepdims=True)
        p = jnp.exp2(s - m)
        l = jnp.sum(p, axis=-1, keepdims=True)
        pb = p.astype(BF16)
        inv = 1.0 / l
        for jj in range(q_per_kv_tiles):
            j = kv * q_per_kv_tiles + jj
            ra = (2 * jj) * tq
            rb = (2 * jj + 1) * tq
            oa = jnp.dot(pb[ra:ra + tq], v_lo, preferred_element_type=F32) * inv[ra:ra + tq]
            ob = jnp.dot(pb[rb:rb + tq], v_hi, preferred_element_type=F32) * inv[rb:rb + tq]
            o_ref[:, j * LANES:(j + 1) * LANES] = (oa + ob).astype(BF16)


def _attn_call(q, k, v, batch):
    T = q.shape[0]
    nq = SEQ // Q_TILE
    return pl.pallas_call(
        _attn_kernel, grid=(batch, nq),
        in_specs=[pl.BlockSpec((Q_TILE, ATT_WIDTH), lambda b, i: (b * nq + i, 0)),
                  pl.BlockSpec((SEQ, N_KV_HEADS * LANES), lambda b, i: (b, 0)),
                  pl.BlockSpec((SEQ, N_KV_HEADS * LANES), lambda b, i: (b, 0))],
        out_specs=pl.BlockSpec((Q_TILE, ATT_WIDTH), lambda b, i: (b * nq + i, 0)),
        out_shape=jax.ShapeDtypeStruct((T, ATT_WIDTH), BF16),
        name="attention",
        compiler_params=pltpu.CompilerParams(dimension_semantics=("arbitrary", "arbitrary"),
                                             vmem_limit_bytes=VMEM_LIMIT),
    )(q, k, v)


def _softplus(x):
    return jnp.maximum(x, 0.0) + jnp.log1p(jnp.exp(-jnp.abs(x)))


def _dot3(a_pieces, b, dims):
    out = None
    for piece in a_pieces:
        lhs, rhs = (piece, b) if dims == "piece_lhs" else (b, piece)
        contract = (((1,), (0,)), ((), ())) if dims == "piece_lhs" else (((1,), (1,)), ((), ()))
        term = lax.dot_general(lhs, rhs, contract, preferred_element_type=F32)
        out = term if out is None else out + term
    return out


def _conv_tile(xin, w_ref, b_ref, cols, pad_lo, pad_hi):
    R = xin.shape[0]
    half = CONV_WIDTH // 2
    t_idx = lax.broadcasted_iota(jnp.int32, (R, 1), 0)
    acc = b_ref[:, cols] + w_ref[half:half + 1, cols] * xin
    for j in range(CONV_WIDTH):
        off = j - half
        if off == 0:
            continue
        rolled = pltpu.roll(xin, (-off) % R, axis=0)
        if off < 0 and pad_lo:
            rolled = jnp.where(t_idx >= -off, rolled, 0.0)
        if off > 0 and pad_hi:
            rolled = jnp.where(t_idx < R - off, rolled, 0.0)
        acc = acc + w_ref[j:j + 1, cols] * rolled
    return acc * _sigmoid(acc)


def _ssd_kernel(xbc_ref, z_ref, dt_ref, cw_ref, cb_ref, dtb_ref, alog_ref, dsk_ref, nw_ref, tri_ref,
                o_ref, xc_ref, y_ref, st_ref, bt_ref, dtt_ref):
    S = xbc_ref.shape[0]
    n_chunks = S // CHUNK
    pair_tiles = SSD_WIDTH // LANES
    tiles_per_group = pair_tiles // SSD_GROUPS
    n_heads_all = N_DIRS * SSD_HEADS
    edge = 2 * 8
    b_off = SSD_WIDTH
    c_off = SSD_WIDTH + SSD_GROUPS * D_STATE
    lane = lax.broadcasted_iota(jnp.int32, (1, LANES), 1)
    low = lane < HALF_LANES

    for c in range(CONV_CH // LANES):
        cols = slice(c * LANES, (c + 1) * LANES)
        xc_ref[:, cols] = _conv_tile(xbc_ref[:, cols], cw_ref, cb_ref, cols, False, False)
        xc_ref[0:edge // 2, cols] = _conv_tile(xbc_ref[0:edge, cols], cw_ref, cb_ref, cols, True, False)[0:edge // 2]
        xc_ref[S - edge // 2:S, cols] = _conv_tile(xbc_ref[S - edge:S, cols], cw_ref, cb_ref, cols,
                                                   False, True)[edge // 2:edge]

    y_ref[...] = jnp.zeros_like(y_ref)
    st_ref[...] = jnp.zeros_like(st_ref)

    a_col = -jnp.exp(alog_ref[...])
    dtb_col = dtb_ref[...]

    def prep(c, carry):
        rows = pl.ds(pl.multiple_of(c * CHUNK, CHUNK), CHUNK)
        for g in range(SSD_GROUPS):
            bt_ref[c, g * D_STATE:(g + 1) * D_STATE, :] = xc_ref[rows, b_off + g * D_STATE:b_off + (g + 1) * D_STATE].T
        dtt_ref[c] = _softplus(dt_ref[rows, :].T[0:n_heads_all, :] + dtb_col)
        return carry

    lax.fori_loop(0, n_chunks, prep, 0)

    r_i = lax.broadcasted_iota(jnp.int32, (CHUNK, CHUNK), 0)
    c_i = lax.broadcasted_iota(jnp.int32, (CHUNK, CHUNK), 1)
    masks = (c_i <= r_i, c_i >= r_i)
    zero_rows = jnp.zeros((CHUNK - n_heads_all, CHUNK), BF16)

    def one_chunk(c, d):
        rows = pl.ds(pl.multiple_of(c * CHUNK, CHUNK), CHUNK)
        hs = slice(d * SSD_HEADS, (d + 1) * SSD_HEADS)
        dt_all = dtt_ref[c]
        pieces = _split3(dt_all * a_col)
        cum_row = _dot3(pieces, tri_ref[1 - d], "piece_lhs")[hs]
        padded = [jnp.concatenate([pc, zero_rows], axis=0) for pc in pieces]
        cum_col = _dot3(padded, tri_ref[d], "piece_rhs")
        dt_d = dt_all[hs]
        tot = jnp.sum(dt_d * a_col[hs], axis=1, keepdims=True)
        w_row = dt_d * jnp.exp(tot - cum_row)
        for g in range(SSD_GROUPS):
            c_g = xc_ref[rows, c_off + g * D_STATE:c_off + (g + 1) * D_STATE]
            bt_g = bt_ref[c, g * D_STATE:(g + 1) * D_STATE, :]
            gmat = jnp.dot(c_g.astype(BF16), bt_g.astype(BF16), preferred_element_type=F32)
            for tt in range(tiles_per_group):
                t = g * tiles_per_group + tt
                cols = slice(t * LANES, (t + 1) * LANES)
                x_t = xc_ref[rows, cols].astype(BF16)
                st = st_ref[d, :, cols]
                rhs = jnp.concatenate([x_t, st.astype(BF16)], axis=0)
                ys, sn = [], []
                for hh in (2 * t, 2 * t + 1):
                    col = d * SSD_HEADS + hh
                    colb = jnp.broadcast_to(cum_col[:, col:col + 1], (CHUNK, CHUNK))
                    seg = colb - cum_row[hh:hh + 1, :]
                    lmat = jnp.exp(jnp.where(masks[d], seg, -jnp.inf))
                    w = gmat * lmat * dt_d[hh:hh + 1, :]
                    a = c_g * jnp.exp(colb)
                    lhs = jnp.concatenate([w, a], axis=1).astype(BF16)
                    ys.append(jnp.dot(lhs, rhs, preferred_element_type=F32))
                    bts = (bt_g * w_row[hh:hh + 1, :]).astype(BF16)
                    sn.append(jnp.dot(bts, x_t, preferred_element_type=F32))
                y_ref[rows, cols] = y_ref[rows, cols] + jnp.where(low, ys[0], ys[1])
                decay = jnp.exp(jnp.where(low, tot[2 * t:2 * t + 1, :], tot[2 * t + 1:2 * t + 2, :]))
                st_ref[d, :, cols] = st * decay + jnp.where(low, sn[0], sn[1])

    def body(c, carry):
        one_chunk(c, 0)
        one_chunk(n_chunks - 1 - c, 1)
        return carry

    lax.fori_loop(0, n_chunks, body, 0)

    group_w = SSD_WIDTH // SSD_GROUPS
    def fin(c, carry):
        rows = pl.ds(pl.multiple_of(c * CHUNK, CHUNK), CHUNK)
        zz = z_ref[rows, :]
        y = (y_ref[rows, :] + dsk_ref[...] * xc_ref[rows, 0:SSD_WIDTH]) * (zz * _sigmoid(zz))
        outs = []
        for g in range(SSD_GROUPS):
            yg = y[:, g * group_w:(g + 1) * group_w]
            ms = jnp.mean(yg * yg, axis=-1, keepdims=True)
            outs.append(yg * lax.rsqrt(ms + RMS_EPS))
        o_ref[rows, :] = (jnp.concatenate(outs, axis=-1) * nw_ref[...]).astype(BF16)
        return carry

    lax.fori_loop(0, n_chunks, fin, 0)


def _ssd_call(xbc, z, dt, cw, cb, dtb, alog, dsk, nw, tri, batch):
    T = xbc.shape[0]
    n_chunks = SEQ // CHUNK
    seq = lambda b: (b, 0)
    return pl.pallas_call(
        _ssd_kernel, grid=(batch,),
        in_specs=[pl.BlockSpec((SEQ, CONV_CH), seq), pl.BlockSpec((SEQ, SSD_WIDTH), seq),
                  pl.BlockSpec((SEQ, DT_PAD), seq),
                  _const_spec(cw.shape), _const_spec(cb.shape), _const_spec(dtb.shape),
                  _const_spec(alog.shape), _const_spec(dsk.shape), _const_spec(nw.shape),
                  _const_spec(tri.shape)],
        out_specs=pl.BlockSpec((SEQ, SSD_WIDTH), seq),
        out_shape=jax.ShapeDtypeStruct((T, SSD_WIDTH), BF16),
        scratch_shapes=[pltpu.VMEM((SEQ, CONV_CH), F32),
                        pltpu.VMEM((SEQ, SSD_WIDTH), F32),
                        pltpu.VMEM((N_DIRS, D_STATE, SSD_WIDTH), F32),
                        pltpu.VMEM((n_chunks, SSD_GROUPS * D_STATE, CHUNK), F32),
                        pltpu.VMEM((n_chunks, N_DIRS * SSD_HEADS, CHUNK), F32)],
        name="ssd_bidir",
        compiler_params=pltpu.CompilerParams(dimension_semantics=("arbitrary",),
                                             vmem_limit_bytes=VMEM_LIMIT),
    )(xbc, z, dt, cw, cb, dtb, alog, dsk, nw, tri)


def _back_kernel(x1_ref, att_ref, ssd_ref, p_ref, woa_ref, wos_ref, wg_ref, wu_ref, wo_ref,
                 lng_ref, lnb_ref, wp_ref, wgate_ref, bgate_ref, o_ref):
    for r0 in range(0, x1_ref.shape[0], TOKEN_SUB):
        rows = slice(r0, r0 + TOKEN_SUB)
        mix = (jnp.dot(att_ref[rows, :], woa_ref[...], preferred_element_type=F32)
               + jnp.dot(ssd_ref[rows, :], wos_ref[...], preferred_element_type=F32))
        x2 = _layer_norm(ALPHA * x1_ref[rows, :] + mix, lng_ref[0:1, :], lnb_ref[0:1, :])
        f = _swiglu(x2.astype(BF16), wg_ref, wu_ref, wo_ref)
        x3 = _layer_norm(ALPHA * x2 + 0.5 * f, lng_ref[1:2, :], lnb_ref[1:2, :])
        e = jnp.dot(p_ref[rows, :].astype(BF16), wp_ref[...], preferred_element_type=F32)
        gate = _sigmoid(jnp.dot(x3.astype(BF16), wgate_ref[...], preferred_element_type=F32) + bgate_ref[...])
        o_ref[rows, :] = _layer_norm(ALPHA * x3 + gate * e, lng_ref[2:3, :], lnb_ref[2:3, :])


def _back_call(x1, att, ssd, p2d, woa, wos, wg, wu, wo, lng, lnb, wp, wgate, bgate):
    T = x1.shape[0]
    tm = TOKEN_TILE
    row = lambda i: (i, 0)
    consts = (woa, wos, wg, wu, wo, lng, lnb, wp, wgate, bgate)
    return pl.pallas_call(
        _back_kernel, grid=(T // tm,),
        in_specs=[pl.BlockSpec((tm, D_MODEL), row), pl.BlockSpec((tm, ATT_WIDTH), row),
                  pl.BlockSpec((tm, SSD_WIDTH), row), pl.BlockSpec((tm, PLE_DIM), row)]
                 + [_const_spec(c.shape) for c in consts],
        out_specs=pl.BlockSpec((tm, D_MODEL), row),
        out_shape=jax.ShapeDtypeStruct((T, D_MODEL), F32),
        name="back_outproj_ffn_ple",
        compiler_params=pltpu.CompilerParams(dimension_semantics=("arbitrary",),
                                             vmem_limit_bytes=VMEM_LIMIT),
    )(x1, att, ssd, p2d, *consts)


def _in_proj_columns():
    half_blk = HEAD_DIM // 2
    lanes = np.arange(LANES)
    blk, i = lanes // half_blk, lanes % half_blk
    q_cols = []
    for j in range(ATT_WIDTH // LANES):
        head = 2 * j + blk % 2
        q_cols.append(head * HEAD_DIM + 2 * i + blk // 2)
    k_base = ATT_WIDTH
    k_cols = [k_base + kv * HEAD_DIM + 2 * i + blk // 2 for kv in range(N_KV_HEADS)]
    v_base = k_base + N_KV_HEADS * HEAD_DIM
    v_cols = [v_base + kv * HEAD_DIM + lanes % HEAD_DIM for kv in range(N_KV_HEADS)]
    rest_base = v_base + N_KV_HEADS * HEAD_DIM
    rest = np.arange(rest_base, rest_base + SSD_WIDTH + CONV_CH + N_DIRS * SSD_HEADS)
    return np.concatenate(q_cols + k_cols + v_cols + [rest]).astype(np.int32)


def _rope_tiles():
    rows = SEQ // GRID_W
    row = jnp.repeat(jnp.arange(rows, dtype=F32), GRID_W)
    col = jnp.tile(jnp.arange(GRID_W, dtype=F32), rows)
    inv = ROPE_THETA ** (-jnp.arange(0, ROPE_AXIS_DIM, 2, dtype=F32) / ROPE_AXIS_DIM)
    ang = jnp.concatenate([row[:, None] * inv, col[:, None] * inv], axis=-1)
    cos, sin = jnp.cos(ang), jnp.sin(ang)
    cos_t = jnp.concatenate([cos, cos, cos, cos], axis=-1)
    sin_t = jnp.concatenate([-sin, -sin, sin, sin], axis=-1)
    return cos_t, sin_t


def _norm_tile(w):
    lanes = np.arange(LANES)
    idx = 2 * (lanes % (HEAD_DIM // 2)) + lanes // HALF_LANES
    return w[idx][None, :]


def kernel(x, p, ln_g, ln_b, ffn1_w_in, ffn1_w_out, w_in, q_norm, k_norm, conv_w, conv_b, dt_bias,
           a_log, d_skip, ssd_norm, w_out, ffn2_w_in, ffn2_w_out, ple_w, ple_gate_w, ple_gate_b):
    B, S, D = x.shape
    assert (S, D) == (SEQ, D_MODEL) and ln_g.shape[0] == DEPTH == 1
    T = B * S
    x2d = x.reshape(T, D)
    p2d = p[0].reshape(T, PLE_DIM)

    cos_t, sin_t = _rope_tiles()
    lanes = np.arange(LANES)
    same_head = ((lanes[:, None] // (HEAD_DIM // 2)) % 2) == ((lanes[None, :] // (HEAD_DIM // 2)) % 2)
    mq = jnp.asarray(np.where(same_head, 1.0 / HEAD_DIM, 0.0), BF16)
    mk = jnp.full((LANES, LANES), 1.0 / LANES, BF16)
    ones = np.ones((CHUNK, CHUNK), np.float32)
    tri = jnp.asarray(np.stack([np.tril(ones), np.triu(ones)]), BF16)

    i = 0
    wg1, wu1 = ffn1_w_in[i, :, :D_FF].astype(BF16), ffn1_w_in[i, :, D_FF:].astype(BF16)
    wo1 = ffn1_w_out[i].astype(BF16)
    wcat = jnp.concatenate([w_in[i][:, _in_proj_columns()],
                            jnp.zeros((D, DT_PAD - N_DIRS * SSD_HEADS), F32)], axis=1).astype(BF16)
    qw = _norm_tile(q_norm[i]) * (HEAD_DIM ** -0.5 * LOG2E)
    kw = _norm_tile(k_norm[i])
    x1, qh, kh, vh, z, xbc, dtr = _front_call(
        x2d, wg1, wu1, wo1, ln_g[i, 0:1], ln_b[i, 0:1], wcat, cos_t, sin_t, qw, kw, mq, mk)

    att = _attn_call(qh, kh, vh, B)

    dsk = jnp.repeat(d_skip[i], SSD_HEAD_DIM)[None, :]
    ssd = _ssd_call(xbc, z, dtr, conv_w[i], conv_b[i][None, :], dt_bias[i].reshape(-1, 1), a_log[i].reshape(-1, 1),
                    dsk, ssd_norm[i][None, :], tri, B)

    wg2, wu2 = ffn2_w_in[i, :, :D_FF].astype(BF16), ffn2_w_in[i, :, D_FF:].astype(BF16)
    out = _back_call(x1, att, ssd, p2d,
                     w_out[i, :ATT_WIDTH].astype(BF16), w_out[i, ATT_WIDTH:].astype(BF16),
                     wg2, wu2, ffn2_w_out[i].astype(BF16), ln_g[i, 1:4], ln_b[i, 1:4],
                     ple_w[i].astype(BF16), ple_gate_w[i].astype(BF16), ple_gate_b[i][None, :])
    return out.reshape(B, S, D)
```

```python
import functools
import math

import numpy as np
import jax
import jax.numpy as jnp
from jax import lax
from jax.experimental import pallas as pl
from jax.experimental.pallas import tpu as pltpu

F32 = jnp.float32
BF16 = jnp.bfloat16

D_MODEL = 1024
SEQ = 2048
DEPTH = 1
HEAD_DIM = 64
N_Q_HEADS = 8
N_KV_HEADS = 2
ROPE_AXIS_DIM = HEAD_DIM // 2
ROPE_THETA = 10000.0
GRID_W = 64
ATT_WIDTH = N_Q_HEADS * HEAD_DIM
SSD_WIDTH = 512
SSD_HEAD_DIM = 64
SSD_HEADS = 8
SSD_GROUPS = 2
D_STATE = 128
CONV_WIDTH = 5
CONV_CH = SSD_WIDTH + 2 * SSD_GROUPS * D_STATE
CHUNK = 128
N_DIRS = 2
D_FF = 2816
PLE_DIM = 256
ALPHA = (2.0 * DEPTH) ** 0.25
LN_EPS = 1e-5
RMS_EPS = 1e-6
LOG2E = 1.4426950408889634

LANES = 128
HALF_LANES = LANES // 2
TOKEN_TILE = 512
TOKEN_SUB = 512
Q_TILE = 512
FF_CHUNKS = ((0, 768), (768, 768), (1536, 768), (2304, 512))
DT_PAD = LANES
Q_OFF = 0
K_OFF = Q_OFF + ATT_WIDTH
V_OFF = K_OFF + N_KV_HEADS * LANES
Z_OFF = V_OFF + N_KV_HEADS * LANES
XBC_OFF = Z_OFF + SSD_WIDTH
DT_OFF = XBC_OFF + CONV_CH
IN_COLS = DT_OFF + DT_PAD
VMEM_LIMIT = 56 * 1024 * 1024


def _const_spec(shape):
    nd = len(shape)
    return pl.BlockSpec(shape, lambda *_: (0,) * nd, pipeline_mode=pl.Buffered(1))


def _layer_norm(y, g, b):
    mu = jnp.mean(y, axis=-1, keepdims=True)
    yc = y - mu
    var = jnp.mean(yc * yc, axis=-1, keepdims=True)
    return yc * lax.rsqrt(var + LN_EPS) * g + b


def _sigmoid(x):
    return 1.0 / (1.0 + jnp.exp(-x))


def _swiglu(xb, wg_ref, wu_ref, wo_ref):
    acc = None
    for c0, cw in FF_CHUNKS:
        hg = jnp.dot(xb, wg_ref[:, c0:c0 + cw], preferred_element_type=F32)
        hu = jnp.dot(xb, wu_ref[:, c0:c0 + cw], preferred_element_type=F32)
        act = (hg * _sigmoid(hg) * hu).astype(BF16)
        part = jnp.dot(act, wo_ref[c0:c0 + cw, :], preferred_element_type=F32)
        acc = part if acc is None else acc + part
    return acc


def _split2(x):
    hi = x.astype(BF16)
    lo = (x - hi.astype(F32)).astype(BF16)
    return hi, lo


def _split3(x):
    hi = x.astype(BF16)
    r1 = x - hi.astype(F32)
    mid = r1.astype(BF16)
    lo = (r1 - mid.astype(F32)).astype(BF16)
    return hi, mid, lo


def _norm_rope(t2, seg_mean, w, cos, sin):
    hi, lo = _split2(t2 * t2)
    ms = (jnp.dot(hi, seg_mean, preferred_element_type=F32)
          + jnp.dot(lo, seg_mean, preferred_element_type=F32))
    tn2 = t2 * lax.rsqrt(ms + RMS_EPS)
    outs = []
    for j in range(2):
        tn = tn2[:, j * LANES:(j + 1) * LANES] * w
        outs.append(tn * cos + pltpu.roll(tn, HALF_LANES, axis=1) * sin)
    return outs


def _front_kernel(x_ref, wg_ref, wu_ref, wo_ref, g_ref, b_ref, win_ref,
                  cos_ref, sin_ref, qw_ref, kw_ref, mq_ref, mk_ref,
                  x1_ref, q_ref, k_ref, v_ref, z_ref, xbc_ref, dt_ref):
    mq = mq_ref[...]
    mk = mk_ref[...]
    for r0 in range(0, x_ref.shape[0], TOKEN_SUB):
        rows = slice(r0, r0 + TOKEN_SUB)
        x = x_ref[rows, :]
        f = _swiglu(x.astype(BF16), wg_ref, wu_ref, wo_ref)
        x1 = _layer_norm(ALPHA * x + 0.5 * f, g_ref[...], b_ref[...])
        x1_ref[rows, :] = x1
        u = jnp.dot(x1.astype(BF16), win_ref[...], preferred_element_type=F32)
        cos = cos_ref[rows, :]
        sin = sin_ref[rows, :]
        pair = 2 * LANES
        for jp in range(ATT_WIDTH // pair):
            tiles = _norm_rope(u[:, Q_OFF + jp * pair:Q_OFF + (jp + 1) * pair], mq, qw_ref[...], cos, sin)
            for j, tile in enumerate(tiles):
                c0 = jp * pair + j * LANES
                q_ref[rows, c0:c0 + LANES] = tile.astype(BF16)
        tiles = _norm_rope(u[:, K_OFF:K_OFF + pair], mk, kw_ref[...], cos, sin)
        for j, tile in enumerate(tiles):
            k_ref[rows, j * LANES:(j + 1) * LANES] = tile.astype(BF16)
        v_ref[rows, :] = u[:, V_OFF:Z_OFF].astype(BF16)
        z_ref[rows, :] = u[:, Z_OFF:XBC_OFF]
        xbc_ref[rows, :] = u[:, XBC_OFF:DT_OFF]
        dt_ref[rows, :] = u[:, DT_OFF:IN_COLS]


def _front_call(x2d, wg, wu, wo, g, b, wcat, cos_t, sin_t, qw, kw, mq, mk):
    T = x2d.shape[0]
    tm = TOKEN_TILE
    pos_blocks = SEQ // tm
    row = lambda i: (i, 0)
    pos = lambda i: (i % pos_blocks, 0)
    out_shape = (
        jax.ShapeDtypeStruct((T, D_MODEL), F32),
        jax.ShapeDtypeStruct((T, ATT_WIDTH), BF16),
        jax.ShapeDtypeStruct((T, N_KV_HEADS * LANES), BF16),
        jax.ShapeDtypeStruct((T, N_KV_HEADS * LANES), BF16),
        jax.ShapeDtypeStruct((T, SSD_WIDTH), F32),
        jax.ShapeDtypeStruct((T, CONV_CH), F32),
        jax.ShapeDtypeStruct((T, DT_PAD), F32),
    )
    in_specs = [
        pl.BlockSpec((tm, D_MODEL), row),
        _const_spec(wg.shape), _const_spec(wu.shape), _const_spec(wo.shape),
        _const_spec(g.shape), _const_spec(b.shape), _const_spec(wcat.shape),
        pl.BlockSpec((tm, LANES), pos), pl.BlockSpec((tm, LANES), pos),
        _const_spec(qw.shape), _const_spec(kw.shape), _const_spec(mq.shape), _const_spec(mk.shape),
    ]
    out_specs = tuple(pl.BlockSpec((tm, s.shape[1]), row) for s in out_shape)
    return pl.pallas_call(
        _front_kernel, grid=(T // tm,), in_specs=in_specs, out_specs=out_specs, out_shape=out_shape,
        name="front_ffn_inproj",
        compiler_params=pltpu.CompilerParams(dimension_semantics=("arbitrary",),
                                             vmem_limit_bytes=VMEM_LIMIT),
    )(x2d, wg, wu, wo, g, b, wcat, cos_t, sin_t, qw, kw, mq, mk)


def _attn_kernel(q_ref, k_ref, v_ref, o_ref):
    tq = q_ref.shape[0]
    lane = lax.broadcasted_iota(jnp.int32, (1, LANES), 1)
    low = lane < HALF_LANES
    q_per_kv_tiles = (N_Q_HEADS // N_KV_HEADS) * HEAD_DIM // LANES
    for kv in range(N_KV_HEADS):
        kt = k_ref[:, kv * LANES:(kv + 1) * LANES]
        vt = v_ref[:, kv * LANES:(kv + 1) * LANES]
        zero = jnp.zeros_like(vt)
        v_lo = jnp.where(low, vt, zero)
        v_hi = jnp.where(low, zero, vt)
        blk = (lane // (HEAD_DIM // 2)) % 2
        for jj in range(q_per_kv_tiles):
            j = kv * q_per_kv_tiles + jj
            qt = q_ref[:, j * LANES:(j + 1) * LANES]
            zq = jnp.zeros_like(qt)
            out = None
            for a, v_a in ((0, v_lo), (1, v_hi)):
                qa = jnp.where(blk == a, qt, zq)
                s = lax.dot_general(qa, kt, (((1,), (1,)), ((), ())), preferred_element_type=F32)
                m = jnp.max(s, axis=-1, keepdims=True)
                p = jnp.exp2(s - m)
                l = jnp.sum(p, axis=-1, keepdims=True)
                o = jnp.dot(p.astype(BF16), v_a, preferred_element_type=F32) * (1.0 / l)
                out = o if out is None else out + o
            o_ref[:, j * LANES:(j + 1) * LANES] = out.astype(BF16)


def _attn_call(q, k, v, batch):
    T = q.shape[0]
    nq = SEQ // Q_TILE
    return pl.pallas_call(
        _attn_kernel, grid=(batch, nq),
        in_specs=[pl.BlockSpec((Q_TILE, ATT_WIDTH), lambda b, i: (b * nq + i, 0)),
                  pl.BlockSpec((SEQ, N_KV_HEADS * LANES), lambda b, i: (b, 0)),
                  pl.BlockSpec((SEQ, N_KV_HEADS * LANES), lambda b, i: (b, 0))],
        out_specs=pl.BlockSpec((Q_TILE, ATT_WIDTH), lambda b, i: (b * nq + i, 0)),
        out_shape=jax.ShapeDtypeStruct((T, ATT_WIDTH), BF16),
        name="attention",
        compiler_params=pltpu.CompilerParams(dimension_semantics=("arbitrary", "arbitrary"),
                                             vmem_limit_bytes=VMEM_LIMIT),
    )(q, k, v)


def _softplus(x):
    return jnp.maximum(x, 0.0) + jnp.log1p(jnp.exp(-jnp.abs(x)))


def _dot3(a_pieces, b, dims):
    out = None
    for piece in a_pieces:
        lhs, rhs = (piece, b) if dims == "piece_lhs" else (b, piece)
        contract = (((1,), (0,)), ((), ())) if dims == "piece_lhs" else (((1,), (1,)), ((), ()))
        term = lax.dot_general(lhs, rhs, contract, preferred_element_type=F32)
        out = term if out is None else out + term
    return out


def _conv_tile(xin, w_ref, b_ref, cols, pad_lo, pad_hi):
    R = xin.shape[0]
    half = CONV_WIDTH // 2
    t_idx = lax.broadcasted_iota(jnp.int32, (R, 1), 0)
    acc = b_ref[:, cols] + w_ref[half:half + 1, cols] * xin
    for j in range(CONV_WIDTH):
        off = j - half
        if off == 0:
            continue
        rolled = pltpu.roll(xin, (-off) % R, axis=0)
        if off < 0 and pad_lo:
            rolled = jnp.where(t_idx >= -off, rolled, 0.0)
        if off > 0 and pad_hi:
            rolled = jnp.where(t_idx < R - off, rolled, 0.0)
        acc = acc + w_ref[j:j + 1, cols] * rolled
    return acc * _sigmoid(acc)


def _ssd_kernel(xbc_ref, z_ref, dt_ref, cw_ref, cb_ref, dtb_ref, alog_ref, dsk_ref, nw_ref, tri_ref,
                o_ref, xc_ref, ylo_ref, yhi_ref, st_ref, bt_ref, dtt_ref, cumrow_ref, cumcol_ref, gm_ref):
    y_refs = (ylo_ref, yhi_ref)
    S = xbc_ref.shape[0]
    n_chunks = S // CHUNK
    pair_tiles = SSD_WIDTH // LANES
    tiles_per_group = pair_tiles // SSD_GROUPS
    n_heads_all = N_DIRS * SSD_HEADS
    edge = 2 * 8
    b_off = SSD_WIDTH
    c_off = SSD_WIDTH + SSD_GROUPS * D_STATE
    lane = lax.broadcasted_iota(jnp.int32, (1, LANES), 1)
    low = lane < HALF_LANES

    for c in range(CONV_CH // LANES):
        cols = slice(c * LANES, (c + 1) * LANES)
        xc_ref[:, cols] = _conv_tile(xbc_ref[:, cols], cw_ref, cb_ref, cols, False, False)
        xc_ref[0:edge // 2, cols] = _conv_tile(xbc_ref[0:edge, cols], cw_ref, cb_ref, cols, True, False)[0:edge // 2]
        xc_ref[S - edge // 2:S, cols] = _conv_tile(xbc_ref[S - edge:S, cols], cw_ref, cb_ref, cols,
                                                   False, True)[edge // 2:edge]

    for y_half in y_refs:
        y_half[...] = jnp.zeros_like(y_half)
    st_ref[...] = jnp.zeros_like(st_ref)

    a_col = -jnp.exp(alog_ref[...])
    dtb_col = dtb_ref[...]
    r_i = lax.broadcasted_iota(jnp.int32, (CHUNK, CHUNK), 0)
    c_i = lax.broadcasted_iota(jnp.int32, (CHUNK, CHUNK), 1)
    masks = (c_i <= r_i, c_i >= r_i)
    zero_rows = jnp.zeros((CHUNK - n_heads_all, CHUNK), BF16)
    fwd_rows = lax.broadcasted_iota(jnp.int32, (n_heads_all, 1), 0) < SSD_HEADS
    fwd_lanes = lane < SSD_HEADS

    def prep_chunk(c):
        rows = pl.ds(pl.multiple_of(c * CHUNK, CHUNK), CHUNK)
        dt_all = _softplus(dt_ref[rows, :].T[0:n_heads_all, :] + dtb_col)
        dtt_ref[c] = dt_all
        pieces = _split3(dt_all * a_col)
        padded = [jnp.concatenate([pc, zero_rows], axis=0) for pc in pieces]
        cumrow_ref[c] = jnp.where(fwd_rows, _dot3(pieces, tri_ref[1], "piece_lhs"),
                                  _dot3(pieces, tri_ref[0], "piece_lhs"))
        cumcol_ref[c] = jnp.where(fwd_lanes, _dot3(padded, tri_ref[0], "piece_rhs"),
                                  _dot3(padded, tri_ref[1], "piece_rhs"))
        for g in range(SSD_GROUPS):
            bt = xc_ref[rows, b_off + g * D_STATE:b_off + (g + 1) * D_STATE].T
            bt_ref[c, g * D_STATE:(g + 1) * D_STATE, :] = bt
            c_b = xc_ref[rows, c_off + g * D_STATE:c_off + (g + 1) * D_STATE].astype(BF16)
            gmat = jnp.dot(c_b, bt.astype(BF16), preferred_element_type=F32)
            for d in range(N_DIRS):
                k = d * SSD_GROUPS + g
                gm_ref[c, k * CHUNK:(k + 1) * CHUNK, :] = jnp.where(masks[d], gmat, 0.0).astype(BF16)

    prep_unroll = 4

    def prep(i, carry):
        for u in range(prep_unroll):
            prep_chunk(i * prep_unroll + u)
        return carry

    lax.fori_loop(0, n_chunks // prep_unroll, prep, 0)

    def one_chunk(c, d, y_half, c_local):
        rows = pl.ds(pl.multiple_of(c * CHUNK, CHUNK), CHUNK)
        rows_local = pl.ds(pl.multiple_of(c_local * CHUNK, CHUNK), CHUNK)
        hs = slice(d * SSD_HEADS, (d + 1) * SSD_HEADS)
        dt_d = dtt_ref[c][hs]
        cum_row = cumrow_ref[c][hs]
        cum_col = cumcol_ref[c]
        tot = jnp.sum(dt_d * a_col[hs], axis=1, keepdims=True)
        w_row = dt_d * jnp.exp(tot - cum_row)
        for g in range(SSD_GROUPS):
            c_b = xc_ref[rows, c_off + g * D_STATE:c_off + (g + 1) * D_STATE].astype(BF16)
            bt_g = bt_ref[c, g * D_STATE:(g + 1) * D_STATE, :]
            k = d * SSD_GROUPS + g
            gm_b = gm_ref[c, k * CHUNK:(k + 1) * CHUNK, :]
            for tt in range(tiles_per_group):
                t = g * tiles_per_group + tt
                cols = slice(t * LANES, (t + 1) * LANES)
                x_t = xc_ref[rows, cols].astype(BF16)
                st = st_ref[d, :, cols]
                rhs = jnp.concatenate([x_t, st.astype(BF16)], axis=0)
                ys, sn = [], []
                for hh in (2 * t, 2 * t + 1):
                    col = d * SSD_HEADS + hh
                    colb = jnp.broadcast_to(cum_col[:, col:col + 1], (CHUNK, CHUNK))
                    seg = jnp.minimum(colb - cum_row[hh:hh + 1, :], 0.0)
                    w = (jnp.exp(seg) * dt_d[hh:hh + 1, :]).astype(BF16) * gm_b
                    a = jnp.exp(colb).astype(BF16) * c_b
                    lhs = jnp.concatenate([w, a], axis=1)
                    ys.append(jnp.dot(lhs, rhs, preferred_element_type=F32))
                    bts = (bt_g * w_row[hh:hh + 1, :]).astype(BF16)
                    sn.append(jnp.dot(bts, x_t, preferred_element_type=F32))
                y_half[rows_local, cols] = y_half[rows_local, cols] + jnp.where(low, ys[0], ys[1])
                decay = jnp.exp(jnp.where(low, tot[2 * t:2 * t + 1, :], tot[2 * t + 1:2 * t + 2, :]))
                st_ref[d, :, cols] = st * decay + jnp.where(low, sn[0], sn[1])

    half_chunks = n_chunks // 2

    def body_first(c, carry):
        one_chunk(c, 0, y_refs[0], c)
        one_chunk(n_chunks - 1 - c, 1, y_refs[1], half_chunks - 1 - c)
        return carry

    def body_second(c, carry):
        one_chunk(c, 0, y_refs[1], c - half_chunks)
        one_chunk(n_chunks - 1 - c, 1, y_refs[0], n_chunks - 1 - c)
        return carry

    lax.fori_loop(0, half_chunks, body_first, 0)
    lax.fori_loop(half_chunks, n_chunks, body_second, 0)

    group_w = SSD_WIDTH // SSD_GROUPS
    for half, y_half in enumerate(y_refs):
        def fin(c, carry, half=half, y_half=y_half):
            rows = pl.ds(pl.multiple_of((half * half_chunks + c) * CHUNK, CHUNK), CHUNK)
            rows_local = pl.ds(pl.multiple_of(c * CHUNK, CHUNK), CHUNK)
            zz = z_ref[rows, :]
            y = (y_half[rows_local, :] + dsk_ref[...] * xc_ref[rows, 0:SSD_WIDTH]) * (zz * _sigmoid(zz))
            outs = []
            for g in range(SSD_GROUPS):
                yg = y[:, g * group_w:(g + 1) * group_w]
                ms = jnp.mean(yg * yg, axis=-1, keepdims=True)
                outs.append(yg * lax.rsqrt(ms + RMS_EPS))
            o_ref[rows, :] = (jnp.concatenate(outs, axis=-1) * nw_ref[...]).astype(BF16)
            return carry

        lax.fori_loop(0, half_chunks, fin, 0)


def _ssd_call(xbc, z, dt, cw, cb, dtb, alog, dsk, nw, tri, batch):
    T = xbc.shape[0]
    n_chunks = SEQ // CHUNK
    seq = lambda b: (b, 0)
    return pl.pallas_call(
        _ssd_kernel, grid=(batch,),
        in_specs=[pl.BlockSpec((SEQ, CONV_CH), seq), pl.BlockSpec((SEQ, SSD_WIDTH), seq),
                  pl.BlockSpec((SEQ, DT_PAD), seq),
                  _const_spec(cw.shape), _const_spec(cb.shape), _const_spec(dtb.shape),
                  _const_spec(alog.shape), _const_spec(dsk.shape), _const_spec(nw.shape),
                  _const_spec(tri.shape)],
        out_specs=pl.BlockSpec((SEQ, SSD_WIDTH), seq),
        out_shape=jax.ShapeDtypeStruct((T, SSD_WIDTH), BF16),
        scratch_shapes=[pltpu.VMEM((SEQ, CONV_CH), F32),
                        pltpu.VMEM((SEQ // 2, SSD_WIDTH), F32),
                        pltpu.VMEM((SEQ // 2, SSD_WIDTH), F32),
                        pltpu.VMEM((N_DIRS, D_STATE, SSD_WIDTH), F32),
                        pltpu.VMEM((n_chunks, SSD_GROUPS * D_STATE, CHUNK), F32),
                        pltpu.VMEM((n_chunks, N_DIRS * SSD_HEADS, CHUNK), F32),
                        pltpu.VMEM((n_chunks, N_DIRS * SSD_HEADS, CHUNK), F32),
                        pltpu.VMEM((n_chunks, CHUNK, LANES), F32),
                        pltpu.VMEM((n_chunks, N_DIRS * SSD_GROUPS * CHUNK, CHUNK), BF16)],
        name="ssd_bidir",
        compiler_params=pltpu.CompilerParams(dimension_semantics=("arbitrary",),
                                             vmem_limit_bytes=VMEM_LIMIT),
    )(xbc, z, dt, cw, cb, dtb, alog, dsk, nw, tri)


def _back_kernel(x1_ref, att_ref, ssd_ref, p_ref, woa_ref, wos_ref, wg_ref, wu_ref, wo_ref,
                 lng_ref, lnb_ref, wp_ref, wgate_ref, bgate_ref, o_ref):
    for r0 in range(0, x1_ref.shape[0], TOKEN_SUB):
        rows = slice(r0, r0 + TOKEN_SUB)
        mix = (jnp.dot(att_ref[rows, :], woa_ref[...], preferred_element_type=F32)
               + jnp.dot(ssd_ref[rows, :], wos_ref[...], preferred_element_type=F32))
        x2 = _layer_norm(ALPHA * x1_ref[rows, :] + mix, lng_ref[0:1, :], lnb_ref[0:1, :])
        f = _swiglu(x2.astype(BF16), wg_ref, wu_ref, wo_ref)
        x3 = _layer_norm(ALPHA * x2 + 0.5 * f, lng_ref[1:2, :], lnb_ref[1:2, :])
        e = jnp.dot(p_ref[rows, :].astype(BF16), wp_ref[...], preferred_element_type=F32)
        gate = _sigmoid(jnp.dot(x3.astype(BF16), wgate_ref[...], preferred_element_type=F32) + bgate_ref[...])
        o_ref[rows, :] = _layer_norm(ALPHA * x3 + gate * e, lng_ref[2:3, :], lnb_ref[2:3, :])


def _back_call(x1, att, ssd, p2d, woa, wos, wg, wu, wo, lng, lnb, wp, wgate, bgate):
    T = x1.shape[0]
    tm = TOKEN_TILE
    row = lambda i: (i, 0)
    consts = (woa, wos, wg, wu, wo, lng, lnb, wp, wgate, bgate)
    return pl.pallas_call(
        _back_kernel, grid=(T // tm,),
        in_specs=[pl.BlockSpec((tm, D_MODEL), row), pl.BlockSpec((tm, ATT_WIDTH), row),
                  pl.BlockSpec((tm, SSD_WIDTH), row), pl.BlockSpec((tm, PLE_DIM), row)]
                 + [_const_spec(c.shape) for c in consts],
        out_specs=pl.BlockSpec((tm, D_MODEL), row),
        out_shape=jax.ShapeDtypeStruct((T, D_MODEL), F32),
        name="back_outproj_ffn_ple",
        compiler_params=pltpu.CompilerParams(dimension_semantics=("arbitrary",),
                                             vmem_limit_bytes=VMEM_LIMIT),
    )(x1, att, ssd, p2d, *consts)


def _in_proj_columns():
    half_blk = HEAD_DIM // 2
    lanes = np.arange(LANES)
    blk, i = lanes // half_blk, lanes % half_blk
    q_cols = []
    for j in range(ATT_WIDTH // LANES):
        head = 2 * j + blk % 2
        q_cols.append(head * HEAD_DIM + 2 * i + blk // 2)
    k_base = ATT_WIDTH
    k_cols = [k_base + kv * HEAD_DIM + 2 * i + blk // 2 for kv in range(N_KV_HEADS)]
    v_base = k_base + N_KV_HEADS * HEAD_DIM
    v_cols = [v_base + kv * HEAD_DIM + lanes % HEAD_DIM for kv in range(N_KV_HEADS)]
    rest_base = v_base + N_KV_HEADS * HEAD_DIM
    rest = np.arange(rest_base, rest_base + SSD_WIDTH + CONV_CH + N_DIRS * SSD_HEADS)
    return np.concatenate(q_cols + k_cols + v_cols + [rest]).astype(np.int32)


def _rope_tiles():
    rows = SEQ // GRID_W
    row = jnp.repeat(jnp.arange(rows, dtype=F32), GRID_W)
    col = jnp.tile(jnp.arange(GRID_W, dtype=F32), rows)
    inv = ROPE_THETA ** (-jnp.arange(0, ROPE_AXIS_DIM, 2, dtype=F32) / ROPE_AXIS_DIM)
    ang = jnp.concatenate([row[:, None] * inv, col[:, None] * inv], axis=-1)
    cos, sin = jnp.cos(ang), jnp.sin(ang)
    cos_t = jnp.concatenate([cos, cos, cos, cos], axis=-1)
    sin_t = jnp.concatenate([-sin, -sin, sin, sin], axis=-1)
    return cos_t, sin_t


def _norm_tile(w):
    lanes = np.arange(LANES)
    idx = 2 * (lanes % (HEAD_DIM // 2)) + lanes // HALF_LANES
    return w[idx][None, :]


def kernel(x, p, ln_g, ln_b, ffn1_w_in, ffn1_w_out, w_in, q_norm, k_norm, conv_w, conv_b, dt_bias,
           a_log, d_skip, ssd_norm, w_out, ffn2_w_in, ffn2_w_out, ple_w, ple_gate_w, ple_gate_b):
    B, S, D = x.shape
    assert (S, D) == (SEQ, D_MODEL) and ln_g.shape[0] == DEPTH == 1
    T = B * S
    x2d = x.reshape(T, D)
    p2d = p[0].reshape(T, PLE_DIM)

    cos_t, sin_t = _rope_tiles()
    lanes = np.arange(LANES)
    same_head = ((lanes[:, None] // (HEAD_DIM // 2)) % 2) == ((lanes[None, :] // (HEAD_DIM // 2)) % 2)
    two_tiles = np.eye(2, dtype=np.float32)
    mq = jnp.asarray(np.kron(two_tiles, np.where(same_head, 1.0 / HEAD_DIM, 0.0)), BF16)
    mk = jnp.asarray(np.kron(two_tiles, np.full((LANES, LANES), 1.0 / LANES)), BF16)
    ones = np.ones((CHUNK, CHUNK), np.float32)
    tri = jnp.asarray(np.stack([np.tril(ones), np.triu(ones)]), BF16)

    i = 0
    wg1, wu1 = ffn1_w_in[i, :, :D_FF].astype(BF16), ffn1_w_in[i, :, D_FF:].astype(BF16)
    wo1 = ffn1_w_out[i].astype(BF16)
    wcat = jnp.concatenate([w_in[i][:, _in_proj_columns()],
                            jnp.zeros((D, DT_PAD - N_DIRS * SSD_HEADS), F32)], axis=1).astype(BF16)
    qw = _norm_tile(q_norm[i]) * (HEAD_DIM ** -0.5 * LOG2E)
    kw = _norm_tile(k_norm[i])
    x1, qh, kh, vh, z, xbc, dtr = _front_call(
        x2d, wg1, wu1, wo1, ln_g[i, 0:1], ln_b[i, 0:1], wcat, cos_t, sin_t, qw, kw, mq, mk)

    att = _attn_call(qh, kh, vh, B)

    dsk = jnp.repeat(d_skip[i], SSD_HEAD_DIM)[None, :]
    ssd = _ssd_call(xbc, z, dtr, conv_w[i], conv_b[i][None, :], dt_bias[i].reshape(-1, 1), a_log[i].reshape(-1, 1),
                    dsk, ssd_norm[i][None, :], tri, B)

    wg2, wu2 = ffn2_w_in[i, :, :D_FF].astype(BF16), ffn2_w_in[i, :, D_FF:].astype(BF16)
    out = _back_call(x1, att, ssd, p2d,
                     w_out[i, :ATT_WIDTH].astype(BF16), w_out[i, ATT_WIDTH:].astype(BF16),
                     wg2, wu2, ffn2_w_out[i].astype(BF16), ln_g[i, 1:4], ln_b[i, 1:4],
                     ple_w[i].astype(BF16), ple_gate_w[i].astype(BF16), ple_gate_b[i][None, :])
    return out.reshape(B, S, D)
```

```python
import functools
import math

import numpy as np
import jax
import jax.numpy as jnp
from jax import lax
from jax.experimental import pallas as pl
from jax.experimental.pallas import tpu as pltpu

F32 = jnp.float32
BF16 = jnp.bfloat16

D_MODEL = 1024
SEQ = 2048
DEPTH = 1
HEAD_DIM = 64
N_Q_HEADS = 8
N_KV_HEADS = 2
ROPE_AXIS_DIM = HEAD_DIM // 2
ROPE_THETA = 10000.0
GRID_W = 64
ATT_WIDTH = N_Q_HEADS * HEAD_DIM
SSD_WIDTH = 512
SSD_HEAD_DIM = 64
SSD_HEADS = 8
SSD_GROUPS = 2
D_STATE = 128
CONV_WIDTH = 5
CONV_CH = SSD_WIDTH + 2 * SSD_GROUPS * D_STATE
CHUNK = 128
N_DIRS = 2
D_FF = 2816
PLE_DIM = 256
ALPHA = (2.0 * DEPTH) ** 0.25
LN_EPS = 1e-5
RMS_EPS = 1e-6
LOG2E = 1.4426950408889634

LANES = 128
HALF_LANES = LANES // 2
TOKEN_TILE = 512
TOKEN_SUB = 512
Q_TILE = 1024
Q_UNIT = 512
FF_CHUNKS = ((0, 768), (768, 768), (1536, 768), (2304, 512))
DT_PAD = LANES
Q_OFF = 0
K_OFF = Q_OFF + ATT_WIDTH
V_OFF = K_OFF + N_KV_HEADS * LANES
Z_OFF = V_OFF + N_KV_HEADS * LANES
XBC_OFF = Z_OFF + SSD_WIDTH
DT_OFF = XBC_OFF + CONV_CH
IN_COLS = DT_OFF + DT_PAD
VMEM_LIMIT = 56 * 1024 * 1024


def _const_spec(shape):
    nd = len(shape)
    return pl.BlockSpec(shape, lambda *_: (0,) * nd, pipeline_mode=pl.Buffered(1))


def _layer_norm(y, g, b):
    mu = jnp.mean(y, axis=-1, keepdims=True)
    yc = y - mu
    var = jnp.mean(yc * yc, axis=-1, keepdims=True)
    return yc * lax.rsqrt(var + LN_EPS) * g + b


def _sigmoid(x):
    return 1.0 / (1.0 + jnp.exp(-x))


def _swiglu(xb, wgu_ref, wo_ref):
    acc = None
    for c0, cw in FF_CHUNKS:
        hg = jnp.dot(xb, wgu_ref[:, c0:c0 + cw], preferred_element_type=F32)
        hu = jnp.dot(xb, wgu_ref[:, D_FF + c0:D_FF + c0 + cw], preferred_element_type=F32)
        act = (hg * _sigmoid(hg) * hu).astype(BF16)
        part = jnp.dot(act, wo_ref[c0:c0 + cw, :], preferred_element_type=F32)
        acc = part if acc is None else acc + part
    return acc


def _split2(x):
    hi = x.astype(BF16)
    lo = (x - hi.astype(F32)).astype(BF16)
    return hi, lo


def _split3(x):
    hi = x.astype(BF16)
    r1 = x - hi.astype(F32)
    mid = r1.astype(BF16)
    lo = (r1 - mid.astype(F32)).astype(BF16)
    return hi, mid, lo


def _norm_rope(t2, seg_mean, w, cos, sin):
    hi, lo = _split2(t2 * t2)
    ms = (jnp.dot(hi, seg_mean, preferred_element_type=F32)
          + jnp.dot(lo, seg_mean, preferred_element_type=F32))
    tn2 = t2 * lax.rsqrt(ms + RMS_EPS)
    outs = []
    for j in range(2):
        tn = tn2[:, j * LANES:(j + 1) * LANES] * w
        outs.append(tn * cos + pltpu.roll(tn, HALF_LANES, axis=1) * sin)
    return outs


def _front_kernel(x_ref, wgu_ref, wo_ref, g_ref, b_ref, win_ref,
                  cos_ref, sin_ref, qw_ref, kw_ref, mq_ref, mk_ref,
                  x1_ref, q_ref, k_ref, v_ref, z_ref, xbc_ref, dt_ref):
    mq = mq_ref[...]
    mk = mk_ref[...]
    for r0 in range(0, x_ref.shape[0], TOKEN_SUB):
        rows = slice(r0, r0 + TOKEN_SUB)
        x = x_ref[rows, :]
        f = _swiglu(x.astype(BF16), wgu_ref, wo_ref)
        x1 = _layer_norm(ALPHA * x + 0.5 * f, g_ref[...], b_ref[...])
        x1_ref[rows, :] = x1
        u = jnp.dot(x1.astype(BF16), win_ref[...], preferred_element_type=F32)
        cos = cos_ref[rows, :]
        sin = sin_ref[rows, :]
        pair = 2 * LANES
        for jp in range(ATT_WIDTH // pair):
            tiles = _norm_rope(u[:, Q_OFF + jp * pair:Q_OFF + (jp + 1) * pair], mq, qw_ref[...], cos, sin)
            for j, tile in enumerate(tiles):
                c0 = jp * pair + j * LANES
                q_ref[rows, c0:c0 + LANES] = tile.astype(BF16)
        tiles = _norm_rope(u[:, K_OFF:K_OFF + pair], mk, kw_ref[...], cos, sin)
        for j, tile in enumerate(tiles):
            k_ref[rows, j * LANES:(j + 1) * LANES] = tile.astype(BF16)
        v_ref[rows, :] = u[:, V_OFF:Z_OFF].astype(BF16)
        z_ref[rows, :] = u[:, Z_OFF:XBC_OFF]
        xbc_ref[rows, :] = u[:, XBC_OFF:DT_OFF]
        dt_ref[rows, :] = u[:, DT_OFF:IN_COLS]


def _front_call(x2d, wgu, wo, g, b, wcat, cos_t, sin_t, qw, kw, mq, mk):
    T = x2d.shape[0]
    tm = TOKEN_TILE
    pos_blocks = SEQ // tm
    row = lambda i: (i, 0)
    pos = lambda i: (i % pos_blocks, 0)
    out_shape = (
        jax.ShapeDtypeStruct((T, D_MODEL), F32),
        jax.ShapeDtypeStruct((T, ATT_WIDTH), BF16),
        jax.ShapeDtypeStruct((T, N_KV_HEADS * LANES), BF16),
        jax.ShapeDtypeStruct((T, N_KV_HEADS * LANES), BF16),
        jax.ShapeDtypeStruct((T, SSD_WIDTH), F32),
        jax.ShapeDtypeStruct((T, CONV_CH), F32),
        jax.ShapeDtypeStruct((T, DT_PAD), F32),
    )
    in_specs = [
        pl.BlockSpec((tm, D_MODEL), row),
        _const_spec(wgu.shape), _const_spec(wo.shape),
        _const_spec(g.shape), _const_spec(b.shape), _const_spec(wcat.shape),
        pl.BlockSpec((tm, LANES), pos), pl.BlockSpec((tm, LANES), pos),
        _const_spec(qw.shape), _const_spec(kw.shape), _const_spec(mq.shape), _const_spec(mk.shape),
    ]
    out_specs = tuple(pl.BlockSpec((tm, s.shape[1]), row) for s in out_shape)
    return pl.pallas_call(
        _front_kernel, grid=(T // tm,), in_specs=in_specs, out_specs=out_specs, out_shape=out_shape,
        name="front_ffn_inproj",
        compiler_params=pltpu.CompilerParams(dimension_semantics=("arbitrary",),
                                             vmem_limit_bytes=VMEM_LIMIT),
    )(x2d, wgu, wo, g, b, wcat, cos_t, sin_t, qw, kw, mq, mk)


def _attn_kernel(q_ref, k_ref, v_ref, o_ref):
    tq = q_ref.shape[0]
    lane = lax.broadcasted_iota(jnp.int32, (1, LANES), 1)
    low = lane < HALF_LANES
    q_per_kv_tiles = (N_Q_HEADS // N_KV_HEADS) * HEAD_DIM // LANES
    for kv in range(N_KV_HEADS):
        kt = k_ref[:, kv * LANES:(kv + 1) * LANES]
        vt = v_ref[:, kv * LANES:(kv + 1) * LANES]
        zero = jnp.zeros_like(vt)
        v_lo = jnp.where(low, vt, zero)
        v_hi = jnp.where(low, zero, vt)
        blk = (lane // (HEAD_DIM // 2)) % 2
        for jj in range(q_per_kv_tiles):
            j = kv * q_per_kv_tiles + jj
            cols = slice(j * LANES, (j + 1) * LANES)
            for r0 in range(0, tq, Q_UNIT):
                rows = slice(r0, r0 + Q_UNIT)
                qt = q_ref[rows, cols]
                zq = jnp.zeros_like(qt)
                out = None
                for a, v_a in ((0, v_lo), (1, v_hi)):
                    qa = jnp.where(blk == a, qt, zq)
                    s = lax.dot_general(qa, kt, (((1,), (1,)), ((), ())), preferred_element_type=F32)
                    m = jnp.max(s, axis=-1, keepdims=True)
                    p = jnp.exp2(s - m)
                    l = jnp.sum(p, axis=-1, keepdims=True)
                    o = jnp.dot(p.astype(BF16), v_a, preferred_element_type=F32) * (1.0 / l)
                    out = o if out is None else out + o
                o_ref[rows, cols] = out.astype(BF16)


def _attn_call(q, k, v, batch):
    T = q.shape[0]
    nq = SEQ // Q_TILE
    return pl.pallas_call(
        _attn_kernel, grid=(batch, nq),
        in_specs=[pl.BlockSpec((Q_TILE, ATT_WIDTH), lambda b, i: (b * nq + i, 0)),
                  pl.BlockSpec((SEQ, N_KV_HEADS * LANES), lambda b, i: (b, 0)),
                  pl.BlockSpec((SEQ, N_KV_HEADS * LANES), lambda b, i: (b, 0))],
        out_specs=pl.BlockSpec((Q_TILE, ATT_WIDTH), lambda b, i: (b * nq + i, 0)),
        out_shape=jax.ShapeDtypeStruct((T, ATT_WIDTH), BF16),
        name="attention",
        compiler_params=pltpu.CompilerParams(dimension_semantics=("arbitrary", "arbitrary"),
                                             vmem_limit_bytes=VMEM_LIMIT),
    )(q, k, v)


def _softplus(x):
    return jnp.maximum(x, 0.0) + jnp.log1p(jnp.exp(-jnp.abs(x)))


def _dot3(a_pieces, b, dims):
    out = None
    for piece in a_pieces:
        lhs, rhs = (piece, b) if dims == "piece_lhs" else (b, piece)
        contract = (((1,), (0,)), ((), ())) if dims == "piece_lhs" else (((1,), (1,)), ((), ()))
        term = lax.dot_general(lhs, rhs, contract, preferred_element_type=F32)
        out = term if out is None else out + term
    return out


def _conv_tile(xin, w_ref, b_ref, cols, pad_lo, pad_hi):
    R = xin.shape[0]
    half = CONV_WIDTH // 2
    t_idx = lax.broadcasted_iota(jnp.int32, (R, 1), 0)
    acc = b_ref[:, cols] + w_ref[half:half + 1, cols] * xin
    for j in range(CONV_WIDTH):
        off = j - half
        if off == 0:
            continue
        rolled = pltpu.roll(xin, (-off) % R, axis=0)
        if off < 0 and pad_lo:
            rolled = jnp.where(t_idx >= -off, rolled, 0.0)
        if off > 0 and pad_hi:
            rolled = jnp.where(t_idx < R - off, rolled, 0.0)
        acc = acc + w_ref[j:j + 1, cols] * rolled
    return acc * _sigmoid(acc)


def _ssd_kernel(xbc_ref, z_ref, dt_ref, cw_ref, cb_ref, dtb_ref, alog_ref, dsk_ref, nw_ref, tri_ref,
                o_ref, xc_ref, ylo_ref, yhi_ref, st_ref, bt_ref, dtt_ref, cumrow_ref, cumcol_ref, gm_ref):
    y_refs = (ylo_ref, yhi_ref)
    S = xbc_ref.shape[0]
    n_chunks = S // CHUNK
    pair_tiles = SSD_WIDTH // LANES
    tiles_per_group = pair_tiles // SSD_GROUPS
    n_heads_all = N_DIRS * SSD_HEADS
    edge = 2 * 8
    b_off = SSD_WIDTH
    c_off = SSD_WIDTH + SSD_GROUPS * D_STATE
    lane = lax.broadcasted_iota(jnp.int32, (1, LANES), 1)
    low = lane < HALF_LANES

    for c in range(CONV_CH // LANES):
        cols = slice(c * LANES, (c + 1) * LANES)
        xc_ref[:, cols] = _conv_tile(xbc_ref[:, cols], cw_ref, cb_ref, cols, False, False)
        xc_ref[0:edge // 2, cols] = _conv_tile(xbc_ref[0:edge, cols], cw_ref, cb_ref, cols, True, False)[0:edge // 2]
        xc_ref[S - edge // 2:S, cols] = _conv_tile(xbc_ref[S - edge:S, cols], cw_ref, cb_ref, cols,
                                                   False, True)[edge // 2:edge]

    for y_half in y_refs:
        y_half[...] = jnp.zeros_like(y_half)
    st_ref[...] = jnp.zeros_like(st_ref)

    a_col = -jnp.exp(alog_ref[...])
    dtb_col = dtb_ref[...]
    r_i = lax.broadcasted_iota(jnp.int32, (CHUNK, CHUNK), 0)
    c_i = lax.broadcasted_iota(jnp.int32, (CHUNK, CHUNK), 1)
    masks = (c_i <= r_i, c_i >= r_i)
    zero_rows = jnp.zeros((CHUNK - n_heads_all, CHUNK), BF16)
    fwd_rows = lax.broadcasted_iota(jnp.int32, (n_heads_all, 1), 0) < SSD_HEADS
    fwd_lanes = lane < SSD_HEADS

    def prep_chunk(c):
        rows = pl.ds(pl.multiple_of(c * CHUNK, CHUNK), CHUNK)
        dt_all = _softplus(dt_ref[rows, :].T[0:n_heads_all, :] + dtb_col)
        dtt_ref[c] = dt_all
        pieces = _split3(dt_all * a_col)
        padded = [jnp.concatenate([pc, zero_rows], axis=0) for pc in pieces]
        cumrow_ref[c] = jnp.where(fwd_rows, _dot3(pieces, tri_ref[1], "piece_lhs"),
                                  _dot3(pieces, tri_ref[0], "piece_lhs"))
        cumcol_ref[c] = jnp.where(fwd_lanes, _dot3(padded, tri_ref[0], "piece_rhs"),
                                  _dot3(padded, tri_ref[1], "piece_rhs"))
        for g in range(SSD_GROUPS):
            bt = xc_ref[rows, b_off + g * D_STATE:b_off + (g + 1) * D_STATE].T
            bt_ref[c, g * D_STATE:(g + 1) * D_STATE, :] = bt
            c_b = xc_ref[rows, c_off + g * D_STATE:c_off + (g + 1) * D_STATE].astype(BF16)
            gmat = jnp.dot(c_b, bt.astype(BF16), preferred_element_type=F32)
            for d in range(N_DIRS):
                k = d * SSD_GROUPS + g
                gm_ref[c, k * CHUNK:(k + 1) * CHUNK, :] = jnp.where(masks[d], gmat, 0.0).astype(BF16)

    prep_unroll = 4

    def prep(i, carry):
        for u in range(prep_unroll):
            prep_chunk(i * prep_unroll + u)
        return carry

    lax.fori_loop(0, n_chunks // prep_unroll, prep, 0)

    def one_chunk(c, d, y_half, c_local):
        rows = pl.ds(pl.multiple_of(c * CHUNK, CHUNK), CHUNK)
        rows_local = pl.ds(pl.multiple_of(c_local * CHUNK, CHUNK), CHUNK)
        hs = slice(d * SSD_HEADS, (d + 1) * SSD_HEADS)
        dt_d = dtt_ref[c][hs]
        cum_row = cumrow_ref[c][hs]
        cum_col = cumcol_ref[c]
        tot = jnp.sum(dt_d * a_col[hs], axis=1, keepdims=True)
        w_row = dt_d * jnp.exp(tot - cum_row)
        for g in range(SSD_GROUPS):
            c_b = xc_ref[rows, c_off + g * D_STATE:c_off + (g + 1) * D_STATE].astype(BF16)
            bt_g = bt_ref[c, g * D_STATE:(g + 1) * D_STATE, :]
            k = d * SSD_GROUPS + g
            gm_b = gm_ref[c, k * CHUNK:(k + 1) * CHUNK, :]
            for tt in range(tiles_per_group):
                t = g * tiles_per_group + tt
                cols = slice(t * LANES, (t + 1) * LANES)
                x_t = xc_ref[rows, cols].astype(BF16)
                st = st_ref[d, :, cols]
                rhs = jnp.concatenate([x_t, st.astype(BF16)], axis=0)
                ys, sn = [], []
                for hh in (2 * t, 2 * t + 1):
                    col = d * SSD_HEADS + hh
                    colb = jnp.broadcast_to(cum_col[:, col:col + 1], (CHUNK, CHUNK))
                    seg = jnp.minimum(colb - cum_row[hh:hh + 1, :], 0.0)
                    w = (jnp.exp(seg) * dt_d[hh:hh + 1, :]).astype(BF16) * gm_b
                    a = jnp.exp(colb).astype(BF16) * c_b
                    lhs = jnp.concatenate([w, a], axis=1)
                    ys.append(jnp.dot(lhs, rhs, preferred_element_type=F32))
                    bts = (bt_g * w_row[hh:hh + 1, :]).astype(BF16)
                    sn.append(jnp.dot(bts, x_t, preferred_element_type=F32))
                y_half[rows_local, cols] = y_half[rows_local, cols] + jnp.where(low, ys[0], ys[1])
                decay = jnp.exp(jnp.where(low, tot[2 * t:2 * t + 1, :], tot[2 * t + 1:2 * t + 2, :]))
                st_ref[d, :, cols] = st * decay + jnp.where(low, sn[0], sn[1])

    half_chunks = n_chunks // 2

    def body_first(c, carry):
        one_chunk(c, 0, y_refs[0], c)
        one_chunk(n_chunks - 1 - c, 1, y_refs[1], half_chunks - 1 - c)
        return carry

    def body_second(c, carry):
        one_chunk(c, 0, y_refs[1], c - half_chunks)
        one_chunk(n_chunks - 1 - c, 1, y_refs[0], n_chunks - 1 - c)
        return carry

    lax.fori_loop(0, half_chunks, body_first, 0)
    lax.fori_loop(half_chunks, n_chunks, body_second, 0)

    group_w = SSD_WIDTH // SSD_GROUPS
    for half, y_half in enumerate(y_refs):
        def fin(c, carry, half=half, y_half=y_half):
            rows = pl.ds(pl.multiple_of((half * half_chunks + c) * CHUNK, CHUNK), CHUNK)
            rows_local = pl.ds(pl.multiple_of(c * CHUNK, CHUNK), CHUNK)
            zz = z_ref[rows, :]
            y = (y_half[rows_local, :] + dsk_ref[...] * xc_ref[rows, 0:SSD_WIDTH]) * (zz * _sigmoid(zz))
            outs = []
            for g in range(SSD_GROUPS):
                yg = y[:, g * group_w:(g + 1) * group_w]
                ms = jnp.mean(yg * yg, axis=-1, keepdims=True)
                outs.append(yg * lax.rsqrt(ms + RMS_EPS))
            o_ref[rows, :] = (jnp.concatenate(outs, axis=-1) * nw_ref[...]).astype(BF16)
            return carry

        lax.fori_loop(0, half_chunks, fin, 0)


def _ssd_call(xbc, z, dt, cw, cb, dtb, alog, dsk, nw, tri, batch):
    T = xbc.shape[0]
    n_chunks = SEQ // CHUNK
    seq = lambda b: (b, 0)
    return pl.pallas_call(
        _ssd_kernel, grid=(batch,),
        in_specs=[pl.BlockSpec((SEQ, CONV_CH), seq), pl.BlockSpec((SEQ, SSD_WIDTH), seq),
                  pl.BlockSpec((SEQ, DT_PAD), seq),
                  _const_spec(cw.shape), _const_spec(cb.shape), _const_spec(dtb.shape),
                  _const_spec(alog.shape), _const_spec(dsk.shape), _const_spec(nw.shape),
                  _const_spec(tri.shape)],
        out_specs=pl.BlockSpec((SEQ, SSD_WIDTH), seq),
        out_shape=jax.ShapeDtypeStruct((T, SSD_WIDTH), BF16),
        scratch_shapes=[pltpu.VMEM((SEQ, CONV_CH), F32),
                        pltpu.VMEM((SEQ // 2, SSD_WIDTH), F32),
                        pltpu.VMEM((SEQ // 2, SSD_WIDTH), F32),
                        pltpu.VMEM((N_DIRS, D_STATE, SSD_WIDTH), F32),
                        pltpu.VMEM((n_chunks, SSD_GROUPS * D_STATE, CHUNK), F32),
                        pltpu.VMEM((n_chunks, N_DIRS * SSD_HEADS, CHUNK), F32),
                        pltpu.VMEM((n_chunks, N_DIRS * SSD_HEADS, CHUNK), F32),
                        pltpu.VMEM((n_chunks, CHUNK, LANES), F32),
                        pltpu.VMEM((n_chunks, N_DIRS * SSD_GROUPS * CHUNK, CHUNK), BF16)],
        name="ssd_bidir",
        compiler_params=pltpu.CompilerParams(dimension_semantics=("arbitrary",),
                                             vmem_limit_bytes=VMEM_LIMIT),
    )(xbc, z, dt, cw, cb, dtb, alog, dsk, nw, tri)


def _back_kernel(x1_ref, att_ref, ssd_ref, p_ref, wout_ref, wgu_ref, wo_ref,
                 lng_ref, lnb_ref, wp_ref, wgate_ref, bgate_ref, o_ref):
    for r0 in range(0, x1_ref.shape[0], TOKEN_SUB):
        rows = slice(r0, r0 + TOKEN_SUB)
        mix = (jnp.dot(att_ref[rows, :], wout_ref[0:ATT_WIDTH, :], preferred_element_type=F32)
               + jnp.dot(ssd_ref[rows, :], wout_ref[ATT_WIDTH:, :], preferred_element_type=F32))
        x2 = _layer_norm(ALPHA * x1_ref[rows, :] + mix, lng_ref[0:1, :], lnb_ref[0:1, :])
        f = _swiglu(x2.astype(BF16), wgu_ref, wo_ref)
        x3 = _layer_norm(ALPHA * x2 + 0.5 * f, lng_ref[1:2, :], lnb_ref[1:2, :])
        e = jnp.dot(p_ref[rows, :].astype(BF16), wp_ref[...], preferred_element_type=F32)
        gate = _sigmoid(jnp.dot(x3.astype(BF16), wgate_ref[...], preferred_element_type=F32) + bgate_ref[...])
        o_ref[rows, :] = _layer_norm(ALPHA * x3 + gate * e, lng_ref[2:3, :], lnb_ref[2:3, :])


def _back_call(x1, att, ssd, p2d, wout, wgu, wo, lng, lnb, wp, wgate, bgate):
    T = x1.shape[0]
    tm = TOKEN_TILE
    row = lambda i: (i, 0)
    consts = (wout, wgu, wo, lng, lnb, wp, wgate, bgate)
    return pl.pallas_call(
        _back_kernel, grid=(T // tm,),
        in_specs=[pl.BlockSpec((tm, D_MODEL), row), pl.BlockSpec((tm, ATT_WIDTH), row),
                  pl.BlockSpec((tm, SSD_WIDTH), row), pl.BlockSpec((tm, PLE_DIM), row)]
                 + [_const_spec(c.shape) for c in consts],
        out_specs=pl.BlockSpec((tm, D_MODEL), row),
        out_shape=jax.ShapeDtypeStruct((T, D_MODEL), F32),
        name="back_outproj_ffn_ple",
        compiler_params=pltpu.CompilerParams(dimension_semantics=("arbitrary",),
                                             vmem_limit_bytes=VMEM_LIMIT),
    )(x1, att, ssd, p2d, *consts)


def _in_proj_weight(w):
    d = w.shape[0]
    half_blk = HEAD_DIM // 2
    n_q = N_Q_HEADS * HEAD_DIM
    n_kv = N_KV_HEADS * HEAD_DIM
    wq = w[:, :n_q].reshape(d, N_Q_HEADS // 2, 2, half_blk, 2).transpose(0, 1, 4, 2, 3).reshape(d, n_q)
    wk = w[:, n_q:n_q + n_kv].reshape(d, N_KV_HEADS, half_blk, 2).transpose(0, 1, 3, 2)
    wk = jnp.broadcast_to(wk[:, :, :, None, :], (d, N_KV_HEADS, 2, 2, half_blk)).reshape(d, N_KV_HEADS * LANES)
    wv = w[:, n_q + n_kv:n_q + 2 * n_kv].reshape(d, N_KV_HEADS, 1, HEAD_DIM)
    wv = jnp.broadcast_to(wv, (d, N_KV_HEADS, 2, HEAD_DIM)).reshape(d, N_KV_HEADS * LANES)
    rest = w[:, n_q + 2 * n_kv:]
    pad = jnp.zeros((d, DT_PAD - N_DIRS * SSD_HEADS), w.dtype)
    return jnp.concatenate([wq, wk, wv, rest, pad], axis=1).astype(BF16)


def _rope_tiles():
    rows = SEQ // GRID_W
    row = jnp.repeat(jnp.arange(rows, dtype=F32), GRID_W)
    col = jnp.tile(jnp.arange(GRID_W, dtype=F32), rows)
    inv = ROPE_THETA ** (-jnp.arange(0, ROPE_AXIS_DIM, 2, dtype=F32) / ROPE_AXIS_DIM)
    ang = jnp.concatenate([row[:, None] * inv, col[:, None] * inv], axis=-1)
    cos, sin = jnp.cos(ang), jnp.sin(ang)
    cos_t = jnp.concatenate([cos, cos, cos, cos], axis=-1)
    sin_t = jnp.concatenate([-sin, -sin, sin, sin], axis=-1)
    return cos_t, sin_t


def _norm_tile(w):
    lanes = np.arange(LANES)
    idx = 2 * (lanes % (HEAD_DIM // 2)) + lanes // HALF_LANES
    return w[idx][None, :]


def kernel(x, p, ln_g, ln_b, ffn1_w_in, ffn1_w_out, w_in, q_norm, k_norm, conv_w, conv_b, dt_bias,
           a_log, d_skip, ssd_norm, w_out, ffn2_w_in, ffn2_w_out, ple_w, ple_gate_w, ple_gate_b):
    B, S, D = x.shape
    assert (S, D) == (SEQ, D_MODEL) and ln_g.shape[0] == DEPTH == 1
    T = B * S
    x2d = x.reshape(T, D)
    p2d = p[0].reshape(T, PLE_DIM)

    cos_t, sin_t = _rope_tiles()
    lanes = np.arange(LANES)
    same_head = ((lanes[:, None] // (HEAD_DIM // 2)) % 2) == ((lanes[None, :] // (HEAD_DIM // 2)) % 2)
    two_tiles = np.eye(2, dtype=np.float32)
    mq = jnp.asarray(np.kron(two_tiles, np.where(same_head, 1.0 / HEAD_DIM, 0.0)), BF16)
    mk = jnp.asarray(np.kron(two_tiles, np.full((LANES, LANES), 1.0 / LANES)), BF16)
    ones = np.ones((CHUNK, CHUNK), np.float32)
    tri = jnp.asarray(np.stack([np.tril(ones), np.triu(ones)]), BF16)

    i = 0
    wcat = _in_proj_weight(w_in[i])
    qw = _norm_tile(q_norm[i]) * (HEAD_DIM ** -0.5 * LOG2E)
    kw = _norm_tile(k_norm[i])
    x1, qh, kh, vh, z, xbc, dtr = _front_call(
        x2d, ffn1_w_in[i].astype(BF16), ffn1_w_out[i].astype(BF16), ln_g[i, 0:1], ln_b[i, 0:1],
        wcat, cos_t, sin_t, qw, kw, mq, mk)

    att = _attn_call(qh, kh, vh, B)

    dsk = jnp.repeat(d_skip[i], SSD_HEAD_DIM)[None, :]
    ssd = _ssd_call(xbc, z, dtr, conv_w[i], conv_b[i][None, :], dt_bias[i].reshape(-1, 1), a_log[i].reshape(-1, 1),
                    dsk, ssd_norm[i][None, :], tri, B)

    out = _back_call(x1, att, ssd, p2d, w_out[i].astype(BF16),
                     ffn2_w_in[i].astype(BF16), ffn2_w_out[i].astype(BF16), ln_g[i, 1:4], ln_b[i, 1:4],
                     ple_w[i].astype(BF16), ple_gate_w[i].astype(BF16), ple_gate_b[i][None, :])
    return out.reshape(B, S, D)
```

```python
import functools
import math

import numpy as np
import jax
import jax.numpy as jnp
from jax import lax
from jax.experimental import pallas as pl
from jax.experimental.pallas import tpu as pltpu

F32 = jnp.float32
BF16 = jnp.bfloat16

D_MODEL = 1024
SEQ = 2048
DEPTH = 1
HEAD_DIM = 64
N_Q_HEADS = 8
N_KV_HEADS = 2
ROPE_AXIS_DIM = HEAD_DIM // 2
ROPE_THETA = 10000.0
GRID_W = 64
ATT_WIDTH = N_Q_HEADS * HEAD_DIM
SSD_WIDTH = 512
SSD_HEAD_DIM = 64
SSD_HEADS = 8
SSD_GROUPS = 2
D_STATE = 128
CONV_WIDTH = 5
CONV_CH = SSD_WIDTH + 2 * SSD_GROUPS * D_STATE
CHUNK = 128
N_DIRS = 2
D_FF = 2816
PLE_DIM = 256
ALPHA = (2.0 * DEPTH) ** 0.25
LN_EPS = 1e-5
RMS_EPS = 1e-6
LOG2E = 1.4426950408889634

LANES = 128
HALF_LANES = LANES // 2
TOKEN_TILE = 512
NORM_SUB = 256
Q_TILE = 1024
Q_UNIT = 512
FF_CHUNKS = ((0, 768), (768, 768), (1536, 768), (2304, 512))
DT_PAD = LANES
Q_OFF = 0
K_OFF = Q_OFF + ATT_WIDTH
V_OFF = K_OFF + N_KV_HEADS * LANES
Z_OFF = V_OFF + N_KV_HEADS * LANES
XBC_OFF = Z_OFF + SSD_WIDTH
DT_OFF = XBC_OFF + CONV_CH
IN_COLS = DT_OFF + DT_PAD
VMEM_LIMIT = 56 * 1024 * 1024


def _const_spec(shape):
    nd = len(shape)
    return pl.BlockSpec(shape, lambda *_: (0,) * nd, pipeline_mode=pl.Buffered(1))


def _layer_norm(y, g, b):
    mu = jnp.mean(y, axis=-1, keepdims=True)
    yc = y - mu
    var = jnp.mean(yc * yc, axis=-1, keepdims=True)
    return yc * lax.rsqrt(var + LN_EPS) * g + b


def _sigmoid(x):
    return 1.0 / (1.0 + jnp.exp(-x))


def _swiglu(xb, wgu_ref, wo_ref):
    acc = None
    for c0, cw in FF_CHUNKS:
        hg = jnp.dot(xb, wgu_ref[:, c0:c0 + cw], preferred_element_type=F32)
        hu = jnp.dot(xb, wgu_ref[:, D_FF + c0:D_FF + c0 + cw], preferred_element_type=F32)
        act = (hg * _sigmoid(hg) * hu).astype(BF16)
        part = jnp.dot(act, wo_ref[c0:c0 + cw, :], preferred_element_type=F32)
        acc = part if acc is None else acc + part
    return acc


def _split2(x):
    hi = x.astype(BF16)
    lo = (x - hi.astype(F32)).astype(BF16)
    return hi, lo


def _split3(x):
    hi = x.astype(BF16)
    r1 = x - hi.astype(F32)
    mid = r1.astype(BF16)
    lo = (r1 - mid.astype(F32)).astype(BF16)
    return hi, mid, lo


def _norm_rope(t2, seg_mean, w, cos, sin):
    hi, lo = _split2(t2 * t2)
    ms = (jnp.dot(hi, seg_mean, preferred_element_type=F32)
          + jnp.dot(lo, seg_mean, preferred_element_type=F32))
    tn2 = t2 * lax.rsqrt(ms + RMS_EPS)
    outs = []
    for j in range(2):
        tn = tn2[:, j * LANES:(j + 1) * LANES] * w
        outs.append(tn * cos + pltpu.roll(tn, HALF_LANES, axis=1) * sin)
    return outs


def _front_kernel(x_ref, wgu_ref, wo_ref, g_ref, b_ref, win_ref,
                  cos_ref, sin_ref, qw_ref, kw_ref, mq_ref, mk_ref,
                  x1_ref, q_ref, k_ref, v_ref, z_ref, xbc_ref, dt_ref, pre_ref):
    i = pl.program_id(0)
    last = pl.num_programs(0) - 1

    def ffn_stage():
        x = x_ref[...]
        pre_ref[...] = ALPHA * x + 0.5 * _swiglu(x.astype(BF16), wgu_ref, wo_ref)

    def projection_stage():
        mq = mq_ref[...]
        mk = mk_ref[...]
        pair = 2 * LANES
        for r0 in range(0, pre_ref.shape[0], NORM_SUB):
            rows = slice(r0, r0 + NORM_SUB)
            x1 = _layer_norm(pre_ref[rows, :], g_ref[...], b_ref[...])
            x1_ref[rows, :] = x1
            u = jnp.dot(x1.astype(BF16), win_ref[...], preferred_element_type=F32)
            cos = cos_ref[rows, :]
            sin = sin_ref[rows, :]
            for jp in range(ATT_WIDTH // pair):
                tiles = _norm_rope(u[:, Q_OFF + jp * pair:Q_OFF + (jp + 1) * pair], mq, qw_ref[...], cos, sin)
                for j, tile in enumerate(tiles):
                    c0 = jp * pair + j * LANES
                    q_ref[rows, c0:c0 + LANES] = tile.astype(BF16)
            tiles = _norm_rope(u[:, K_OFF:K_OFF + pair], mk, kw_ref[...], cos, sin)
            for j, tile in enumerate(tiles):
                k_ref[rows, j * LANES:(j + 1) * LANES] = tile.astype(BF16)
            v_ref[rows, :] = u[:, V_OFF:Z_OFF].astype(BF16)
            z_ref[rows, :] = u[:, Z_OFF:XBC_OFF]
            xbc_ref[rows, :] = u[:, XBC_OFF:DT_OFF]
            dt_ref[rows, :] = u[:, DT_OFF:IN_COLS]

    @pl.when(i == 0)
    def _():
        pre_ref[...] = jnp.zeros_like(pre_ref)

    @pl.when(i < last)
    def _():
        projection_stage()
        ffn_stage()

    @pl.when(i == last)
    def _():
        projection_stage()


def _front_call(x2d, wgu, wo, g, b, wcat, cos_t, sin_t, qw, kw, mq, mk):
    T = x2d.shape[0]
    tm = TOKEN_TILE
    n_tiles = T // tm
    pos_blocks = SEQ // tm
    cur = lambda i: (jnp.minimum(i, n_tiles - 1), 0)
    late = lambda i: (jnp.maximum(i - 1, 0), 0)
    pos = lambda i: (jnp.maximum(i - 1, 0) % pos_blocks, 0)
    out_shape = (
        jax.ShapeDtypeStruct((T, D_MODEL), F32),
        jax.ShapeDtypeStruct((T, ATT_WIDTH), BF16),
        jax.ShapeDtypeStruct((T, N_KV_HEADS * LANES), BF16),
        jax.ShapeDtypeStruct((T, N_KV_HEADS * LANES), BF16),
        jax.ShapeDtypeStruct((T, SSD_WIDTH), F32),
        jax.ShapeDtypeStruct((T, CONV_CH), F32),
        jax.ShapeDtypeStruct((T, DT_PAD), F32),
    )
    in_specs = [
        pl.BlockSpec((tm, D_MODEL), cur),
        _const_spec(wgu.shape), _const_spec(wo.shape),
        _const_spec(g.shape), _const_spec(b.shape), _const_spec(wcat.shape),
        pl.BlockSpec((tm, LANES), pos), pl.BlockSpec((tm, LANES), pos),
        _const_spec(qw.shape), _const_spec(kw.shape), _const_spec(mq.shape), _const_spec(mk.shape),
    ]
    out_specs = tuple(pl.BlockSpec((tm, s.shape[1]), late) for s in out_shape)
    return pl.pallas_call(
        _front_kernel, grid=(n_tiles + 1,), in_specs=in_specs, out_specs=out_specs, out_shape=out_shape,
        scratch_shapes=[pltpu.VMEM((tm, D_MODEL), F32)],
        name="front_ffn_inproj",
        compiler_params=pltpu.CompilerParams(dimension_semantics=("arbitrary",),
                                             vmem_limit_bytes=VMEM_LIMIT),
    )(x2d, wgu, wo, g, b, wcat, cos_t, sin_t, qw, kw, mq, mk)


def _attn_kernel(q_ref, k_ref, v_ref, o_ref):
    tq = q_ref.shape[0]
    lane = lax.broadcasted_iota(jnp.int32, (1, LANES), 1)
    low = lane < HALF_LANES
    q_per_kv_tiles = (N_Q_HEADS // N_KV_HEADS) * HEAD_DIM // LANES
    for kv in range(N_KV_HEADS):
        kt = k_ref[:, kv * LANES:(kv + 1) * LANES]
        vt = v_ref[:, kv * LANES:(kv + 1) * LANES]
        zero = jnp.zeros_like(vt)
        v_lo = jnp.where(low, vt, zero)
        v_hi = jnp.where(low, zero, vt)
        blk = (lane // (HEAD_DIM // 2)) % 2
        for jj in range(q_per_kv_tiles):
            j = kv * q_per_kv_tiles + jj
            cols = slice(j * LANES, (j + 1) * LANES)
            for r0 in range(0, tq, Q_UNIT):
                rows = slice(r0, r0 + Q_UNIT)
                qt = q_ref[rows, cols]
                zq = jnp.zeros_like(qt)
                out = None
                for a, v_a in ((0, v_lo), (1, v_hi)):
                    qa = jnp.where(blk == a, qt, zq)
                    s = lax.dot_general(qa, kt, (((1,), (1,)), ((), ())), preferred_element_type=F32)
                    m = jnp.max(s, axis=-1, keepdims=True)
                    p = jnp.exp2(s - m)
                    l = jnp.sum(p, axis=-1, keepdims=True)
                    o = jnp.dot(p.astype(BF16), v_a, preferred_element_type=F32) * (1.0 / l)
                    out = o if out is None else out + o
                o_ref[rows, cols] = out.astype(BF16)


def _attn_call(q, k, v, batch):
    T = q.shape[0]
    nq = SEQ // Q_TILE
    return pl.pallas_call(
        _attn_kernel, grid=(batch, nq),
        in_specs=[pl.BlockSpec((Q_TILE, ATT_WIDTH), lambda b, i: (b * nq + i, 0)),
                  pl.BlockSpec((SEQ, N_KV_HEADS * LANES), lambda b, i: (b, 0)),
                  pl.BlockSpec((SEQ, N_KV_HEADS * LANES), lambda b, i: (b, 0))],
        out_specs=pl.BlockSpec((Q_TILE, ATT_WIDTH), lambda b, i: (b * nq + i, 0)),
        out_shape=jax.ShapeDtypeStruct((T, ATT_WIDTH), BF16),
        name="attention",
        compiler_params=pltpu.CompilerParams(dimension_semantics=("arbitrary", "arbitrary"),
                                             vmem_limit_bytes=VMEM_LIMIT),
    )(q, k, v)


def _softplus(x):
    return jnp.maximum(x, 0.0) + jnp.log1p(jnp.exp(-jnp.abs(x)))


def _dot3(a_pieces, b, dims):
    out = None
    for piece in a_pieces:
        lhs, rhs = (piece, b) if dims == "piece_lhs" else (b, piece)
        contract = (((1,), (0,)), ((), ())) if dims == "piece_lhs" else (((1,), (1,)), ((), ()))
        term = lax.dot_general(lhs, rhs, contract, preferred_element_type=F32)
        out = term if out is None else out + term
    return out


def _conv_tile(xin, w_ref, b_ref, cols, pad_lo, pad_hi):
    R = xin.shape[0]
    half = CONV_WIDTH // 2
    t_idx = lax.broadcasted_iota(jnp.int32, (R, 1), 0)
    acc = b_ref[:, cols] + w_ref[half:half + 1, cols] * xin
    for j in range(CONV_WIDTH):
        off = j - half
        if off == 0:
            continue
        rolled = pltpu.roll(xin, (-off) % R, axis=0)
        if off < 0 and pad_lo:
            rolled = jnp.where(t_idx >= -off, rolled, 0.0)
        if off > 0 and pad_hi:
            rolled = jnp.where(t_idx < R - off, rolled, 0.0)
        acc = acc + w_ref[j:j + 1, cols] * rolled
    return acc * _sigmoid(acc)


def _ssd_kernel(xbc_ref, z_ref, dt_ref, cw_ref, cb_ref, dtb_ref, alog_ref, dsk_ref, nw_ref, tri_ref,
                o_ref, xc_ref, ylo_ref, yhi_ref, st_ref, bt_ref, dtt_ref, cumrow_ref, cumcol_ref, gm_ref):
    y_refs = (ylo_ref, yhi_ref)
    S = xbc_ref.shape[0]
    n_chunks = S // CHUNK
    pair_tiles = SSD_WIDTH // LANES
    tiles_per_group = pair_tiles // SSD_GROUPS
    n_heads_all = N_DIRS * SSD_HEADS
    edge = 2 * 8
    b_off = SSD_WIDTH
    c_off = SSD_WIDTH + SSD_GROUPS * D_STATE
    lane = lax.broadcasted_iota(jnp.int32, (1, LANES), 1)
    low = lane < HALF_LANES

    for c in range(CONV_CH // LANES):
        cols = slice(c * LANES, (c + 1) * LANES)
        xc_ref[:, cols] = _conv_tile(xbc_ref[:, cols], cw_ref, cb_ref, cols, False, False)
        xc_ref[0:edge // 2, cols] = _conv_tile(xbc_ref[0:edge, cols], cw_ref, cb_ref, cols, True, False)[0:edge // 2]
        xc_ref[S - edge // 2:S, cols] = _conv_tile(xbc_ref[S - edge:S, cols], cw_ref, cb_ref, cols,
                                                   False, True)[edge // 2:edge]

    for y_half in y_refs:
        y_half[...] = jnp.zeros_like(y_half)
    st_ref[...] = jnp.zeros_like(st_ref)

    a_col = -jnp.exp(alog_ref[...])
    dtb_col = dtb_ref[...]
    r_i = lax.broadcasted_iota(jnp.int32, (CHUNK, CHUNK), 0)
    c_i = lax.broadcasted_iota(jnp.int32, (CHUNK, CHUNK), 1)
    masks = (c_i <= r_i, c_i >= r_i)
    zero_rows = jnp.zeros((CHUNK - n_heads_all, CHUNK), BF16)
    fwd_rows = lax.broadcasted_iota(jnp.int32, (n_heads_all, 1), 0) < SSD_HEADS
    fwd_lanes = lane < SSD_HEADS

    def prep_chunk(c):
        rows = pl.ds(pl.multiple_of(c * CHUNK, CHUNK), CHUNK)
        dt_all = _softplus(dt_ref[rows, :].T[0:n_heads_all, :] + dtb_col)
        dtt_ref[c] = dt_all
        pieces = _split3(dt_all * a_col)
        padded = [jnp.concatenate([pc, zero_rows], axis=0) for pc in pieces]
        cumrow_ref[c] = jnp.where(fwd_rows, _dot3(pieces, tri_ref[1], "piece_lhs"),
                                  _dot3(pieces, tri_ref[0], "piece_lhs"))
        cumcol_ref[c] = jnp.where(fwd_lanes, _dot3(padded, tri_ref[0], "piece_rhs"),
                                  _dot3(padded, tri_ref[1], "piece_rhs"))
        for g in range(SSD_GROUPS):
            bt = xc_ref[rows, b_off + g * D_STATE:b_off + (g + 1) * D_STATE].T
            bt_ref[c, g * D_STATE:(g + 1) * D_STATE, :] = bt
            c_b = xc_ref[rows, c_off + g * D_STATE:c_off + (g + 1) * D_STATE].astype(BF16)
            gmat = jnp.dot(c_b, bt.astype(BF16), preferred_element_type=F32)
            for d in range(N_DIRS):
                k = d * SSD_GROUPS + g
                gm_ref[c, k * CHUNK:(k + 1) * CHUNK, :] = jnp.where(masks[d], gmat, 0.0).astype(BF16)

    prep_unroll = 4

    def prep(i, carry):
        for u in range(prep_unroll):
            prep_chunk(i * prep_unroll + u)
        return carry

    lax.fori_loop(0, n_chunks // prep_unroll, prep, 0)

    def one_chunk(c, d, y_half, c_local):
        rows = pl.ds(pl.multiple_of(c * CHUNK, CHUNK), CHUNK)
        rows_local = pl.ds(pl.multiple_of(c_local * CHUNK, CHUNK), CHUNK)
        hs = slice(d * SSD_HEADS, (d + 1) * SSD_HEADS)
        dt_d = dtt_ref[c][hs]
        cum_row = cumrow_ref[c][hs]
        cum_col = cumcol_ref[c]
        tot = jnp.sum(dt_d * a_col[hs], axis=1, keepdims=True)
        w_row = dt_d * jnp.exp(tot - cum_row)
        for g in range(SSD_GROUPS):
            c_b = xc_ref[rows, c_off + g * D_STATE:c_off + (g + 1) * D_STATE].astype(BF16)
            bt_g = bt_ref[c, g * D_STATE:(g + 1) * D_STATE, :]
            k = d * SSD_GROUPS + g
            gm_b = gm_ref[c, k * CHUNK:(k + 1) * CHUNK, :]
            for tt in range(tiles_per_group):
                t = g * tiles_per_group + tt
                cols = slice(t * LANES, (t + 1) * LANES)
                x_t = xc_ref[rows, cols].astype(BF16)
                st = st_ref[d, :, cols]
                rhs = jnp.concatenate([x_t, st.astype(BF16)], axis=0)
                ys, sn = [], []
                for hh in (2 * t, 2 * t + 1):
                    col = d * SSD_HEADS + hh
                    colb = jnp.broadcast_to(cum_col[:, col:col + 1], (CHUNK, CHUNK))
                    seg = jnp.minimum(colb - cum_row[hh:hh + 1, :], 0.0)
                    w = (jnp.exp(seg) * dt_d[hh:hh + 1, :]).astype(BF16) * gm_b
                    a = jnp.exp(colb).astype(BF16) * c_b
                    lhs = jnp.concatenate([w, a], axis=1)
                    ys.append(jnp.dot(lhs, rhs, preferred_element_type=F32))
                    bts = (bt_g * w_row[hh:hh + 1, :]).astype(BF16)
                    sn.append(jnp.dot(bts, x_t, preferred_element_type=F32))
                y_half[rows_local, cols] = y_half[rows_local, cols] + jnp.where(low, ys[0], ys[1])
                decay = jnp.exp(jnp.where(low, tot[2 * t:2 * t + 1, :], tot[2 * t + 1:2 * t + 2, :]))
                st_ref[d, :, cols] = st * decay + jnp.where(low, sn[0], sn[1])

    half_chunks = n_chunks // 2

    def body_first(c, carry):
        one_chunk(c, 0, y_refs[0], c)
        one_chunk(n_chunks - 1 - c, 1, y_refs[1], half_chunks - 1 - c)
        return carry

    def body_second(c, carry):
        one_chunk(c, 0, y_refs[1], c - half_chunks)
        one_chunk(n_chunks - 1 - c, 1, y_refs[0], n_chunks - 1 - c)
        return carry

    lax.fori_loop(0, half_chunks, body_first, 0)
    lax.fori_loop(half_chunks, n_chunks, body_second, 0)

    group_w = SSD_WIDTH // SSD_GROUPS
    for half, y_half in enumerate(y_refs):
        def fin(c, carry, half=half, y_half=y_half):
            rows = pl.ds(pl.multiple_of((half * half_chunks + c) * CHUNK, CHUNK), CHUNK)
            rows_local = pl.ds(pl.multiple_of(c * CHUNK, CHUNK), CHUNK)
            zz = z_ref[rows, :]
            y = (y_half[rows_local, :] + dsk_ref[...] * xc_ref[rows, 0:SSD_WIDTH]) * (zz * _sigmoid(zz))
            outs = []
            for g in range(SSD_GROUPS):
                yg = y[:, g * group_w:(g + 1) * group_w]
                ms = jnp.mean(yg * yg, axis=-1, keepdims=True)
                outs.append(yg * lax.rsqrt(ms + RMS_EPS))
            o_ref[rows, :] = (jnp.concatenate(outs, axis=-1) * nw_ref[...]).astype(BF16)
            return carry

        lax.fori_loop(0, half_chunks, fin, 0)


def _ssd_call(xbc, z, dt, cw, cb, dtb, alog, dsk, nw, tri, batch):
    T = xbc.shape[0]
    n_chunks = SEQ // CHUNK
    seq = lambda b: (b, 0)
    return pl.pallas_call(
        _ssd_kernel, grid=(batch,),
        in_specs=[pl.BlockSpec((SEQ, CONV_CH), seq), pl.BlockSpec((SEQ, SSD_WIDTH), seq),
                  pl.BlockSpec((SEQ, DT_PAD), seq),
                  _const_spec(cw.shape), _const_spec(cb.shape), _const_spec(dtb.shape),
                  _const_spec(alog.shape), _const_spec(dsk.shape), _const_spec(nw.shape),
                  _const_spec(tri.shape)],
        out_specs=pl.BlockSpec((SEQ, SSD_WIDTH), seq),
        out_shape=jax.ShapeDtypeStruct((T, SSD_WIDTH), BF16),
        scratch_shapes=[pltpu.VMEM((SEQ, CONV_CH), F32),
                        pltpu.VMEM((SEQ // 2, SSD_WIDTH), F32),
                        pltpu.VMEM((SEQ // 2, SSD_WIDTH), F32),
                        pltpu.VMEM((N_DIRS, D_STATE, SSD_WIDTH), F32),
                        pltpu.VMEM((n_chunks, SSD_GROUPS * D_STATE, CHUNK), F32),
                        pltpu.VMEM((n_chunks, N_DIRS * SSD_HEADS, CHUNK), F32),
                        pltpu.VMEM((n_chunks, N_DIRS * SSD_HEADS, CHUNK), F32),
                        pltpu.VMEM((n_chunks, CHUNK, LANES), F32),
                        pltpu.VMEM((n_chunks, N_DIRS * SSD_GROUPS * CHUNK, CHUNK), BF16)],
        name="ssd_bidir",
        compiler_params=pltpu.CompilerParams(dimension_semantics=("arbitrary",),
                                             vmem_limit_bytes=VMEM_LIMIT),
    )(xbc, z, dt, cw, cb, dtb, alog, dsk, nw, tri)


def _back_kernel(x1_ref, att_ref, ssd_ref, p_ref, wout_ref, wgu_ref, wo_ref,
                 lng_ref, lnb_ref, wp_ref, wgate_ref, bgate_ref, o_ref, pre_ref):
    i = pl.program_id(0)
    last = pl.num_programs(0) - 1
    subs = [slice(r0, r0 + NORM_SUB) for r0 in range(0, x1_ref.shape[0], NORM_SUB)]

    def ffn_stage():
        x2_parts = []
        for rows in subs:
            mix = (jnp.dot(att_ref[rows, :], wout_ref[0:ATT_WIDTH, :], preferred_element_type=F32)
                   + jnp.dot(ssd_ref[rows, :], wout_ref[ATT_WIDTH:, :], preferred_element_type=F32))
            x2_parts.append(_layer_norm(ALPHA * x1_ref[rows, :] + mix, lng_ref[0:1, :], lnb_ref[0:1, :]))
        x2 = jnp.concatenate(x2_parts, axis=0)
        pre_ref[...] = ALPHA * x2 + 0.5 * _swiglu(x2.astype(BF16), wgu_ref, wo_ref)

    def embed_stage():
        for rows in subs:
            x3 = _layer_norm(pre_ref[rows, :], lng_ref[1:2, :], lnb_ref[1:2, :])
            e = jnp.dot(p_ref[rows, :].astype(BF16), wp_ref[...], preferred_element_type=F32)
            gate = _sigmoid(jnp.dot(x3.astype(BF16), wgate_ref[...], preferred_element_type=F32) + bgate_ref[...])
            o_ref[rows, :] = _layer_norm(ALPHA * x3 + gate * e, lng_ref[2:3, :], lnb_ref[2:3, :])

    @pl.when(i == 0)
    def _():
        pre_ref[...] = jnp.zeros_like(pre_ref)

    @pl.when(i < last)
    def _():
        embed_stage()
        ffn_stage()

    @pl.when(i == last)
    def _():
        embed_stage()


def _back_call(x1, att, ssd, p2d, wout, wgu, wo, lng, lnb, wp, wgate, bgate):
    T = x1.shape[0]
    tm = TOKEN_TILE
    n_tiles = T // tm
    cur = lambda i: (jnp.minimum(i, n_tiles - 1), 0)
    late = lambda i: (jnp.maximum(i - 1, 0), 0)
    consts = (wout, wgu, wo, lng, lnb, wp, wgate, bgate)
    return pl.pallas_call(
        _back_kernel, grid=(n_tiles + 1,),
        in_specs=[pl.BlockSpec((tm, D_MODEL), cur), pl.BlockSpec((tm, ATT_WIDTH), cur),
                  pl.BlockSpec((tm, SSD_WIDTH), cur), pl.BlockSpec((tm, PLE_DIM), late)]
                 + [_const_spec(c.shape) for c in consts],
        out_specs=pl.BlockSpec((tm, D_MODEL), late),
        out_shape=jax.ShapeDtypeStruct((T, D_MODEL), F32),
        scratch_shapes=[pltpu.VMEM((tm, D_MODEL), F32)],
        name="back_outproj_ffn_ple",
        compiler_params=pltpu.CompilerParams(dimension_semantics=("arbitrary",),
                                             vmem_limit_bytes=VMEM_LIMIT),
    )(x1, att, ssd, p2d, *consts)


def _in_proj_weight(w):
    d = w.shape[0]
    half_blk = HEAD_DIM // 2
    n_q = N_Q_HEADS * HEAD_DIM
    n_kv = N_KV_HEADS * HEAD_DIM
    wq = w[:, :n_q].reshape(d, N_Q_HEADS // 2, 2, half_blk, 2).transpose(0, 1, 4, 2, 3).reshape(d, n_q)
    wk = w[:, n_q:n_q + n_kv].reshape(d, N_KV_HEADS, half_blk, 2).transpose(0, 1, 3, 2)
    wk = jnp.broadcast_to(wk[:, :, :, None, :], (d, N_KV_HEADS, 2, 2, half_blk)).reshape(d, N_KV_HEADS * LANES)
    wv = w[:, n_q + n_kv:n_q + 2 * n_kv].reshape(d, N_KV_HEADS, 1, HEAD_DIM)
    wv = jnp.broadcast_to(wv, (d, N_KV_HEADS, 2, HEAD_DIM)).reshape(d, N_KV_HEADS * LANES)
    rest = w[:, n_q + 2 * n_kv:]
    pad = jnp.zeros((d, DT_PAD - N_DIRS * SSD_HEADS), w.dtype)
    return jnp.concatenate([wq, wk, wv, rest, pad], axis=1).astype(BF16)


def _rope_tiles():
    rows = SEQ // GRID_W
    row = jnp.repeat(jnp.arange(rows, dtype=F32), GRID_W)
    col = jnp.tile(jnp.arange(GRID_W, dtype=F32), rows)
    inv = ROPE_THETA ** (-jnp.arange(0, ROPE_AXIS_DIM, 2, dtype=F32) / ROPE_AXIS_DIM)
    ang = jnp.concatenate([row[:, None] * inv, col[:, None] * inv], axis=-1)
    cos, sin = jnp.cos(ang), jnp.sin(ang)
    cos_t = jnp.concatenate([cos, cos, cos, cos], axis=-1)
    sin_t = jnp.concatenate([-sin, -sin, sin, sin], axis=-1)
    return cos_t, sin_t


def _norm_tile(w):
    lanes = np.arange(LANES)
    idx = 2 * (lanes % (HEAD_DIM // 2)) + lanes // HALF_LANES
    return w[idx][None, :]


def kernel(x, p, ln_g, ln_b, ffn1_w_in, ffn1_w_out, w_in, q_norm, k_norm, conv_w, conv_b, dt_bias,
           a_log, d_skip, ssd_norm, w_out, ffn2_w_in, ffn2_w_out, ple_w, ple_gate_w, ple_gate_b):
    B, S, D = x.shape
    assert (S, D) == (SEQ, D_MODEL) and ln_g.shape[0] == DEPTH == 1
    T = B * S
    x2d = x.reshape(T, D)
    p2d = p[0].reshape(T, PLE_DIM)

    cos_t, sin_t = _rope_tiles()
    lanes = np.arange(LANES)
    same_head = ((lanes[:, None] // (HEAD_DIM // 2)) % 2) == ((lanes[None, :] // (HEAD_DIM // 2)) % 2)
    two_tiles = np.eye(2, dtype=np.float32)
    mq = jnp.asarray(np.kron(two_tiles, np.where(same_head, 1.0 / HEAD_DIM, 0.0)), BF16)
    mk = jnp.asarray(np.kron(two_tiles, np.full((LANES, LANES), 1.0 / LANES)), BF16)
    ones = np.ones((CHUNK, CHUNK), np.float32)
    tri = jnp.asarray(np.stack([np.tril(ones), np.triu(ones)]), BF16)

    i = 0
    wcat = _in_proj_weight(w_in[i])
    qw = _norm_tile(q_norm[i]) * (HEAD_DIM ** -0.5 * LOG2E)
    kw = _norm_tile(k_norm[i])
    x1, qh, kh, vh, z, xbc, dtr = _front_call(
        x2d, ffn1_w_in[i].astype(BF16), ffn1_w_out[i].astype(BF16), ln_g[i, 0:1], ln_b[i, 0:1],
        wcat, cos_t, sin_t, qw, kw, mq, mk)

    att = _attn_call(qh, kh, vh, B)

    dsk = jnp.repeat(d_skip[i], SSD_HEAD_DIM)[None, :]
    ssd = _ssd_call(xbc, z, dtr, conv_w[i], conv_b[i][None, :], dt_bias[i].reshape(-1, 1), a_log[i].reshape(-1, 1),
                    dsk, ssd_norm[i][None, :], tri, B)

    out = _back_call(x1, att, ssd, p2d, w_out[i].astype(BF16),
                     ffn2_w_in[i].astype(BF16), ffn2_w_out[i].astype(BF16), ln_g[i, 1:4], ln_b[i, 1:4],
                     ple_w[i].astype(BF16), ple_gate_w[i].astype(BF16), ple_gate_b[i][None, :])
    return out.reshape(B, S, D)
```

```python
import functools
import math

import numpy as np
import jax
import jax.numpy as jnp
from jax import lax
from jax.experimental import pallas as pl
from jax.experimental.pallas import tpu as pltpu

F32 = jnp.float32
BF16 = jnp.bfloat16

D_MODEL = 1024
SEQ = 2048
DEPTH = 1
HEAD_DIM = 64
N_Q_HEADS = 8
N_KV_HEADS = 2
ROPE_AXIS_DIM = HEAD_DIM // 2
ROPE_THETA = 10000.0
GRID_W = 64
ATT_WIDTH = N_Q_HEADS * HEAD_DIM
SSD_WIDTH = 512
SSD_HEAD_DIM = 64
SSD_HEADS = 8
SSD_GROUPS = 2
D_STATE = 128
CONV_WIDTH = 5
CONV_CH = SSD_WIDTH + 2 * SSD_GROUPS * D_STATE
CHUNK = 128
N_DIRS = 2
D_FF = 2816
PLE_DIM = 256
ALPHA = (2.0 * DEPTH) ** 0.25
LN_EPS = 1e-5
RMS_EPS = 1e-6
LOG2E = 1.4426950408889634

LANES = 128
HALF_LANES = LANES // 2
SUBLANES = 8
HALO_ROWS = SUBLANES // 2
TOKEN_TILE = 512
NORM_SUB = 256
Q_TILE = 1024
Q_UNIT = 512
FF_CHUNKS = ((0, 768), (768, 768), (1536, 768), (2304, 512))
DT_PAD = LANES
Q_OFF = 0
K_OFF = Q_OFF + ATT_WIDTH
V_OFF = K_OFF + N_KV_HEADS * LANES
Z_OFF = V_OFF + N_KV_HEADS * LANES
XBC_OFF = Z_OFF + SSD_WIDTH
DT_OFF = XBC_OFF + CONV_CH
IN_COLS = DT_OFF + DT_PAD
VMEM_LIMIT = 56 * 1024 * 1024


def _const_spec(shape):
    nd = len(shape)
    return pl.BlockSpec(shape, lambda *_: (0,) * nd, pipeline_mode=pl.Buffered(1))


def _layer_norm(y, g, b):
    mu = jnp.mean(y, axis=-1, keepdims=True)
    yc = y - mu
    var = jnp.mean(yc * yc, axis=-1, keepdims=True)
    return yc * lax.rsqrt(var + LN_EPS) * g + b


def _sigmoid(x):
    return 1.0 / (1.0 + jnp.exp(-x))


def _swiglu(xb, wgu_ref, wo_ref):
    acc = None
    for c0, cw in FF_CHUNKS:
        hg = jnp.dot(xb, wgu_ref[:, c0:c0 + cw], preferred_element_type=F32)
        hu = jnp.dot(xb, wgu_ref[:, D_FF + c0:D_FF + c0 + cw], preferred_element_type=F32)
        act = (hg * _sigmoid(hg) * hu).astype(BF16)
        part = jnp.dot(act, wo_ref[c0:c0 + cw, :], preferred_element_type=F32)
        acc = part if acc is None else acc + part
    return acc


def _split2(x):
    hi = x.astype(BF16)
    lo = (x - hi.astype(F32)).astype(BF16)
    return hi, lo


def _split3(x):
    hi = x.astype(BF16)
    r1 = x - hi.astype(F32)
    mid = r1.astype(BF16)
    lo = (r1 - mid.astype(F32)).astype(BF16)
    return hi, mid, lo


def _norm_rope(t2, seg_mean, w, cos, sin):
    hi, lo = _split2(t2 * t2)
    ms = (jnp.dot(hi, seg_mean, preferred_element_type=F32)
          + jnp.dot(lo, seg_mean, preferred_element_type=F32))
    tn2 = t2 * lax.rsqrt(ms + RMS_EPS)
    outs = []
    for j in range(2):
        tn = tn2[:, j * LANES:(j + 1) * LANES] * w
        outs.append(tn * cos + pltpu.roll(tn, HALF_LANES, axis=1) * sin)
    return outs


def _front_kernel(x_ref, wgu_ref, wo_ref, g_ref, b_ref, win_ref,
                  cos_ref, sin_ref, qw_ref, kw_ref, mq_ref, mk_ref, cw_ref, cb_ref,
                  x1_ref, q_ref, k_ref, v_ref, z_ref, xc_ref, halo_ref, dt_ref, pre_ref):
    i = pl.program_id(0)
    last = pl.num_programs(0) - 1

    def ffn_stage():
        x = x_ref[...]
        pre_ref[...] = ALPHA * x + 0.5 * _swiglu(x.astype(BF16), wgu_ref, wo_ref)

    def projection_stage():
        mq = mq_ref[...]
        mk = mk_ref[...]
        pair = 2 * LANES
        for r0 in range(0, pre_ref.shape[0], NORM_SUB):
            rows = slice(r0, r0 + NORM_SUB)
            x1 = _layer_norm(pre_ref[rows, :], g_ref[...], b_ref[...])
            x1_ref[rows, :] = x1
            u = jnp.dot(x1.astype(BF16), win_ref[...], preferred_element_type=F32)
            cos = cos_ref[rows, :]
            sin = sin_ref[rows, :]
            for jp in range(ATT_WIDTH // pair):
                tiles = _norm_rope(u[:, Q_OFF + jp * pair:Q_OFF + (jp + 1) * pair], mq, qw_ref[...], cos, sin)
                for j, tile in enumerate(tiles):
                    c0 = jp * pair + j * LANES
                    q_ref[rows, c0:c0 + LANES] = tile.astype(BF16)
            tiles = _norm_rope(u[:, K_OFF:K_OFF + pair], mk, kw_ref[...], cos, sin)
            for j, tile in enumerate(tiles):
                k_ref[rows, j * LANES:(j + 1) * LANES] = tile.astype(BF16)
            v_ref[rows, :] = u[:, V_OFF:Z_OFF].astype(BF16)
            z_ref[rows, :] = u[:, Z_OFF:XBC_OFF]
            xraw = u[:, XBC_OFF:DT_OFF]
            for c in range(CONV_CH // LANES):
                cols = slice(c * LANES, (c + 1) * LANES)
                xc_ref[rows, cols] = _conv_tile(xraw[:, cols], cw_ref, cb_ref, cols)
            first_rows = lax.broadcasted_iota(jnp.int32, (SUBLANES, 1), 0) < HALO_ROWS
            halo_ref[r0 // NORM_SUB] = jnp.where(first_rows, xraw[0:SUBLANES, :], xraw[NORM_SUB - SUBLANES:, :])
            dt_ref[rows, :] = u[:, DT_OFF:IN_COLS]

    @pl.when(i == 0)
    def _():
        pre_ref[...] = jnp.zeros_like(pre_ref)

    @pl.when(i < last)
    def _():
        projection_stage()
        ffn_stage()

    @pl.when(i == last)
    def _():
        projection_stage()


def _front_call(x2d, wgu, wo, g, b, wcat, cos_t, sin_t, qw, kw, mq, mk, cw, cb):
    T = x2d.shape[0]
    tm = TOKEN_TILE
    n_tiles = T // tm
    pos_blocks = SEQ // tm
    cur = lambda i: (jnp.minimum(i, n_tiles - 1), 0)
    late = lambda i: (jnp.maximum(i - 1, 0), 0)
    pos = lambda i: (jnp.maximum(i - 1, 0) % pos_blocks, 0)
    out_shape = (
        jax.ShapeDtypeStruct((T, D_MODEL), F32),
        jax.ShapeDtypeStruct((T, ATT_WIDTH), BF16),
        jax.ShapeDtypeStruct((T, N_KV_HEADS * LANES), BF16),
        jax.ShapeDtypeStruct((T, N_KV_HEADS * LANES), BF16),
        jax.ShapeDtypeStruct((T, SSD_WIDTH), F32),
        jax.ShapeDtypeStruct((T, CONV_CH), F32),
        jax.ShapeDtypeStruct((T // NORM_SUB, SUBLANES, CONV_CH), F32),
        jax.ShapeDtypeStruct((T, DT_PAD), F32),
    )
    in_specs = [
        pl.BlockSpec((tm, D_MODEL), cur),
        _const_spec(wgu.shape), _const_spec(wo.shape),
        _const_spec(g.shape), _const_spec(b.shape), _const_spec(wcat.shape),
        pl.BlockSpec((tm, LANES), pos), pl.BlockSpec((tm, LANES), pos),
        _const_spec(qw.shape), _const_spec(kw.shape), _const_spec(mq.shape), _const_spec(mk.shape),
        _const_spec(cw.shape), _const_spec(cb.shape),
    ]
    late3 = lambda i: (jnp.maximum(i - 1, 0), 0, 0)
    out_specs = tuple(pl.BlockSpec((tm, s.shape[1]), late) if len(s.shape) == 2
                      else pl.BlockSpec((tm // NORM_SUB,) + s.shape[1:], late3) for s in out_shape)
    return pl.pallas_call(
        _front_kernel, grid=(n_tiles + 1,), in_specs=in_specs, out_specs=out_specs, out_shape=out_shape,
        scratch_shapes=[pltpu.VMEM((tm, D_MODEL), F32)],
        name="front_ffn_inproj",
        compiler_params=pltpu.CompilerParams(dimension_semantics=("arbitrary",),
                                             vmem_limit_bytes=VMEM_LIMIT),
    )(x2d, wgu, wo, g, b, wcat, cos_t, sin_t, qw, kw, mq, mk, cw, cb)


def _attn_kernel(q_ref, k_ref, v_ref, o_ref):
    tq = q_ref.shape[0]
    lane = lax.broadcasted_iota(jnp.int32, (1, LANES), 1)
    low = lane < HALF_LANES
    q_per_kv_tiles = (N_Q_HEADS // N_KV_HEADS) * HEAD_DIM // LANES
    for kv in range(N_KV_HEADS):
        kt = k_ref[:, kv * LANES:(kv + 1) * LANES]
        vt = v_ref[:, kv * LANES:(kv + 1) * LANES]
        zero = jnp.zeros_like(vt)
        v_lo = jnp.where(low, vt, zero)
        v_hi = jnp.where(low, zero, vt)
        blk = (lane // (HEAD_DIM // 2)) % 2
        for jj in range(q_per_kv_tiles):
            j = kv * q_per_kv_tiles + jj
            cols = slice(j * LANES, (j + 1) * LANES)
            for r0 in range(0, tq, Q_UNIT):
                rows = slice(r0, r0 + Q_UNIT)
                qt = q_ref[rows, cols]
                zq = jnp.zeros_like(qt)
                out = None
                for a, v_a in ((0, v_lo), (1, v_hi)):
                    qa = jnp.where(blk == a, qt, zq)
                    s = lax.dot_general(qa, kt, (((1,), (1,)), ((), ())), preferred_element_type=F32)
                    m = jnp.max(s, axis=-1, keepdims=True)
                    p = jnp.exp2(s - m)
                    l = jnp.sum(p, axis=-1, keepdims=True)
                    o = jnp.dot(p.astype(BF16), v_a, preferred_element_type=F32) * (1.0 / l)
                    out = o if out is None else out + o
                o_ref[rows, cols] = out.astype(BF16)


def _attn_call(q, k, v, batch):
    T = q.shape[0]
    nq = SEQ // Q_TILE
    return pl.pallas_call(
        _attn_kernel, grid=(batch, nq),
        in_specs=[pl.BlockSpec((Q_TILE, ATT_WIDTH), lambda b, i: (b * nq + i, 0)),
                  pl.BlockSpec((SEQ, N_KV_HEADS * LANES), lambda b, i: (b, 0)),
                  pl.BlockSpec((SEQ, N_KV_HEADS * LANES), lambda b, i: (b, 0))],
        out_specs=pl.BlockSpec((Q_TILE, ATT_WIDTH), lambda b, i: (b * nq + i, 0)),
        out_shape=jax.ShapeDtypeStruct((T, ATT_WIDTH), BF16),
        name="attention",
        compiler_params=pltpu.CompilerParams(dimension_semantics=("arbitrary", "arbitrary"),
                                             vmem_limit_bytes=VMEM_LIMIT),
    )(q, k, v)


def _softplus(x):
    return jnp.maximum(x, 0.0) + jnp.log1p(jnp.exp(-jnp.abs(x)))


def _dot3(a_pieces, b, dims):
    out = None
    for piece in a_pieces:
        lhs, rhs = (piece, b) if dims == "piece_lhs" else (b, piece)
        contract = (((1,), (0,)), ((), ())) if dims == "piece_lhs" else (((1,), (1,)), ((), ()))
        term = lax.dot_general(lhs, rhs, contract, preferred_element_type=F32)
        out = term if out is None else out + term
    return out


def _conv_tile(xin, w_ref, b_ref, cols):
    R = xin.shape[0]
    half = CONV_WIDTH // 2
    acc = b_ref[:, cols] + w_ref[half:half + 1, cols] * xin
    for j in range(CONV_WIDTH):
        off = j - half
        if off != 0:
            acc = acc + w_ref[j:j + 1, cols] * pltpu.roll(xin, (-off) % R, axis=0)
    return acc * _sigmoid(acc)


def _ssd_kernel(xcin_ref, halo_ref, z_ref, dt_ref, cw_ref, cb_ref, dtb_ref, alog_ref, dsk_ref, nw_ref, tri_ref,
                o_ref, xc_ref, ylo_ref, yhi_ref, st_ref, bt_ref, dtt_ref, cumrow_ref, cumcol_ref, gm_ref):
    y_refs = (ylo_ref, yhi_ref)
    S = xcin_ref.shape[0]
    n_chunks = S // CHUNK
    pair_tiles = SSD_WIDTH // LANES
    tiles_per_group = pair_tiles // SSD_GROUPS
    n_heads_all = N_DIRS * SSD_HEADS
    b_off = SSD_WIDTH
    c_off = SSD_WIDTH + SSD_GROUPS * D_STATE
    lane = lax.broadcasted_iota(jnp.int32, (1, LANES), 1)
    low = lane < HALF_LANES

    xc_ref[...] = xcin_ref[...]
    n_groups = S // NORM_SUB
    row8 = lax.broadcasted_iota(jnp.int32, (SUBLANES, 1), 0)
    no_rows = jnp.zeros((SUBLANES, LANES), F32)
    for k in range(n_groups + 1):
        b = k * NORM_SUB
        for c in range(CONV_CH // LANES):
            cols = slice(c * LANES, (c + 1) * LANES)
            before = halo_ref[k - 1, :, cols] if k > 0 else no_rows
            after = halo_ref[k, :, cols] if k < n_groups else no_rows
            window = jnp.where(row8 < HALO_ROWS, pltpu.roll(before, HALO_ROWS, axis=0),
                               pltpu.roll(after, HALO_ROWS, axis=0))
            fixed = pltpu.roll(_conv_tile(window, cw_ref, cb_ref, cols), HALO_ROWS, axis=0)
            if k > 0:
                blk = slice(b - SUBLANES, b)
                xc_ref[blk, cols] = jnp.where(row8 >= SUBLANES - CONV_WIDTH // 2, fixed, xc_ref[blk, cols])
            if k < n_groups:
                blk = slice(b, b + SUBLANES)
                xc_ref[blk, cols] = jnp.where(row8 < CONV_WIDTH // 2, fixed, xc_ref[blk, cols])

    for y_half in y_refs:
        y_half[...] = jnp.zeros_like(y_half)
    st_ref[...] = jnp.zeros_like(st_ref)

    a_col = -jnp.exp(alog_ref[...])
    dtb_col = dtb_ref[...]
    r_i = lax.broadcasted_iota(jnp.int32, (CHUNK, CHUNK), 0)
    c_i = lax.broadcasted_iota(jnp.int32, (CHUNK, CHUNK), 1)
    masks = (c_i <= r_i, c_i >= r_i)
    zero_rows = jnp.zeros((CHUNK - n_heads_all, CHUNK), BF16)
    fwd_rows = lax.broadcasted_iota(jnp.int32, (n_heads_all, 1), 0) < SSD_HEADS
    fwd_lanes = lane < SSD_HEADS

    def prep_chunk(c):
        rows = pl.ds(pl.multiple_of(c * CHUNK, CHUNK), CHUNK)
        dt_all = _softplus(dt_ref[rows, :].T[0:n_heads_all, :] + dtb_col)
        dtt_ref[c] = dt_all
        pieces = _split3(dt_all * a_col)
        padded = [jnp.concatenate([pc, zero_rows], axis=0) for pc in pieces]
        cumrow_ref[c] = jnp.where(fwd_rows, _dot3(pieces, tri_ref[1], "piece_lhs"),
                                  _dot3(pieces, tri_ref[0], "piece_lhs"))
        cumcol_ref[c] = jnp.where(fwd_lanes, _dot3(padded, tri_ref[0], "piece_rhs"),
                                  _dot3(padded, tri_ref[1], "piece_rhs"))
        for g in range(SSD_GROUPS):
            bt = xc_ref[rows, b_off + g * D_STATE:b_off + (g + 1) * D_STATE].T
            bt_ref[c, g * D_STATE:(g + 1) * D_STATE, :] = bt
            c_b = xc_ref[rows, c_off + g * D_STATE:c_off + (g + 1) * D_STATE].astype(BF16)
            gmat = jnp.dot(c_b, bt.astype(BF16), preferred_element_type=F32)
            for d in range(N_DIRS):
                k = d * SSD_GROUPS + g
                gm_ref[c, k * CHUNK:(k + 1) * CHUNK, :] = jnp.where(masks[d], gmat, 0.0).astype(BF16)

    prep_unroll = 4

    def prep(i, carry):
        for u in range(prep_unroll):
            prep_chunk(i * prep_unroll + u)
        return carry

    lax.fori_loop(0, n_chunks // prep_unroll, prep, 0)

    def one_chunk(c, d, y_half, c_local):
        rows = pl.ds(pl.multiple_of(c * CHUNK, CHUNK), CHUNK)
        rows_local = pl.ds(pl.multiple_of(c_local * CHUNK, CHUNK), CHUNK)
        hs = slice(d * SSD_HEADS, (d + 1) * SSD_HEADS)
        dt_d = dtt_ref[c][hs]
        cum_row = cumrow_ref[c][hs]
        cum_col = cumcol_ref[c]
        tot = jnp.sum(dt_d * a_col[hs], axis=1, keepdims=True)
        w_row = dt_d * jnp.exp(tot - cum_row)
        for g in range(SSD_GROUPS):
            c_b = xc_ref[rows, c_off + g * D_STATE:c_off + (g + 1) * D_STATE].astype(BF16)
            bt_g = bt_ref[c, g * D_STATE:(g + 1) * D_STATE, :]
            k = d * SSD_GROUPS + g
            gm_b = gm_ref[c, k * CHUNK:(k + 1) * CHUNK, :]
            for tt in range(tiles_per_group):
                t = g * tiles_per_group + tt
                cols = slice(t * LANES, (t + 1) * LANES)
                x_t = xc_ref[rows, cols].astype(BF16)
                st = st_ref[d, :, cols]
                rhs = jnp.concatenate([x_t, st.astype(BF16)], axis=0)
                ys, sn = [], []
                for hh in (2 * t, 2 * t + 1):
                    col = d * SSD_HEADS + hh
                    colb = jnp.broadcast_to(cum_col[:, col:col + 1], (CHUNK, CHUNK))
                    seg = jnp.minimum(colb - cum_row[hh:hh + 1, :], 0.0)
                    w = (jnp.exp(seg) * dt_d[hh:hh + 1, :]).astype(BF16) * gm_b
                    a = jnp.exp(colb).astype(BF16) * c_b
                    lhs = jnp.concatenate([w, a], axis=1)
                    ys.append(jnp.dot(lhs, rhs, preferred_element_type=F32))
                    bts = (bt_g * w_row[hh:hh + 1, :]).astype(BF16)
                    sn.append(jnp.dot(bts, x_t, preferred_element_type=F32))
                y_half[rows_local, cols] = y_half[rows_local, cols] + jnp.where(low, ys[0], ys[1])
                decay = jnp.exp(jnp.where(low, tot[2 * t:2 * t + 1, :], tot[2 * t + 1:2 * t + 2, :]))
                st_ref[d, :, cols] = st * decay + jnp.where(low, sn[0], sn[1])

    half_chunks = n_chunks // 2

    def body_first(c, carry):
        one_chunk(c, 0, y_refs[0], c)
        one_chunk(n_chunks - 1 - c, 1, y_refs[1], half_chunks - 1 - c)
        return carry

    def body_second(c, carry):
        one_chunk(c, 0, y_refs[1], c - half_chunks)
        one_chunk(n_chunks - 1 - c, 1, y_refs[0], n_chunks - 1 - c)
        return carry

    lax.fori_loop(0, half_chunks, body_first, 0)
    lax.fori_loop(half_chunks, n_chunks, body_second, 0)

    group_w = SSD_WIDTH // SSD_GROUPS
    for half, y_half in enumerate(y_refs):
        def fin(c, carry, half=half, y_half=y_half):
            rows = pl.ds(pl.multiple_of((half * half_chunks + c) * CHUNK, CHUNK), CHUNK)
            rows_local = pl.ds(pl.multiple_of(c * CHUNK, CHUNK), CHUNK)
            zz = z_ref[rows, :]
            y = (y_half[rows_local, :] + dsk_ref[...] * xc_ref[rows, 0:SSD_WIDTH]) * (zz * _sigmoid(zz))
            outs = []
            for g in range(SSD_GROUPS):
                yg = y[:, g * group_w:(g + 1) * group_w]
                ms = jnp.mean(yg * yg, axis=-1, keepdims=True)
                outs.append(yg * lax.rsqrt(ms + RMS_EPS))
            o_ref[rows, :] = (jnp.concatenate(outs, axis=-1) * nw_ref[...]).astype(BF16)
            return carry

        lax.fori_loop(0, half_chunks, fin, 0)


def _ssd_call(xc, halo, z, dt, cw, cb, dtb, alog, dsk, nw, tri, batch):
    T = xc.shape[0]
    n_chunks = SEQ // CHUNK
    seq = lambda b: (b, 0)
    return pl.pallas_call(
        _ssd_kernel, grid=(batch,),
        in_specs=[pl.BlockSpec((SEQ, CONV_CH), seq),
                  pl.BlockSpec((SEQ // NORM_SUB, SUBLANES, CONV_CH), lambda b: (b, 0, 0)),
                  pl.BlockSpec((SEQ, SSD_WIDTH), seq),
                  pl.BlockSpec((SEQ, DT_PAD), seq),
                  _const_spec(cw.shape), _const_spec(cb.shape), _const_spec(dtb.shape),
                  _const_spec(alog.shape), _const_spec(dsk.shape), _const_spec(nw.shape),
                  _const_spec(tri.shape)],
        out_specs=pl.BlockSpec((SEQ, SSD_WIDTH), seq),
        out_shape=jax.ShapeDtypeStruct((T, SSD_WIDTH), BF16),
        scratch_shapes=[pltpu.VMEM((SEQ, CONV_CH), F32),
                        pltpu.VMEM((SEQ // 2, SSD_WIDTH), F32),
                        pltpu.VMEM((SEQ // 2, SSD_WIDTH), F32),
                        pltpu.VMEM((N_DIRS, D_STATE, SSD_WIDTH), F32),
                        pltpu.VMEM((n_chunks, SSD_GROUPS * D_STATE, CHUNK), F32),
                        pltpu.VMEM((n_chunks, N_DIRS * SSD_HEADS, CHUNK), F32),
                        pltpu.VMEM((n_chunks, N_DIRS * SSD_HEADS, CHUNK), F32),
                        pltpu.VMEM((n_chunks, CHUNK, LANES), F32),
                        pltpu.VMEM((n_chunks, N_DIRS * SSD_GROUPS * CHUNK, CHUNK), BF16)],
        name="ssd_bidir",
        compiler_params=pltpu.CompilerParams(dimension_semantics=("arbitrary",),
                                             vmem_limit_bytes=VMEM_LIMIT),
    )(xc, halo, z, dt, cw, cb, dtb, alog, dsk, nw, tri)


def _back_kernel(x1_ref, att_ref, ssd_ref, p_ref, wout_ref, wgu_ref, wo_ref,
                 lng_ref, lnb_ref, wp_ref, wgate_ref, bgate_ref, o_ref, pre_ref):
    i = pl.program_id(0)
    last = pl.num_programs(0) - 1
    subs = [slice(r0, r0 + NORM_SUB) for r0 in range(0, x1_ref.shape[0], NORM_SUB)]

    def ffn_stage():
        x2_parts = []
        for rows in subs:
            mix = (jnp.dot(att_ref[rows, :], wout_ref[0:ATT_WIDTH, :], preferred_element_type=F32)
                   + jnp.dot(ssd_ref[rows, :], wout_ref[ATT_WIDTH:, :], preferred_element_type=F32))
            x2_parts.append(_layer_norm(ALPHA * x1_ref[rows, :] + mix, lng_ref[0:1, :], lnb_ref[0:1, :]))
        x2 = jnp.concatenate(x2_parts, axis=0)
        pre_ref[...] = ALPHA * x2 + 0.5 * _swiglu(x2.astype(BF16), wgu_ref, wo_ref)

    def embed_stage():
        for rows in subs:
            x3 = _layer_norm(pre_ref[rows, :], lng_ref[1:2, :], lnb_ref[1:2, :])
            e = jnp.dot(p_ref[rows, :].astype(BF16), wp_ref[...], preferred_element_type=F32)
            gate = _sigmoid(jnp.dot(x3.astype(BF16), wgate_ref[...], preferred_element_type=F32) + bgate_ref[...])
            o_ref[rows, :] = _layer_norm(ALPHA * x3 + gate * e, lng_ref[2:3, :], lnb_ref[2:3, :])

    @pl.when(i == 0)
    def _():
        pre_ref[...] = jnp.zeros_like(pre_ref)

    @pl.when(i < last)
    def _():
        embed_stage()
        ffn_stage()

    @pl.when(i == last)
    def _():
        embed_stage()


def _back_call(x1, att, ssd, p2d, wout, wgu, wo, lng, lnb, wp, wgate, bgate):
    T = x1.shape[0]
    tm = TOKEN_TILE
    n_tiles = T // tm
    cur = lambda i: (jnp.minimum(i, n_tiles - 1), 0)
    late = lambda i: (jnp.maximum(i - 1, 0), 0)
    consts = (wout, wgu, wo, lng, lnb, wp, wgate, bgate)
    return pl.pallas_call(
        _back_kernel, grid=(n_tiles + 1,),
        in_specs=[pl.BlockSpec((tm, D_MODEL), cur), pl.BlockSpec((tm, ATT_WIDTH), cur),
                  pl.BlockSpec((tm, SSD_WIDTH), cur), pl.BlockSpec((tm, PLE_DIM), late)]
                 + [_const_spec(c.shape) for c in consts],
        out_specs=pl.BlockSpec((tm, D_MODEL), late),
        out_shape=jax.ShapeDtypeStruct((T, D_MODEL), F32),
        scratch_shapes=[pltpu.VMEM((tm, D_MODEL), F32)],
        name="back_outproj_ffn_ple",
        compiler_params=pltpu.CompilerParams(dimension_semantics=("arbitrary",),
                                             vmem_limit_bytes=VMEM_LIMIT),
    )(x1, att, ssd, p2d, *consts)


def _in_proj_weight(w):
    d = w.shape[0]
    half_blk = HEAD_DIM // 2
    n_q = N_Q_HEADS * HEAD_DIM
    n_kv = N_KV_HEADS * HEAD_DIM
    wq = w[:, :n_q].reshape(d, N_Q_HEADS // 2, 2, half_blk, 2).transpose(0, 1, 4, 2, 3).reshape(d, n_q)
    wk = w[:, n_q:n_q + n_kv].reshape(d, N_KV_HEADS, half_blk, 2).transpose(0, 1, 3, 2)
    wk = jnp.broadcast_to(wk[:, :, :, None, :], (d, N_KV_HEADS, 2, 2, half_blk)).reshape(d, N_KV_HEADS * LANES)
    wv = w[:, n_q + n_kv:n_q + 2 * n_kv].reshape(d, N_KV_HEADS, 1, HEAD_DIM)
    wv = jnp.broadcast_to(wv, (d, N_KV_HEADS, 2, HEAD_DIM)).reshape(d, N_KV_HEADS * LANES)
    rest = w[:, n_q + 2 * n_kv:]
    pad = jnp.zeros((d, DT_PAD - N_DIRS * SSD_HEADS), w.dtype)
    return jnp.concatenate([wq, wk, wv, rest, pad], axis=1).astype(BF16)


def _rope_tiles():
    rows = SEQ // GRID_W
    row = jnp.repeat(jnp.arange(rows, dtype=F32), GRID_W)
    col = jnp.tile(jnp.arange(GRID_W, dtype=F32), rows)
    inv = ROPE_THETA ** (-jnp.arange(0, ROPE_AXIS_DIM, 2, dtype=F32) / ROPE_AXIS_DIM)
    ang = jnp.concatenate([row[:, None] * inv, col[:, None] * inv], axis=-1)
    cos, sin = jnp.cos(ang), jnp.sin(ang)
    cos_t = jnp.concatenate([cos, cos, cos, cos], axis=-1)
    sin_t = jnp.concatenate([-sin, -sin, sin, sin], axis=-1)
    return cos_t, sin_t


def _norm_tile(w):
    lanes = np.arange(LANES)
    idx = 2 * (lanes % (HEAD_DIM // 2)) + lanes // HALF_LANES
    return w[idx][None, :]


def kernel(x, p, ln_g, ln_b, ffn1_w_in, ffn1_w_out, w_in, q_norm, k_norm, conv_w, conv_b, dt_bias,
           a_log, d_skip, ssd_norm, w_out, ffn2_w_in, ffn2_w_out, ple_w, ple_gate_w, ple_gate_b):
    B, S, D = x.shape
    assert (S, D) == (SEQ, D_MODEL) and ln_g.shape[0] == DEPTH == 1
    T = B * S
    x2d = x.reshape(T, D)
    p2d = p[0].reshape(T, PLE_DIM)

    cos_t, sin_t = _rope_tiles()
    lanes = np.arange(LANES)
    same_head = ((lanes[:, None] // (HEAD_DIM // 2)) % 2) == ((lanes[None, :] // (HEAD_DIM // 2)) % 2)
    two_tiles = np.eye(2, dtype=np.float32)
    mq = jnp.asarray(np.kron(two_tiles, np.where(same_head, 1.0 / HEAD_DIM, 0.0)), BF16)
    mk = jnp.asarray(np.kron(two_tiles, np.full((LANES, LANES), 1.0 / LANES)), BF16)
    ones = np.ones((CHUNK, CHUNK), np.float32)
    tri = jnp.asarray(np.stack([np.tril(ones), np.triu(ones)]), BF16)

    i = 0
    wcat = _in_proj_weight(w_in[i])
    qw = _norm_tile(q_norm[i]) * (HEAD_DIM ** -0.5 * LOG2E)
    kw = _norm_tile(k_norm[i])
    x1, qh, kh, vh, z, xc, halo, dtr = _front_call(
        x2d, ffn1_w_in[i].astype(BF16), ffn1_w_out[i].astype(BF16), ln_g[i, 0:1], ln_b[i, 0:1],
        wcat, cos_t, sin_t, qw, kw, mq, mk, conv_w[i], conv_b[i][None, :])

    att = _attn_call(qh, kh, vh, B)

    dsk = jnp.repeat(d_skip[i], SSD_HEAD_DIM)[None, :]
    ssd = _ssd_call(xc, halo, z, dtr, conv_w[i], conv_b[i][None, :], dt_bias[i].reshape(-1, 1), a_log[i].reshape(-1, 1),
                    dsk, ssd_norm[i][None, :], tri, B)

    out = _back_call(x1, att, ssd, p2d, w_out[i].astype(BF16),
                     ffn2_w_in[i].astype(BF16), ffn2_w_out[i].astype(BF16), ln_g[i, 1:4], ln_b[i, 1:4],
                     ple_w[i].astype(BF16), ple_gate_w[i].astype(BF16), ple_gate_b[i][None, :])
    return out.reshape(B, S, D)
```

```python
import functools
import math

import numpy as np
import jax
import jax.numpy as jnp
from jax import lax
from jax.experimental import pallas as pl
from jax.experimental.pallas import tpu as pltpu

F32 = jnp.float32
BF16 = jnp.bfloat16

D_MODEL = 1024
SEQ = 2048
DEPTH = 1
HEAD_DIM = 64
N_Q_HEADS = 8
N_KV_HEADS = 2
ROPE_AXIS_DIM = HEAD_DIM // 2
ROPE_THETA = 10000.0
GRID_W = 64
ATT_WIDTH = N_Q_HEADS * HEAD_DIM
SSD_WIDTH = 512
SSD_HEAD_DIM = 64
SSD_HEADS = 8
SSD_GROUPS = 2
D_STATE = 128
CONV_WIDTH = 5
CONV_CH = SSD_WIDTH + 2 * SSD_GROUPS * D_STATE
CHUNK = 128
N_DIRS = 2
D_FF = 2816
PLE_DIM = 256
ALPHA = (2.0 * DEPTH) ** 0.25
LN_EPS = 1e-5
RMS_EPS = 1e-6
LOG2E = 1.4426950408889634

LANES = 128
HALF_LANES = LANES // 2
SUBLANES = 8
HALO_ROWS = SUBLANES // 2
TOKEN_TILE = 512
NORM_SUB = 256
Q_TILE = 1024
Q_UNIT = 512
FF_CHUNKS = ((0, 768), (768, 768), (1536, 768), (2304, 512))
DT_PAD = LANES
Q_OFF = 0
K_OFF = Q_OFF + ATT_WIDTH
V_OFF = K_OFF + N_KV_HEADS * LANES
Z_OFF = V_OFF + N_KV_HEADS * LANES
XBC_OFF = Z_OFF + SSD_WIDTH
DT_OFF = XBC_OFF + CONV_CH
IN_COLS = DT_OFF + DT_PAD
VMEM_LIMIT = 56 * 1024 * 1024


def _const_spec(shape):
    nd = len(shape)
    return pl.BlockSpec(shape, lambda *_: (0,) * nd, pipeline_mode=pl.Buffered(1))


def _layer_norm(y, g, b):
    mu = jnp.mean(y, axis=-1, keepdims=True)
    yc = y - mu
    var = jnp.mean(yc * yc, axis=-1, keepdims=True)
    return yc * lax.rsqrt(var + LN_EPS) * g + b


def _sigmoid(x):
    return 1.0 / (1.0 + jnp.exp(-x))


def _swiglu(xb, wgu_ref, wo_ref, before=()):
    acc = None
    for k, (c0, cw) in enumerate(FF_CHUNKS):
        if k < len(before):
            before[k]()
        hg = jnp.dot(xb, wgu_ref[:, c0:c0 + cw], preferred_element_type=F32)
        hu = jnp.dot(xb, wgu_ref[:, D_FF + c0:D_FF + c0 + cw], preferred_element_type=F32)
        act = (hg * _sigmoid(hg) * hu).astype(BF16)
        part = jnp.dot(act, wo_ref[c0:c0 + cw, :], preferred_element_type=F32)
        acc = part if acc is None else acc + part
    return acc


def _split2(x):
    hi = x.astype(BF16)
    lo = (x - hi.astype(F32)).astype(BF16)
    return hi, lo


def _split3(x):
    hi = x.astype(BF16)
    r1 = x - hi.astype(F32)
    mid = r1.astype(BF16)
    lo = (r1 - mid.astype(F32)).astype(BF16)
    return hi, mid, lo


def _norm_rope(t2, seg_mean, w, cos, sin):
    hi, lo = _split2(t2 * t2)
    ms = (jnp.dot(hi, seg_mean, preferred_element_type=F32)
          + jnp.dot(lo, seg_mean, preferred_element_type=F32))
    tn2 = t2 * lax.rsqrt(ms + RMS_EPS)
    outs = []
    for j in range(2):
        tn = tn2[:, j * LANES:(j + 1) * LANES] * w
        outs.append(tn * cos + pltpu.roll(tn, HALF_LANES, axis=1) * sin)
    return outs


def _front_kernel(x_ref, wgu_ref, wo_ref, g_ref, b_ref, win_ref,
                  cos_ref, sin_ref, qw_ref, kw_ref, mq_ref, mk_ref, cw_ref, cb_ref,
                  x1_ref, q_ref, k_ref, v_ref, z_ref, xc_ref, halo_ref, dt_ref, pre_ref):
    i = pl.program_id(0)
    last = pl.num_programs(0) - 1

    raw = {}

    def project(r0):
        rows = slice(r0, r0 + NORM_SUB)
        pair = 2 * LANES
        x1 = _layer_norm(pre_ref[rows, :], g_ref[...], b_ref[...])
        x1_ref[rows, :] = x1
        u = jnp.dot(x1.astype(BF16), win_ref[...], preferred_element_type=F32)
        cos = cos_ref[rows, :]
        sin = sin_ref[rows, :]
        for jp in range(ATT_WIDTH // pair):
            tiles = _norm_rope(u[:, Q_OFF + jp * pair:Q_OFF + (jp + 1) * pair], mq_ref[...], qw_ref[...], cos, sin)
            for j, tile in enumerate(tiles):
                c0 = jp * pair + j * LANES
                q_ref[rows, c0:c0 + LANES] = tile.astype(BF16)
        tiles = _norm_rope(u[:, K_OFF:K_OFF + pair], mk_ref[...], kw_ref[...], cos, sin)
        for j, tile in enumerate(tiles):
            k_ref[rows, j * LANES:(j + 1) * LANES] = tile.astype(BF16)
        v_ref[rows, :] = u[:, V_OFF:Z_OFF].astype(BF16)
        z_ref[rows, :] = u[:, Z_OFF:XBC_OFF]
        dt_ref[rows, :] = u[:, DT_OFF:IN_COLS]
        raw[r0] = u[:, XBC_OFF:DT_OFF]

    def conv(r0):
        rows = slice(r0, r0 + NORM_SUB)
        xraw = raw[r0]
        for c in range(CONV_CH // LANES):
            cols = slice(c * LANES, (c + 1) * LANES)
            xc_ref[rows, cols] = _conv_tile(xraw[:, cols], cw_ref, cb_ref, cols)
        first_rows = lax.broadcasted_iota(jnp.int32, (SUBLANES, 1), 0) < HALO_ROWS
        halo_ref[r0 // NORM_SUB] = jnp.where(first_rows, xraw[0:SUBLANES, :], xraw[NORM_SUB - SUBLANES:, :])

    late_steps = [functools.partial(step, r0) for r0 in range(0, pre_ref.shape[0], NORM_SUB)
                  for step in (project, conv)]

    @pl.when(i == 0)
    def _():
        pre_ref[...] = jnp.zeros_like(pre_ref)

    @pl.when(i < last)
    def _():
        x = x_ref[...]
        f = _swiglu(x.astype(BF16), wgu_ref, wo_ref, before=late_steps)
        pre_ref[...] = ALPHA * x + 0.5 * f

    @pl.when(i == last)
    def _():
        for step in late_steps:
            step()


def _front_call(x2d, wgu, wo, g, b, wcat, cos_t, sin_t, qw, kw, mq, mk, cw, cb):
    T = x2d.shape[0]
    tm = TOKEN_TILE
    n_tiles = T // tm
    pos_blocks = SEQ // tm
    cur = lambda i: (jnp.minimum(i, n_tiles - 1), 0)
    late = lambda i: (jnp.maximum(i - 1, 0), 0)
    pos = lambda i: (jnp.maximum(i - 1, 0) % pos_blocks, 0)
    out_shape = (
        jax.ShapeDtypeStruct((T, D_MODEL), F32),
        jax.ShapeDtypeStruct((T, ATT_WIDTH), BF16),
        jax.ShapeDtypeStruct((T, N_KV_HEADS * LANES), BF16),
        jax.ShapeDtypeStruct((T, N_KV_HEADS * LANES), BF16),
        jax.ShapeDtypeStruct((T, SSD_WIDTH), F32),
        jax.ShapeDtypeStruct((T, CONV_CH), F32),
        jax.ShapeDtypeStruct((T // NORM_SUB, SUBLANES, CONV_CH), F32),
        jax.ShapeDtypeStruct((T, DT_PAD), F32),
    )
    in_specs = [
        pl.BlockSpec((tm, D_MODEL), cur),
        _const_spec(wgu.shape), _const_spec(wo.shape),
        _const_spec(g.shape), _const_spec(b.shape), _const_spec(wcat.shape),
        pl.BlockSpec((tm, LANES), pos), pl.BlockSpec((tm, LANES), pos),
        _const_spec(qw.shape), _const_spec(kw.shape), _const_spec(mq.shape), _const_spec(mk.shape),
        _const_spec(cw.shape), _const_spec(cb.shape),
    ]
    late3 = lambda i: (jnp.maximum(i - 1, 0), 0, 0)
    out_specs = tuple(pl.BlockSpec((tm, s.shape[1]), late) if len(s.shape) == 2
                      else pl.BlockSpec((tm // NORM_SUB,) + s.shape[1:], late3) for s in out_shape)
    return pl.pallas_call(
        _front_kernel, grid=(n_tiles + 1,), in_specs=in_specs, out_specs=out_specs, out_shape=out_shape,
        scratch_shapes=[pltpu.VMEM((tm, D_MODEL), F32)],
        name="front_ffn_inproj",
        compiler_params=pltpu.CompilerParams(dimension_semantics=("arbitrary",),
                                             vmem_limit_bytes=VMEM_LIMIT),
    )(x2d, wgu, wo, g, b, wcat, cos_t, sin_t, qw, kw, mq, mk, cw, cb)


def _attn_kernel(q_ref, k_ref, v_ref, o_ref):
    tq = q_ref.shape[0]
    lane = lax.broadcasted_iota(jnp.int32, (1, LANES), 1)
    low = lane < HALF_LANES
    q_per_kv_tiles = (N_Q_HEADS // N_KV_HEADS) * HEAD_DIM // LANES
    for kv in range(N_KV_HEADS):
        kt = k_ref[:, kv * LANES:(kv + 1) * LANES]
        vt = v_ref[:, kv * LANES:(kv + 1) * LANES]
        zero = jnp.zeros_like(vt)
        v_lo = jnp.where(low, vt, zero)
        v_hi = jnp.where(low, zero, vt)
        blk = (lane // (HEAD_DIM // 2)) % 2
        for jj in range(q_per_kv_tiles):
            j = kv * q_per_kv_tiles + jj
            cols = slice(j * LANES, (j + 1) * LANES)
            for r0 in range(0, tq, Q_UNIT):
                rows = slice(r0, r0 + Q_UNIT)
                qt = q_ref[rows, cols]
                zq = jnp.zeros_like(qt)
                out = None
                for a, v_a in ((0, v_lo), (1, v_hi)):
                    qa = jnp.where(blk == a, qt, zq)
                    s = lax.dot_general(qa, kt, (((1,), (1,)), ((), ())), preferred_element_type=F32)
                    m = jnp.max(s, axis=-1, keepdims=True)
                    p = jnp.exp2(s - m)
                    l = jnp.sum(p, axis=-1, keepdims=True)
                    o = jnp.dot(p.astype(BF16), v_a, preferred_element_type=F32) * (1.0 / l)
                    out = o if out is None else out + o
                o_ref[rows, cols] = out.astype(BF16)


def _attn_call(q, k, v, batch):
    T = q.shape[0]
    nq = SEQ // Q_TILE
    return pl.pallas_call(
        _attn_kernel, grid=(batch, nq),
        in_specs=[pl.BlockSpec((Q_TILE, ATT_WIDTH), lambda b, i: (b * nq + i, 0)),
                  pl.BlockSpec((SEQ, N_KV_HEADS * LANES), lambda b, i: (b, 0)),
                  pl.BlockSpec((SEQ, N_KV_HEADS * LANES), lambda b, i: (b, 0))],
        out_specs=pl.BlockSpec((Q_TILE, ATT_WIDTH), lambda b, i: (b * nq + i, 0)),
        out_shape=jax.ShapeDtypeStruct((T, ATT_WIDTH), BF16),
        name="attention",
        compiler_params=pltpu.CompilerParams(dimension_semantics=("arbitrary", "arbitrary"),
                                             vmem_limit_bytes=VMEM_LIMIT),
    )(q, k, v)


def _softplus(x):
    return jnp.maximum(x, 0.0) + jnp.log1p(jnp.exp(-jnp.abs(x)))


def _dot3(a_pieces, b, dims):
    out = None
    for piece in a_pieces:
        lhs, rhs = (piece, b) if dims == "piece_lhs" else (b, piece)
        contract = (((1,), (0,)), ((), ())) if dims == "piece_lhs" else (((1,), (1,)), ((), ()))
        term = lax.dot_general(lhs, rhs, contract, preferred_element_type=F32)
        out = term if out is None else out + term
    return out


def _conv_tile(xin, w_ref, b_ref, cols):
    R = xin.shape[0]
    half = CONV_WIDTH // 2
    acc = b_ref[:, cols] + w_ref[half:half + 1, cols] * xin
    for j in range(CONV_WIDTH):
        off = j - half
        if off != 0:
            acc = acc + w_ref[j:j + 1, cols] * pltpu.roll(xin, (-off) % R, axis=0)
    return acc * _sigmoid(acc)


def _ssd_kernel(xcin_ref, halo_ref, z_ref, dt_ref, cw_ref, cb_ref, dtb_ref, alog_ref, dsk_ref, nw_ref, tri_ref,
                o_ref, xc_ref, ylo_ref, yhi_ref, st_ref, bt_ref, dtt_ref, cumrow_ref, cumcol_ref, gm_ref):
    y_refs = (ylo_ref, yhi_ref)
    S = xcin_ref.shape[0]
    n_chunks = S // CHUNK
    pair_tiles = SSD_WIDTH // LANES
    tiles_per_group = pair_tiles // SSD_GROUPS
    n_heads_all = N_DIRS * SSD_HEADS
    b_off = SSD_WIDTH
    c_off = SSD_WIDTH + SSD_GROUPS * D_STATE
    lane = lax.broadcasted_iota(jnp.int32, (1, LANES), 1)
    low = lane < HALF_LANES

    xc_ref[...] = xcin_ref[...]
    n_groups = S // NORM_SUB
    row8 = lax.broadcasted_iota(jnp.int32, (SUBLANES, 1), 0)
    no_rows = jnp.zeros((SUBLANES, LANES), F32)
    for k in range(n_groups + 1):
        b = k * NORM_SUB
        for c in range(CONV_CH // LANES):
            cols = slice(c * LANES, (c + 1) * LANES)
            before = halo_ref[k - 1, :, cols] if k > 0 else no_rows
            after = halo_ref[k, :, cols] if k < n_groups else no_rows
            window = jnp.where(row8 < HALO_ROWS, pltpu.roll(before, HALO_ROWS, axis=0),
                               pltpu.roll(after, HALO_ROWS, axis=0))
            fixed = pltpu.roll(_conv_tile(window, cw_ref, cb_ref, cols), HALO_ROWS, axis=0)
            if k > 0:
                blk = slice(b - SUBLANES, b)
                xc_ref[blk, cols] = jnp.where(row8 >= SUBLANES - CONV_WIDTH // 2, fixed, xc_ref[blk, cols])
            if k < n_groups:
                blk = slice(b, b + SUBLANES)
                xc_ref[blk, cols] = jnp.where(row8 < CONV_WIDTH // 2, fixed, xc_ref[blk, cols])

    for y_half in y_refs:
        y_half[...] = jnp.zeros_like(y_half)
    st_ref[...] = jnp.zeros_like(st_ref)

    a_col = -jnp.exp(alog_ref[...])
    dtb_col = dtb_ref[...]
    r_i = lax.broadcasted_iota(jnp.int32, (CHUNK, CHUNK), 0)
    c_i = lax.broadcasted_iota(jnp.int32, (CHUNK, CHUNK), 1)
    masks = (c_i <= r_i, c_i >= r_i)
    zero_rows = jnp.zeros((CHUNK - n_heads_all, CHUNK), BF16)
    fwd_rows = lax.broadcasted_iota(jnp.int32, (n_heads_all, 1), 0) < SSD_HEADS
    fwd_lanes = lane < SSD_HEADS

    def prep_chunk(c):
        rows = pl.ds(pl.multiple_of(c * CHUNK, CHUNK), CHUNK)
        dt_all = _softplus(dt_ref[rows, :].T[0:n_heads_all, :] + dtb_col)
        dtt_ref[c] = dt_all
        pieces = _split3(dt_all * a_col)
        padded = [jnp.concatenate([pc, zero_rows], axis=0) for pc in pieces]
        cumrow_ref[c] = jnp.where(fwd_rows, _dot3(pieces, tri_ref[1], "piece_lhs"),
                                  _dot3(pieces, tri_ref[0], "piece_lhs"))
        cumcol_ref[c] = jnp.where(fwd_lanes, _dot3(padded, tri_ref[0], "piece_rhs"),
                                  _dot3(padded, tri_ref[1], "piece_rhs"))
        for g in range(SSD_GROUPS):
            bt = xc_ref[rows, b_off + g * D_STATE:b_off + (g + 1) * D_STATE].T
            bt_ref[c, g * D_STATE:(g + 1) * D_STATE, :] = bt
            c_b = xc_ref[rows, c_off + g * D_STATE:c_off + (g + 1) * D_STATE].astype(BF16)
            gmat = jnp.dot(c_b, bt.astype(BF16), preferred_element_type=F32)
            for d in range(N_DIRS):
                k = d * SSD_GROUPS + g
                gm_ref[c, k * CHUNK:(k + 1) * CHUNK, :] = jnp.where(masks[d], gmat, 0.0).astype(BF16)

    prep_unroll = 4

    def prep(i, carry):
        for u in range(prep_unroll):
            prep_chunk(i * prep_unroll + u)
        return carry

    lax.fori_loop(0, n_chunks // prep_unroll, prep, 0)

    def one_chunk(c, d, y_half, c_local):
        rows = pl.ds(pl.multiple_of(c * CHUNK, CHUNK), CHUNK)
        rows_local = pl.ds(pl.multiple_of(c_local * CHUNK, CHUNK), CHUNK)
        hs = slice(d * SSD_HEADS, (d + 1) * SSD_HEADS)
        dt_d = dtt_ref[c][hs]
        cum_row = cumrow_ref[c][hs]
        cum_col = cumcol_ref[c]
        tot = jnp.sum(dt_d * a_col[hs], axis=1, keepdims=True)
        w_row = dt_d * jnp.exp(tot - cum_row)
        for g in range(SSD_GROUPS):
            c_b = xc_ref[rows, c_off + g * D_STATE:c_off + (g + 1) * D_STATE].astype(BF16)
            bt_g = bt_ref[c, g * D_STATE:(g + 1) * D_STATE, :]
            k = d * SSD_GROUPS + g
            gm_b = gm_ref[c, k * CHUNK:(k + 1) * CHUNK, :]
            for tt in range(tiles_per_group):
                t = g * tiles_per_group + tt
                cols = slice(t * LANES, (t + 1) * LANES)
                x_t = xc_ref[rows, cols].astype(BF16)
                st = st_ref[d, :, cols]
                rhs = jnp.concatenate([x_t, st.astype(BF16)], axis=0)
                ys, sn = [], []
                for hh in (2 * t, 2 * t + 1):
                    col = d * SSD_HEADS + hh
                    colb = jnp.broadcast_to(cum_col[:, col:col + 1], (CHUNK, CHUNK))
                    seg = jnp.minimum(colb - cum_row[hh:hh + 1, :], 0.0)
                    w = (jnp.exp(seg) * dt_d[hh:hh + 1, :]).astype(BF16) * gm_b
                    a = jnp.exp(colb).astype(BF16) * c_b
                    lhs = jnp.concatenate([w, a], axis=1)
                    ys.append(jnp.dot(lhs, rhs, preferred_element_type=F32))
                    bts = (bt_g * w_row[hh:hh + 1, :]).astype(BF16)
                    sn.append(jnp.dot(bts, x_t, preferred_element_type=F32))
                y_half[rows_local, cols] = y_half[rows_local, cols] + jnp.where(low, ys[0], ys[1])
                decay = jnp.exp(jnp.where(low, tot[2 * t:2 * t + 1, :], tot[2 * t + 1:2 * t + 2, :]))
                st_ref[d, :, cols] = st * decay + jnp.where(low, sn[0], sn[1])

    half_chunks = n_chunks // 2

    def body_first(c, carry):
        one_chunk(c, 0, y_refs[0], c)
        one_chunk(n_chunks - 1 - c, 1, y_refs[1], half_chunks - 1 - c)
        return carry

    def body_second(c, carry):
        one_chunk(c, 0, y_refs[1], c - half_chunks)
        one_chunk(n_chunks - 1 - c, 1, y_refs[0], n_chunks - 1 - c)
        return carry

    lax.fori_loop(0, half_chunks, body_first, 0)
    lax.fori_loop(half_chunks, n_chunks, body_second, 0)

    group_w = SSD_WIDTH // SSD_GROUPS
    for half, y_half in enumerate(y_refs):
        def fin(c, carry, half=half, y_half=y_half):
            rows = pl.ds(pl.multiple_of((half * half_chunks + c) * CHUNK, CHUNK), CHUNK)
            rows_local = pl.ds(pl.multiple_of(c * CHUNK, CHUNK), CHUNK)
            zz = z_ref[rows, :]
            y = (y_half[rows_local, :] + dsk_ref[...] * xc_ref[rows, 0:SSD_WIDTH]) * (zz * _sigmoid(zz))
            outs = []
            for g in range(SSD_GROUPS):
                yg = y[:, g * group_w:(g + 1) * group_w]
                ms = jnp.mean(yg * yg, axis=-1, keepdims=True)
                outs.append(yg * lax.rsqrt(ms + RMS_EPS))
            o_ref[rows, :] = (jnp.concatenate(outs, axis=-1) * nw_ref[...]).astype(BF16)
            return carry

        lax.fori_loop(0, half_chunks, fin, 0)


def _ssd_call(xc, halo, z, dt, cw, cb, dtb, alog, dsk, nw, tri, batch):
    T = xc.shape[0]
    n_chunks = SEQ // CHUNK
    seq = lambda b: (b, 0)
    return pl.pallas_call(
        _ssd_kernel, grid=(batch,),
        in_specs=[pl.BlockSpec((SEQ, CONV_CH), seq),
                  pl.BlockSpec((SEQ // NORM_SUB, SUBLANES, CONV_CH), lambda b: (b, 0, 0)),
                  pl.BlockSpec((SEQ, SSD_WIDTH), seq),
                  pl.BlockSpec((SEQ, DT_PAD), seq),
                  _const_spec(cw.shape), _const_spec(cb.shape), _const_spec(dtb.shape),
                  _const_spec(alog.shape), _const_spec(dsk.shape), _const_spec(nw.shape),
                  _const_spec(tri.shape)],
        out_specs=pl.BlockSpec((SEQ, SSD_WIDTH), seq),
        out_shape=jax.ShapeDtypeStruct((T, SSD_WIDTH), BF16),
        scratch_shapes=[pltpu.VMEM((SEQ, CONV_CH), F32),
                        pltpu.VMEM((SEQ // 2, SSD_WIDTH), F32),
                        pltpu.VMEM((SEQ // 2, SSD_WIDTH), F32),
                        pltpu.VMEM((N_DIRS, D_STATE, SSD_WIDTH), F32),
                        pltpu.VMEM((n_chunks, SSD_GROUPS * D_STATE, CHUNK), F32),
                        pltpu.VMEM((n_chunks, N_DIRS * SSD_HEADS, CHUNK), F32),
                        pltpu.VMEM((n_chunks, N_DIRS * SSD_HEADS, CHUNK), F32),
                        pltpu.VMEM((n_chunks, CHUNK, LANES), F32),
                        pltpu.VMEM((n_chunks, N_DIRS * SSD_GROUPS * CHUNK, CHUNK), BF16)],
        name="ssd_bidir",
        compiler_params=pltpu.CompilerParams(dimension_semantics=("arbitrary",),
                                             vmem_limit_bytes=VMEM_LIMIT),
    )(xc, halo, z, dt, cw, cb, dtb, alog, dsk, nw, tri)


def _back_kernel(x1_ref, att_ref, ssd_ref, p_ref, wout_ref, wgu_ref, wo_ref,
                 lng_ref, lnb_ref, wp_ref, wgate_ref, bgate_ref, o_ref, pre_ref):
    i = pl.program_id(0)
    last = pl.num_programs(0) - 1
    subs = [slice(r0, r0 + NORM_SUB) for r0 in range(0, x1_ref.shape[0], NORM_SUB)]

    def embed(rows):
        x3 = _layer_norm(pre_ref[rows, :], lng_ref[1:2, :], lnb_ref[1:2, :])
        e = jnp.dot(p_ref[rows, :].astype(BF16), wp_ref[...], preferred_element_type=F32)
        gate = _sigmoid(jnp.dot(x3.astype(BF16), wgate_ref[...], preferred_element_type=F32) + bgate_ref[...])
        o_ref[rows, :] = _layer_norm(ALPHA * x3 + gate * e, lng_ref[2:3, :], lnb_ref[2:3, :])

    late_steps = [functools.partial(embed, rows) for rows in subs]

    def ffn_stage():
        x2_parts = []
        for rows in subs:
            mix = (jnp.dot(att_ref[rows, :], wout_ref[0:ATT_WIDTH, :], preferred_element_type=F32)
                   + jnp.dot(ssd_ref[rows, :], wout_ref[ATT_WIDTH:, :], preferred_element_type=F32))
            x2_parts.append(_layer_norm(ALPHA * x1_ref[rows, :] + mix, lng_ref[0:1, :], lnb_ref[0:1, :]))
        x2 = jnp.concatenate(x2_parts, axis=0)
        f = _swiglu(x2.astype(BF16), wgu_ref, wo_ref, before=late_steps)
        pre_ref[...] = ALPHA * x2 + 0.5 * f

    @pl.when(i == 0)
    def _():
        pre_ref[...] = jnp.zeros_like(pre_ref)

    @pl.when(i < last)
    def _():
        ffn_stage()

    @pl.when(i == last)
    def _():
        for step in late_steps:
            step()


def _back_call(x1, att, ssd, p2d, wout, wgu, wo, lng, lnb, wp, wgate, bgate):
    T = x1.shape[0]
    tm = TOKEN_TILE
    n_tiles = T // tm
    cur = lambda i: (jnp.minimum(i, n_tiles - 1), 0)
    late = lambda i: (jnp.maximum(i - 1, 0), 0)
    consts = (wout, wgu, wo, lng, lnb, wp, wgate, bgate)
    return pl.pallas_call(
        _back_kernel, grid=(n_tiles + 1,),
        in_specs=[pl.BlockSpec((tm, D_MODEL), cur), pl.BlockSpec((tm, ATT_WIDTH), cur),
                  pl.BlockSpec((tm, SSD_WIDTH), cur), pl.BlockSpec((tm, PLE_DIM), late)]
                 + [_const_spec(c.shape) for c in consts],
        out_specs=pl.BlockSpec((tm, D_MODEL), late),
        out_shape=jax.ShapeDtypeStruct((T, D_MODEL), F32),
        scratch_shapes=[pltpu.VMEM((tm, D_MODEL), F32)],
        name="back_outproj_ffn_ple",
        compiler_params=pltpu.CompilerParams(dimension_semantics=("arbitrary",),
                                             vmem_limit_bytes=VMEM_LIMIT),
    )(x1, att, ssd, p2d, *consts)


def _in_proj_weight(w):
    d = w.shape[0]
    half_blk = HEAD_DIM // 2
    n_q = N_Q_HEADS * HEAD_DIM
    n_kv = N_KV_HEADS * HEAD_DIM
    wq = w[:, :n_q].reshape(d, N_Q_HEADS // 2, 2, half_blk, 2).transpose(0, 1, 4, 2, 3).reshape(d, n_q)
    wk = w[:, n_q:n_q + n_kv].reshape(d, N_KV_HEADS, half_blk, 2).transpose(0, 1, 3, 2)
    wk = jnp.broadcast_to(wk[:, :, :, None, :], (d, N_KV_HEADS, 2, 2, half_blk)).reshape(d, N_KV_HEADS * LANES)
    wv = w[:, n_q + n_kv:n_q + 2 * n_kv].reshape(d, N_KV_HEADS, 1, HEAD_DIM)
    wv = jnp.broadcast_to(wv, (d, N_KV_HEADS, 2, HEAD_DIM)).reshape(d, N_KV_HEADS * LANES)
    rest = w[:, n_q + 2 * n_kv:]
    pad = jnp.zeros((d, DT_PAD - N_DIRS * SSD_HEADS), w.dtype)
    return jnp.concatenate([wq, wk, wv, rest, pad], axis=1).astype(BF16)


def _rope_tiles():
    rows = SEQ // GRID_W
    row = jnp.repeat(jnp.arange(rows, dtype=F32), GRID_W)
    col = jnp.tile(jnp.arange(GRID_W, dtype=F32), rows)
    inv = ROPE_THETA ** (-jnp.arange(0, ROPE_AXIS_DIM, 2, dtype=F32) / ROPE_AXIS_DIM)
    ang = jnp.concatenate([row[:, None] * inv, col[:, None] * inv], axis=-1)
    cos, sin = jnp.cos(ang), jnp.sin(ang)
    cos_t = jnp.concatenate([cos, cos, cos, cos], axis=-1)
    sin_t = jnp.concatenate([-sin, -sin, sin, sin], axis=-1)
    return cos_t, sin_t


def _norm_tile(w):
    lanes = np.arange(LANES)
    idx = 2 * (lanes % (HEAD_DIM // 2)) + lanes // HALF_LANES
    return w[idx][None, :]


def kernel(x, p, ln_g, ln_b, ffn1_w_in, ffn1_w_out, w_in, q_norm, k_norm, conv_w, conv_b, dt_bias,
           a_log, d_skip, ssd_norm, w_out, ffn2_w_in, ffn2_w_out, ple_w, ple_gate_w, ple_gate_b):
    B, S, D = x.shape
    assert (S, D) == (SEQ, D_MODEL) and ln_g.shape[0] == DEPTH == 1
    T = B * S
    x2d = x.reshape(T, D)
    p2d = p[0].reshape(T, PLE_DIM)

    cos_t, sin_t = _rope_tiles()
    lanes = np.arange(LANES)
    same_head = ((lanes[:, None] // (HEAD_DIM // 2)) % 2) == ((lanes[None, :] // (HEAD_DIM // 2)) % 2)
    two_tiles = np.eye(2, dtype=np.float32)
    mq = jnp.asarray(np.kron(two_tiles, np.where(same_head, 1.0 / HEAD_DIM, 0.0)), BF16)
    mk = jnp.asarray(np.kron(two_tiles, np.full((LANES, LANES), 1.0 / LANES)), BF16)
    ones = np.ones((CHUNK, CHUNK), np.float32)
    tri = jnp.asarray(np.stack([np.tril(ones), np.triu(ones)]), BF16)

    i = 0
    wcat = _in_proj_weight(w_in[i])
    qw = _norm_tile(q_norm[i]) * (HEAD_DIM ** -0.5 * LOG2E)
    kw = _norm_tile(k_norm[i])
    x1, qh, kh, vh, z, xc, halo, dtr = _front_call(
        x2d, ffn1_w_in[i].astype(BF16), ffn1_w_out[i].astype(BF16), ln_g[i, 0:1], ln_b[i, 0:1],
        wcat, cos_t, sin_t, qw, kw, mq, mk, conv_w[i], conv_b[i][None, :])

    att = _attn_call(qh, kh, vh, B)

    dsk = jnp.repeat(d_skip[i], SSD_HEAD_DIM)[None, :]
    ssd = _ssd_call(xc, halo, z, dtr, conv_w[i], conv_b[i][None, :], dt_bias[i].reshape(-1, 1), a_log[i].reshape(-1, 1),
                    dsk, ssd_norm[i][None, :], tri, B)

    out = _back_call(x1, att, ssd, p2d, w_out[i].astype(BF16),
                     ffn2_w_in[i].astype(BF16), ffn2_w_out[i].astype(BF16), ln_g[i, 1:4], ln_b[i, 1:4],
                     ple_w[i].astype(BF16), ple_gate_w[i].astype(BF16), ple_gate_b[i][None, :])
    return out.reshape(B, S, D)
```

```python
import functools
import math

import numpy as np
import jax
import jax.numpy as jnp
from jax import lax
from jax.experimental import pallas as pl
from jax.experimental.pallas import tpu as pltpu

F32 = jnp.float32
BF16 = jnp.bfloat16

D_MODEL = 1024
SEQ = 2048
DEPTH = 1
HEAD_DIM = 64
N_Q_HEADS = 8
N_KV_HEADS = 2
ROPE_AXIS_DIM = HEAD_DIM // 2
ROPE_THETA = 10000.0
GRID_W = 64
ATT_WIDTH = N_Q_HEADS * HEAD_DIM
SSD_WIDTH = 512
SSD_HEAD_DIM = 64
SSD_HEADS = 8
SSD_GROUPS = 2
D_STATE = 128
CONV_WIDTH = 5
CONV_CH = SSD_WIDTH + 2 * SSD_GROUPS * D_STATE
CHUNK = 128
N_DIRS = 2
D_FF = 2816
PLE_DIM = 256
ALPHA = (2.0 * DEPTH) ** 0.25
LN_EPS = 1e-5
RMS_EPS = 1e-6
LOG2E = 1.4426950408889634

LANES = 128
HALF_LANES = LANES // 2
SUBLANES = 8
TOKEN_TILE = 512
NORM_SUB = 256
Q_TILE = 1024
Q_UNIT = 512
FF_CHUNKS = ((0, 768), (768, 768), (1536, 768), (2304, 512))
DT_PAD = LANES
Q_OFF = 0
K_OFF = Q_OFF + ATT_WIDTH
V_OFF = K_OFF + N_KV_HEADS * LANES
Z_OFF = V_OFF + N_KV_HEADS * LANES
XBC_OFF = Z_OFF + SSD_WIDTH
DT_OFF = XBC_OFF + CONV_CH
IN_COLS = DT_OFF + DT_PAD
VMEM_LIMIT = 56 * 1024 * 1024


def _const_spec(shape):
    nd = len(shape)
    return pl.BlockSpec(shape, lambda *_: (0,) * nd, pipeline_mode=pl.Buffered(1))


def _layer_norm(y, g, b):
    mu = jnp.mean(y, axis=-1, keepdims=True)
    yc = y - mu
    var = jnp.mean(yc * yc, axis=-1, keepdims=True)
    return yc * lax.rsqrt(var + LN_EPS) * g + b


def _sigmoid(x):
    return 1.0 / (1.0 + jnp.exp(-x))


def _swiglu(xb, wgu_ref, wo_ref, before=()):
    acc = None
    for k, (c0, cw) in enumerate(FF_CHUNKS):
        if k < len(before):
            before[k]()
        hg = jnp.dot(xb, wgu_ref[:, c0:c0 + cw], preferred_element_type=F32)
        hu = jnp.dot(xb, wgu_ref[:, D_FF + c0:D_FF + c0 + cw], preferred_element_type=F32)
        act = (hg * _sigmoid(hg) * hu).astype(BF16)
        part = jnp.dot(act, wo_ref[c0:c0 + cw, :], preferred_element_type=F32)
        acc = part if acc is None else acc + part
    return acc


def _split2(x):
    hi = x.astype(BF16)
    lo = (x - hi.astype(F32)).astype(BF16)
    return hi, lo


def _split3(x):
    hi = x.astype(BF16)
    r1 = x - hi.astype(F32)
    mid = r1.astype(BF16)
    lo = (r1 - mid.astype(F32)).astype(BF16)
    return hi, mid, lo


def _norm_rope(t2, seg_mean, w, cos, sin):
    ms = jnp.dot((t2 * t2).astype(BF16), seg_mean, preferred_element_type=F32)
    tn2 = t2 * lax.rsqrt(ms + RMS_EPS)
    outs = []
    for j in range(2):
        tn = tn2[:, j * LANES:(j + 1) * LANES] * w
        outs.append(tn * cos + pltpu.roll(tn, HALF_LANES, axis=1) * sin)
    return outs


def _front_kernel(x_ref, wgu_ref, wo_ref, g_ref, b_ref, win_ref,
                  cos_ref, sin_ref, qw_ref, kw_ref, mq_ref, mk_ref,
                  x1_ref, q_ref, k_ref, v_ref, z_ref, xbc_ref, dt_ref, pre_ref):
    i = pl.program_id(0)
    last = pl.num_programs(0) - 1

    def project(r0):
        rows = slice(r0, r0 + NORM_SUB)
        pair = 2 * LANES
        x1 = _layer_norm(pre_ref[rows, :], g_ref[...], b_ref[...])
        x1_ref[rows, :] = x1
        u = jnp.dot(x1.astype(BF16), win_ref[...], preferred_element_type=F32)
        cos = cos_ref[rows, :]
        sin = sin_ref[rows, :]
        for jp in range(ATT_WIDTH // pair):
            tiles = _norm_rope(u[:, Q_OFF + jp * pair:Q_OFF + (jp + 1) * pair], mq_ref[...], qw_ref[...], cos, sin)
            for j, tile in enumerate(tiles):
                c0 = jp * pair + j * LANES
                q_ref[rows, c0:c0 + LANES] = tile.astype(BF16)
        tiles = _norm_rope(u[:, K_OFF:K_OFF + pair], mk_ref[...], kw_ref[...], cos, sin)
        for j, tile in enumerate(tiles):
            k_ref[rows, j * LANES:(j + 1) * LANES] = tile.astype(BF16)
        v_ref[rows, :] = u[:, V_OFF:Z_OFF].astype(BF16)
        z_ref[rows, :] = u[:, Z_OFF:XBC_OFF]
        xbc_ref[rows, :] = u[:, XBC_OFF:DT_OFF]
        dt_ref[rows, :] = u[:, DT_OFF:IN_COLS]

    late_steps = [functools.partial(project, r0) for r0 in range(0, pre_ref.shape[0], NORM_SUB)]

    @pl.when(i == 0)
    def _():
        pre_ref[...] = jnp.zeros_like(pre_ref)

    @pl.when(i < last)
    def _():
        x = x_ref[...]
        f = _swiglu(x.astype(BF16), wgu_ref, wo_ref, before=late_steps)
        pre_ref[...] = ALPHA * x + 0.5 * f

    @pl.when(i == last)
    def _():
        for step in late_steps:
            step()


def _front_call(x2d, wgu, wo, g, b, wcat, cos_t, sin_t, qw, kw, mq, mk):
    T = x2d.shape[0]
    tm = TOKEN_TILE
    n_tiles = T // tm
    pos_blocks = SEQ // tm
    cur = lambda i: (jnp.minimum(i, n_tiles - 1), 0)
    late = lambda i: (jnp.maximum(i - 1, 0), 0)
    pos = lambda i: (jnp.maximum(i - 1, 0) % pos_blocks, 0)
    out_shape = (
        jax.ShapeDtypeStruct((T, D_MODEL), F32),
        jax.ShapeDtypeStruct((T, ATT_WIDTH), BF16),
        jax.ShapeDtypeStruct((T, N_KV_HEADS * LANES), BF16),
        jax.ShapeDtypeStruct((T, N_KV_HEADS * LANES), BF16),
        jax.ShapeDtypeStruct((T, SSD_WIDTH), F32),
        jax.ShapeDtypeStruct((T, CONV_CH), F32),
        jax.ShapeDtypeStruct((T, DT_PAD), F32),
    )
    in_specs = [
        pl.BlockSpec((tm, D_MODEL), cur),
        _const_spec(wgu.shape), _const_spec(wo.shape),
        _const_spec(g.shape), _const_spec(b.shape), _const_spec(wcat.shape),
        pl.BlockSpec((tm, LANES), pos), pl.BlockSpec((tm, LANES), pos),
        _const_spec(qw.shape), _const_spec(kw.shape), _const_spec(mq.shape), _const_spec(mk.shape),
    ]
    out_specs = tuple(pl.BlockSpec((tm, s.shape[1]), late) for s in out_shape)
    return pl.pallas_call(
        _front_kernel, grid=(n_tiles + 1,), in_specs=in_specs, out_specs=out_specs, out_shape=out_shape,
        scratch_shapes=[pltpu.VMEM((tm, D_MODEL), F32)],
        name="front_ffn_inproj",
        compiler_params=pltpu.CompilerParams(dimension_semantics=("arbitrary",),
                                             vmem_limit_bytes=VMEM_LIMIT),
    )(x2d, wgu, wo, g, b, wcat, cos_t, sin_t, qw, kw, mq, mk)


def _attn_kernel(q_ref, k_ref, v_ref, o_ref):
    tq = q_ref.shape[0]
    lane = lax.broadcasted_iota(jnp.int32, (1, LANES), 1)
    low = lane < HALF_LANES
    q_per_kv_tiles = (N_Q_HEADS // N_KV_HEADS) * HEAD_DIM // LANES
    for kv in range(N_KV_HEADS):
        kt = k_ref[:, kv * LANES:(kv + 1) * LANES]
        vt = v_ref[:, kv * LANES:(kv + 1) * LANES]
        zero = jnp.zeros_like(vt)
        v_lo = jnp.where(low, vt, zero)
        v_hi = jnp.where(low, zero, vt)
        blk = (lane // (HEAD_DIM // 2)) % 2
        for jj in range(q_per_kv_tiles):
            j = kv * q_per_kv_tiles + jj
            cols = slice(j * LANES, (j + 1) * LANES)
            for r0 in range(0, tq, Q_UNIT):
                rows = slice(r0, r0 + Q_UNIT)
                qt = q_ref[rows, cols]
                zq = jnp.zeros_like(qt)
                out = None
                for a, v_a in ((0, v_lo), (1, v_hi)):
                    qa = jnp.where(blk == a, qt, zq)
                    s = lax.dot_general(qa, kt, (((1,), (1,)), ((), ())), preferred_element_type=F32)
                    m = jnp.max(s, axis=-1, keepdims=True)
                    p = jnp.exp2(s - m)
                    l = jnp.sum(p, axis=-1, keepdims=True)
                    o = jnp.dot(p.astype(BF16), v_a, preferred_element_type=F32) * (1.0 / l)
                    out = o if out is None else out + o
                o_ref[rows, cols] = out.astype(BF16)


def _attn_call(q, k, v, batch):
    T = q.shape[0]
    nq = SEQ // Q_TILE
    return pl.pallas_call(
        _attn_kernel, grid=(batch, nq),
        in_specs=[pl.BlockSpec((Q_TILE, ATT_WIDTH), lambda b, i: (b * nq + i, 0)),
                  pl.BlockSpec((SEQ, N_KV_HEADS * LANES), lambda b, i: (b, 0)),
                  pl.BlockSpec((SEQ, N_KV_HEADS * LANES), lambda b, i: (b, 0))],
        out_specs=pl.BlockSpec((Q_TILE, ATT_WIDTH), lambda b, i: (b * nq + i, 0)),
        out_shape=jax.ShapeDtypeStruct((T, ATT_WIDTH), BF16),
        name="attention",
        compiler_params=pltpu.CompilerParams(dimension_semantics=("arbitrary", "arbitrary"),
                                             vmem_limit_bytes=VMEM_LIMIT),
    )(q, k, v)


def _softplus(x):
    return jnp.maximum(x, 0.0) + jnp.log1p(jnp.exp(-jnp.abs(x)))


def _dot3(a_pieces, b, dims):
    out = None
    for piece in a_pieces:
        lhs, rhs = (piece, b) if dims == "piece_lhs" else (b, piece)
        contract = (((1,), (0,)), ((), ())) if dims == "piece_lhs" else (((1,), (1,)), ((), ()))
        term = lax.dot_general(lhs, rhs, contract, preferred_element_type=F32)
        out = term if out is None else out + term
    return out


def _conv_tile(xin, w_ref, b_ref, cols, pad_lo, pad_hi):
    R = xin.shape[0]
    half = CONV_WIDTH // 2
    t_idx = lax.broadcasted_iota(jnp.int32, (R, 1), 0)
    acc = b_ref[:, cols] + w_ref[half:half + 1, cols] * xin
    for j in range(CONV_WIDTH):
        off = j - half
        if off == 0:
            continue
        rolled = pltpu.roll(xin, (-off) % R, axis=0)
        if off < 0 and pad_lo:
            rolled = jnp.where(t_idx >= -off, rolled, 0.0)
        if off > 0 and pad_hi:
            rolled = jnp.where(t_idx < R - off, rolled, 0.0)
        acc = acc + w_ref[j:j + 1, cols] * rolled
    return acc * _sigmoid(acc)


def _ssd_kernel(xbc_ref, z_ref, dt_ref, cw_ref, cb_ref, dtb_ref, alog_ref, dsk_ref, nw_ref, tri_ref,
                o_ref, xc_ref, ylo_ref, yhi_ref, st_ref, bt_ref, dtt_ref, cumrow_ref, cumcol_ref, gm_ref):
    y_refs = (ylo_ref, yhi_ref)
    S = xbc_ref.shape[0]
    n_chunks = S // CHUNK
    pair_tiles = SSD_WIDTH // LANES
    tiles_per_group = pair_tiles // SSD_GROUPS
    n_heads_all = N_DIRS * SSD_HEADS
    b_off = SSD_WIDTH
    c_off = SSD_WIDTH + SSD_GROUPS * D_STATE
    lane = lax.broadcasted_iota(jnp.int32, (1, LANES), 1)
    low = lane < HALF_LANES

    edge = 2 * SUBLANES
    for c in range(CONV_CH // LANES):
        cols = slice(c * LANES, (c + 1) * LANES)
        xc_ref[:, cols] = _conv_tile(xbc_ref[:, cols], cw_ref, cb_ref, cols, False, False)
        xc_ref[0:edge // 2, cols] = _conv_tile(xbc_ref[0:edge, cols], cw_ref, cb_ref, cols, True, False)[0:edge // 2]
        xc_ref[S - edge // 2:S, cols] = _conv_tile(xbc_ref[S - edge:S, cols], cw_ref, cb_ref, cols,
                                                   False, True)[edge // 2:edge]

    for y_half in y_refs:
        y_half[...] = jnp.zeros_like(y_half)
    st_ref[...] = jnp.zeros_like(st_ref)

    a_col = -jnp.exp(alog_ref[...])
    dtb_col = dtb_ref[...]
    r_i = lax.broadcasted_iota(jnp.int32, (CHUNK, CHUNK), 0)
    c_i = lax.broadcasted_iota(jnp.int32, (CHUNK, CHUNK), 1)
    masks = (c_i <= r_i, c_i >= r_i)
    zero_rows = jnp.zeros((CHUNK - n_heads_all, CHUNK), BF16)
    fwd_rows = lax.broadcasted_iota(jnp.int32, (n_heads_all, 1), 0) < SSD_HEADS
    fwd_lanes = lane < SSD_HEADS

    def prep_chunk(c):
        rows = pl.ds(pl.multiple_of(c * CHUNK, CHUNK), CHUNK)
        dt_all = _softplus(dt_ref[rows, :].T[0:n_heads_all, :] + dtb_col)
        dtt_ref[c] = dt_all
        pieces = _split3(dt_all * a_col)
        padded = [jnp.concatenate([pc, zero_rows], axis=0) for pc in pieces]
        cumrow_ref[c] = jnp.where(fwd_rows, _dot3(pieces, tri_ref[1], "piece_lhs"),
                                  _dot3(pieces, tri_ref[0], "piece_lhs"))
        cumcol_ref[c] = jnp.where(fwd_lanes, _dot3(padded, tri_ref[0], "piece_rhs"),
                                  _dot3(padded, tri_ref[1], "piece_rhs"))
        for g in range(SSD_GROUPS):
            bt = xc_ref[rows, b_off + g * D_STATE:b_off + (g + 1) * D_STATE].T
            bt_ref[c, g * D_STATE:(g + 1) * D_STATE, :] = bt
            c_b = xc_ref[rows, c_off + g * D_STATE:c_off + (g + 1) * D_STATE].astype(BF16)
            gmat = jnp.dot(c_b, bt.astype(BF16), preferred_element_type=F32)
            for d in range(N_DIRS):
                k = d * SSD_GROUPS + g
                gm_ref[c, k * CHUNK:(k + 1) * CHUNK, :] = jnp.where(masks[d], gmat, 0.0).astype(BF16)

    prep_unroll = 4

    def prep(i, carry):
        for u in range(prep_unroll):
            prep_chunk(i * prep_unroll + u)
        return carry

    lax.fori_loop(0, n_chunks // prep_unroll, prep, 0)

    def one_chunk(c, d, y_half, c_local):
        rows = pl.ds(pl.multiple_of(c * CHUNK, CHUNK), CHUNK)
        rows_local = pl.ds(pl.multiple_of(c_local * CHUNK, CHUNK), CHUNK)
        hs = slice(d * SSD_HEADS, (d + 1) * SSD_HEADS)
        dt_d = dtt_ref[c][hs]
        cum_row = cumrow_ref[c][hs]
        cum_col = cumcol_ref[c]
        tot = jnp.sum(dt_d * a_col[hs], axis=1, keepdims=True)
        w_row = dt_d * jnp.exp(tot - cum_row)
        for g in range(SSD_GROUPS):
            c_b = xc_ref[rows, c_off + g * D_STATE:c_off + (g + 1) * D_STATE].astype(BF16)
            bt_g = bt_ref[c, g * D_STATE:(g + 1) * D_STATE, :]
            k = d * SSD_GROUPS + g
            gm_b = gm_ref[c, k * CHUNK:(k + 1) * CHUNK, :]
            for tt in range(tiles_per_group):
                t = g * tiles_per_group + tt
                cols = slice(t * LANES, (t + 1) * LANES)
                x_t = xc_ref[rows, cols].astype(BF16)
                st = st_ref[d, :, cols]
                rhs = jnp.concatenate([x_t, st.astype(BF16)], axis=0)
                ys, sn = [], []
                for hh in (2 * t, 2 * t + 1):
                    col = d * SSD_HEADS + hh
                    colb = jnp.broadcast_to(cum_col[:, col:col + 1], (CHUNK, CHUNK))
                    seg = jnp.minimum(colb - cum_row[hh:hh + 1, :], 0.0)
                    w = (jnp.exp(seg) * dt_d[hh:hh + 1, :]).astype(BF16) * gm_b
                    a = jnp.exp(colb).astype(BF16) * c_b
                    lhs = jnp.concatenate([w, a], axis=1)
                    ys.append(jnp.dot(lhs, rhs, preferred_element_type=F32))
                    bts = (bt_g * w_row[hh:hh + 1, :]).astype(BF16)
                    sn.append(jnp.dot(bts, x_t, preferred_element_type=F32))
                y_half[rows_local, cols] = y_half[rows_local, cols] + jnp.where(low, ys[0], ys[1])
                decay = jnp.exp(jnp.where(low, tot[2 * t:2 * t + 1, :], tot[2 * t + 1:2 * t + 2, :]))
                st_ref[d, :, cols] = st * decay + jnp.where(low, sn[0], sn[1])

    half_chunks = n_chunks // 2

    def body_first(c, carry):
        one_chunk(c, 0, y_refs[0], c)
        one_chunk(n_chunks - 1 - c, 1, y_refs[1], half_chunks - 1 - c)
        return carry

    def body_second(c, carry):
        one_chunk(c, 0, y_refs[1], c - half_chunks)
        one_chunk(n_chunks - 1 - c, 1, y_refs[0], n_chunks - 1 - c)
        return carry

    lax.fori_loop(0, half_chunks, body_first, 0)
    lax.fori_loop(half_chunks, n_chunks, body_second, 0)

    group_w = SSD_WIDTH // SSD_GROUPS
    for half, y_half in enumerate(y_refs):
        def fin(c, carry, half=half, y_half=y_half):
            rows = pl.ds(pl.multiple_of((half * half_chunks + c) * CHUNK, CHUNK), CHUNK)
            rows_local = pl.ds(pl.multiple_of(c * CHUNK, CHUNK), CHUNK)
            zz = z_ref[rows, :]
            y = (y_half[rows_local, :] + dsk_ref[...] * xc_ref[rows, 0:SSD_WIDTH]) * (zz * _sigmoid(zz))
            outs = []
            for g in range(SSD_GROUPS):
                yg = y[:, g * group_w:(g + 1) * group_w]
                ms = jnp.mean(yg * yg, axis=-1, keepdims=True)
                outs.append(yg * lax.rsqrt(ms + RMS_EPS))
            o_ref[rows, :] = (jnp.concatenate(outs, axis=-1) * nw_ref[...]).astype(BF16)
            return carry

        lax.fori_loop(0, half_chunks, fin, 0)


def _ssd_call(xbc, z, dt, cw, cb, dtb, alog, dsk, nw, tri, batch):
    T = xbc.shape[0]
    n_chunks = SEQ // CHUNK
    seq = lambda b: (b, 0)
    return pl.pallas_call(
        _ssd_kernel, grid=(batch,),
        in_specs=[pl.BlockSpec((SEQ, CONV_CH), seq), pl.BlockSpec((SEQ, SSD_WIDTH), seq),
                  pl.BlockSpec((SEQ, DT_PAD), seq),
                  _const_spec(cw.shape), _const_spec(cb.shape), _const_spec(dtb.shape),
                  _const_spec(alog.shape), _const_spec(dsk.shape), _const_spec(nw.shape),
                  _const_spec(tri.shape)],
        out_specs=pl.BlockSpec((SEQ, SSD_WIDTH), seq),
        out_shape=jax.ShapeDtypeStruct((T, SSD_WIDTH), BF16),
        scratch_shapes=[pltpu.VMEM((SEQ, CONV_CH), F32),
                        pltpu.VMEM((SEQ // 2, SSD_WIDTH), F32),
                        pltpu.VMEM((SEQ // 2, SSD_WIDTH), F32),
                        pltpu.VMEM((N_DIRS, D_STATE, SSD_WIDTH), F32),
                        pltpu.VMEM((n_chunks, SSD_GROUPS * D_STATE, CHUNK), F32),
                        pltpu.VMEM((n_chunks, N_DIRS * SSD_HEADS, CHUNK), F32),
                        pltpu.VMEM((n_chunks, N_DIRS * SSD_HEADS, CHUNK), F32),
                        pltpu.VMEM((n_chunks, CHUNK, LANES), F32),
                        pltpu.VMEM((n_chunks, N_DIRS * SSD_GROUPS * CHUNK, CHUNK), BF16)],
        name="ssd_bidir",
        compiler_params=pltpu.CompilerParams(dimension_semantics=("arbitrary",),
                                             vmem_limit_bytes=VMEM_LIMIT),
    )(xbc, z, dt, cw, cb, dtb, alog, dsk, nw, tri)


def _back_kernel(x1_ref, att_ref, ssd_ref, p_ref, wout_ref, wgu_ref, wo_ref,
                 lng_ref, lnb_ref, wp_ref, wgate_ref, bgate_ref, o_ref, pre_ref):
    i = pl.program_id(0)
    last = pl.num_programs(0) - 1
    subs = [slice(r0, r0 + NORM_SUB) for r0 in range(0, x1_ref.shape[0], NORM_SUB)]

    def embed(rows):
        x3 = _layer_norm(pre_ref[rows, :], lng_ref[1:2, :], lnb_ref[1:2, :])
        e = jnp.dot(p_ref[rows, :].astype(BF16), wp_ref[...], preferred_element_type=F32)
        gate = _sigmoid(jnp.dot(x3.astype(BF16), wgate_ref[...], preferred_element_type=F32) + bgate_ref[...])
        o_ref[rows, :] = _layer_norm(ALPHA * x3 + gate * e, lng_ref[2:3, :], lnb_ref[2:3, :])

    late_steps = [functools.partial(embed, rows) for rows in subs]

    def ffn_stage():
        x2_parts = []
        for rows in subs:
            mix = (jnp.dot(att_ref[rows, :], wout_ref[0:ATT_WIDTH, :], preferred_element_type=F32)
                   + jnp.dot(ssd_ref[rows, :], wout_ref[ATT_WIDTH:, :], preferred_element_type=F32))
            x2_parts.append(_layer_norm(ALPHA * x1_ref[rows, :] + mix, lng_ref[0:1, :], lnb_ref[0:1, :]))
        x2 = jnp.concatenate(x2_parts, axis=0)
        f = _swiglu(x2.astype(BF16), wgu_ref, wo_ref, before=late_steps)
        pre_ref[...] = ALPHA * x2 + 0.5 * f

    @pl.when(i == 0)
    def _():
        pre_ref[...] = jnp.zeros_like(pre_ref)

    @pl.when(i < last)
    def _():
        ffn_stage()

    @pl.when(i == last)
    def _():
        for step in late_steps:
            step()


def _back_call(x1, att, ssd, p2d, wout, wgu, wo, lng, lnb, wp, wgate, bgate):
    T = x1.shape[0]
    tm = TOKEN_TILE
    n_tiles = T // tm
    cur = lambda i: (jnp.minimum(i, n_tiles - 1), 0)
    late = lambda i: (jnp.maximum(i - 1, 0), 0)
    consts = (wout, wgu, wo, lng, lnb, wp, wgate, bgate)
    return pl.pallas_call(
        _back_kernel, grid=(n_tiles + 1,),
        in_specs=[pl.BlockSpec((tm, D_MODEL), cur), pl.BlockSpec((tm, ATT_WIDTH), cur),
                  pl.BlockSpec((tm, SSD_WIDTH), cur), pl.BlockSpec((tm, PLE_DIM), late)]
                 + [_const_spec(c.shape) for c in consts],
        out_specs=pl.BlockSpec((tm, D_MODEL), late),
        out_shape=jax.ShapeDtypeStruct((T, D_MODEL), F32),
        scratch_shapes=[pltpu.VMEM((tm, D_MODEL), F32)],
        name="back_outproj_ffn_ple",
        compiler_params=pltpu.CompilerParams(dimension_semantics=("arbitrary",),
                                             vmem_limit_bytes=VMEM_LIMIT),
    )(x1, att, ssd, p2d, *consts)


def _in_proj_weight(w):
    d = w.shape[0]
    half_blk = HEAD_DIM // 2
    n_q = N_Q_HEADS * HEAD_DIM
    n_kv = N_KV_HEADS * HEAD_DIM
    wq = w[:, :n_q].reshape(d, N_Q_HEADS // 2, 2, half_blk, 2).transpose(0, 1, 4, 2, 3).reshape(d, n_q)
    wk = w[:, n_q:n_q + n_kv].reshape(d, N_KV_HEADS, half_blk, 2).transpose(0, 1, 3, 2)
    wk = jnp.broadcast_to(wk[:, :, :, None, :], (d, N_KV_HEADS, 2, 2, half_blk)).reshape(d, N_KV_HEADS * LANES)
    wv = w[:, n_q + n_kv:n_q + 2 * n_kv].reshape(d, N_KV_HEADS, 1, HEAD_DIM)
    wv = jnp.broadcast_to(wv, (d, N_KV_HEADS, 2, HEAD_DIM)).reshape(d, N_KV_HEADS * LANES)
    rest = w[:, n_q + 2 * n_kv:]
    pad = jnp.zeros((d, DT_PAD - N_DIRS * SSD_HEADS), w.dtype)
    return jnp.concatenate([wq, wk, wv, rest, pad], axis=1).astype(BF16)


def _rope_tiles():
    rows = SEQ // GRID_W
    row = jnp.repeat(jnp.arange(rows, dtype=F32), GRID_W)
    col = jnp.tile(jnp.arange(GRID_W, dtype=F32), rows)
    inv = ROPE_THETA ** (-jnp.arange(0, ROPE_AXIS_DIM, 2, dtype=F32) / ROPE_AXIS_DIM)
    ang = jnp.concatenate([row[:, None] * inv, col[:, None] * inv], axis=-1)
    cos, sin = jnp.cos(ang), jnp.sin(ang)
    cos_t = jnp.concatenate([cos, cos, cos, cos], axis=-1)
    sin_t = jnp.concatenate([-sin, -sin, sin, sin], axis=-1)
    return cos_t, sin_t


def _norm_tile(w):
    lanes = np.arange(LANES)
    idx = 2 * (lanes % (HEAD_DIM // 2)) + lanes // HALF_LANES
    return w[idx][None, :]


def kernel(x, p, ln_g, ln_b, ffn1_w_in, ffn1_w_out, w_in, q_norm, k_norm, conv_w, conv_b, dt_bias,
           a_log, d_skip, ssd_norm, w_out, ffn2_w_in, ffn2_w_out, ple_w, ple_gate_w, ple_gate_b):
    B, S, D = x.shape
    assert (S, D) == (SEQ, D_MODEL) and ln_g.shape[0] == DEPTH == 1
    T = B * S
    x2d = x.reshape(T, D)
    p2d = p[0].reshape(T, PLE_DIM)

    cos_t, sin_t = _rope_tiles()
    lanes = np.arange(LANES)
    same_head = ((lanes[:, None] // (HEAD_DIM // 2)) % 2) == ((lanes[None, :] // (HEAD_DIM // 2)) % 2)
    two_tiles = np.eye(2, dtype=np.float32)
    mq = jnp.asarray(np.kron(two_tiles, np.where(same_head, 1.0 / HEAD_DIM, 0.0)), BF16)
    mk = jnp.asarray(np.kron(two_tiles, np.full((LANES, LANES), 1.0 / LANES)), BF16)
    ones = np.ones((CHUNK, CHUNK), np.float32)
    tri = jnp.asarray(np.stack([np.tril(ones), np.triu(ones)]), BF16)

    i = 0
    wcat = _in_proj_weight(w_in[i])
    qw = _norm_tile(q_norm[i]) * (HEAD_DIM ** -0.5 * LOG2E)
    kw = _norm_tile(k_norm[i])
    x1, qh, kh, vh, z, xbc, dtr = _front_call(
        x2d, ffn1_w_in[i].astype(BF16), ffn1_w_out[i].astype(BF16), ln_g[i, 0:1], ln_b[i, 0:1],
        wcat, cos_t, sin_t, qw, kw, mq, mk)

    att = _attn_call(qh, kh, vh, B)

    dsk = jnp.repeat(d_skip[i], SSD_HEAD_DIM)[None, :]
    ssd = _ssd_call(xbc, z, dtr, conv_w[i], conv_b[i][None, :], dt_bias[i].reshape(-1, 1), a_log[i].reshape(-1, 1),
                    dsk, ssd_norm[i][None, :], tri, B)

    out = _back_call(x1, att, ssd, p2d, w_out[i].astype(BF16),
                     ffn2_w_in[i].astype(BF16), ffn2_w_out[i].astype(BF16), ln_g[i, 1:4], ln_b[i, 1:4],
                     ple_w[i].astype(BF16), ple_gate_w[i].astype(BF16), ple_gate_b[i][None, :])
    return out.reshape(B, S, D)
```

```python
import functools
import math

import numpy as np
import jax
import jax.numpy as jnp
from jax import lax
from jax.experimental import pallas as pl
from jax.experimental.pallas import tpu as pltpu

F32 = jnp.float32
BF16 = jnp.bfloat16

D_MODEL = 1024
SEQ = 2048
DEPTH = 1
HEAD_DIM = 64
N_Q_HEADS = 8
N_KV_HEADS = 2
ROPE_AXIS_DIM = HEAD_DIM // 2
ROPE_THETA = 10000.0
GRID_W = 64
ATT_WIDTH = N_Q_HEADS * HEAD_DIM
SSD_WIDTH = 512
SSD_HEAD_DIM = 64
SSD_HEADS = 8
SSD_GROUPS = 2
D_STATE = 128
CONV_WIDTH = 5
CONV_CH = SSD_WIDTH + 2 * SSD_GROUPS * D_STATE
CHUNK = 128
N_DIRS = 2
D_FF = 2816
PLE_DIM = 256
ALPHA = (2.0 * DEPTH) ** 0.25
LN_EPS = 1e-5
RMS_EPS = 1e-6
LOG2E = 1.4426950408889634

LANES = 128
HALF_LANES = LANES // 2
SUBLANES = 8
TOKEN_TILE = 512
NORM_SUB = 256
Q_TILE = 1024
Q_UNIT = 512
FF_CHUNKS = ((0, 1024), (1024, 1024), (2048, 768))
DT_PAD = LANES
Q_OFF = 0
K_OFF = Q_OFF + ATT_WIDTH
V_OFF = K_OFF + N_KV_HEADS * LANES
Z_OFF = V_OFF + N_KV_HEADS * LANES
XBC_OFF = Z_OFF + SSD_WIDTH
DT_OFF = XBC_OFF + CONV_CH
IN_COLS = DT_OFF + DT_PAD
VMEM_LIMIT = 56 * 1024 * 1024


def _const_spec(shape):
    nd = len(shape)
    return pl.BlockSpec(shape, lambda *_: (0,) * nd, pipeline_mode=pl.Buffered(1))


def _layer_norm(y, g, b):
    mu = jnp.mean(y, axis=-1, keepdims=True)
    yc = y - mu
    var = jnp.mean(yc * yc, axis=-1, keepdims=True)
    return yc * lax.rsqrt(var + LN_EPS) * g + b


def _sigmoid(x):
    return 1.0 / (1.0 + jnp.exp(-x))


def _swiglu(xb, wgu_ref, wo_ref, before=()):
    acc = None
    for k, (c0, cw) in enumerate(FF_CHUNKS):
        if k < len(before):
            before[k]()
        hg = jnp.dot(xb, wgu_ref[:, c0:c0 + cw], preferred_element_type=F32)
        hu = jnp.dot(xb, wgu_ref[:, D_FF + c0:D_FF + c0 + cw], preferred_element_type=F32)
        act = (hg * _sigmoid(hg) * hu).astype(BF16)
        part = jnp.dot(act, wo_ref[c0:c0 + cw, :], preferred_element_type=F32)
        acc = part if acc is None else acc + part
    return acc


def _split3(x):
    hi = x.astype(BF16)
    r1 = x - hi.astype(F32)
    mid = r1.astype(BF16)
    lo = (r1 - mid.astype(F32)).astype(BF16)
    return hi, mid, lo


def _norm_rope(t2, seg_mean, w, cos, sin):
    ms = jnp.dot((t2 * t2).astype(BF16), seg_mean, preferred_element_type=F32)
    tn2 = t2 * lax.rsqrt(ms + RMS_EPS)
    outs = []
    for j in range(2):
        tn = tn2[:, j * LANES:(j + 1) * LANES] * w
        outs.append(tn * cos + pltpu.roll(tn, HALF_LANES, axis=1) * sin)
    return outs


def _front_kernel(x_ref, wgu_ref, wo_ref, g_ref, b_ref, win_ref,
                  cos_ref, sin_ref, qw_ref, kw_ref, mq_ref, mk_ref,
                  x1_ref, q_ref, k_ref, v_ref, z_ref, xbc_ref, dt_ref, pre_ref):
    i = pl.program_id(0)
    last = pl.num_programs(0) - 1

    def project(r0):
        rows = slice(r0, r0 + NORM_SUB)
        pair = 2 * LANES
        x1 = _layer_norm(pre_ref[rows, :], g_ref[...], b_ref[...])
        x1_ref[rows, :] = x1
        u = jnp.dot(x1.astype(BF16), win_ref[...], preferred_element_type=F32)
        cos = cos_ref[rows, :]
        sin = sin_ref[rows, :]
        for jp in range(ATT_WIDTH // pair):
            tiles = _norm_rope(u[:, Q_OFF + jp * pair:Q_OFF + (jp + 1) * pair], mq_ref[...], qw_ref[...], cos, sin)
            for j, tile in enumerate(tiles):
                c0 = jp * pair + j * LANES
                q_ref[rows, c0:c0 + LANES] = tile.astype(BF16)
        tiles = _norm_rope(u[:, K_OFF:K_OFF + pair], mk_ref[...], kw_ref[...], cos, sin)
        for j, tile in enumerate(tiles):
            k_ref[rows, j * LANES:(j + 1) * LANES] = tile.astype(BF16)
        v_ref[rows, :] = u[:, V_OFF:Z_OFF].astype(BF16)
        z_ref[rows, :] = u[:, Z_OFF:XBC_OFF]
        xbc_ref[rows, :] = u[:, XBC_OFF:DT_OFF]
        dt_ref[rows, :] = u[:, DT_OFF:IN_COLS]

    late_steps = [functools.partial(project, r0) for r0 in range(0, pre_ref.shape[0], NORM_SUB)]

    @pl.when(i == 0)
    def _():
        pre_ref[...] = jnp.zeros_like(pre_ref)

    @pl.when(i < last)
    def _():
        x = x_ref[...]
        f = _swiglu(x.astype(BF16), wgu_ref, wo_ref, before=late_steps)
        pre_ref[...] = ALPHA * x + 0.5 * f

    @pl.when(i == last)
    def _():
        for step in late_steps:
            step()


def _front_call(x2d, wgu, wo, g, b, wcat, cos_t, sin_t, qw, kw, mq, mk):
    T = x2d.shape[0]
    tm = TOKEN_TILE
    n_tiles = T // tm
    pos_blocks = SEQ // tm
    cur = lambda i: (jnp.minimum(i, n_tiles - 1), 0)
    late = lambda i: (jnp.maximum(i - 1, 0), 0)
    pos = lambda i: (jnp.maximum(i - 1, 0) % pos_blocks, 0)
    out_shape = (
        jax.ShapeDtypeStruct((T, D_MODEL), F32),
        jax.ShapeDtypeStruct((T, ATT_WIDTH), BF16),
        jax.ShapeDtypeStruct((T, N_KV_HEADS * LANES), BF16),
        jax.ShapeDtypeStruct((T, N_KV_HEADS * LANES), BF16),
        jax.ShapeDtypeStruct((T, SSD_WIDTH), F32),
        jax.ShapeDtypeStruct((T, CONV_CH), F32),
        jax.ShapeDtypeStruct((T, DT_PAD), F32),
    )
    in_specs = [
        pl.BlockSpec((tm, D_MODEL), cur),
        _const_spec(wgu.shape), _const_spec(wo.shape),
        _const_spec(g.shape), _const_spec(b.shape), _const_spec(wcat.shape),
        pl.BlockSpec((tm, LANES), pos), pl.BlockSpec((tm, LANES), pos),
        _const_spec(qw.shape), _const_spec(kw.shape), _const_spec(mq.shape), _const_spec(mk.shape),
    ]
    out_specs = tuple(pl.BlockSpec((tm, s.shape[1]), late) for s in out_shape)
    return pl.pallas_call(
        _front_kernel, grid=(n_tiles + 1,), in_specs=in_specs, out_specs=out_specs, out_shape=out_shape,
        scratch_shapes=[pltpu.VMEM((tm, D_MODEL), F32)],
        name="front_ffn_inproj",
        compiler_params=pltpu.CompilerParams(dimension_semantics=("arbitrary",),
                                             vmem_limit_bytes=VMEM_LIMIT),
    )(x2d, wgu, wo, g, b, wcat, cos_t, sin_t, qw, kw, mq, mk)


def _attn_kernel(q_ref, k_ref, v_ref, o_ref):
    tq = q_ref.shape[0]
    lane = lax.broadcasted_iota(jnp.int32, (1, LANES), 1)
    low = lane < HALF_LANES
    q_per_kv_tiles = (N_Q_HEADS // N_KV_HEADS) * HEAD_DIM // LANES
    for kv in range(N_KV_HEADS):
        kt = k_ref[:, kv * LANES:(kv + 1) * LANES]
        vt = v_ref[:, kv * LANES:(kv + 1) * LANES]
        zero = jnp.zeros_like(vt)
        v_lo = jnp.where(low, vt, zero)
        v_hi = jnp.where(low, zero, vt)
        blk = (lane // (HEAD_DIM // 2)) % 2
        for jj in range(q_per_kv_tiles):
            j = kv * q_per_kv_tiles + jj
            cols = slice(j * LANES, (j + 1) * LANES)
            for r0 in range(0, tq, Q_UNIT):
                rows = slice(r0, r0 + Q_UNIT)
                qt = q_ref[rows, cols]
                zq = jnp.zeros_like(qt)
                out = None
                for a, v_a in ((0, v_lo), (1, v_hi)):
                    qa = jnp.where(blk == a, qt, zq)
                    s = lax.dot_general(qa, kt, (((1,), (1,)), ((), ())), preferred_element_type=F32)
                    m = jnp.max(s, axis=-1, keepdims=True)
                    p = jnp.exp2(s - m)
                    l = jnp.sum(p, axis=-1, keepdims=True)
                    o = jnp.dot(p.astype(BF16), v_a, preferred_element_type=F32) * (1.0 / l)
                    out = o if out is None else out + o
                o_ref[rows, cols] = out.astype(BF16)


def _attn_call(q, k, v, batch):
    T = q.shape[0]
    nq = SEQ // Q_TILE
    return pl.pallas_call(
        _attn_kernel, grid=(batch, nq),
        in_specs=[pl.BlockSpec((Q_TILE, ATT_WIDTH), lambda b, i: (b * nq + i, 0)),
                  pl.BlockSpec((SEQ, N_KV_HEADS * LANES), lambda b, i: (b, 0)),
                  pl.BlockSpec((SEQ, N_KV_HEADS * LANES), lambda b, i: (b, 0))],
        out_specs=pl.BlockSpec((Q_TILE, ATT_WIDTH), lambda b, i: (b * nq + i, 0)),
        out_shape=jax.ShapeDtypeStruct((T, ATT_WIDTH), BF16),
        name="attention",
        compiler_params=pltpu.CompilerParams(dimension_semantics=("arbitrary", "arbitrary"),
                                             vmem_limit_bytes=VMEM_LIMIT),
    )(q, k, v)


def _softplus(x):
    return jnp.maximum(x, 0.0) + jnp.log1p(jnp.exp(-jnp.abs(x)))


def _dot3(a_pieces, b, dims):
    out = None
    for piece in a_pieces:
        lhs, rhs = (piece, b) if dims == "piece_lhs" else (b, piece)
        contract = (((1,), (0,)), ((), ())) if dims == "piece_lhs" else (((1,), (1,)), ((), ()))
        term = lax.dot_general(lhs, rhs, contract, preferred_element_type=F32)
        out = term if out is None else out + term
    return out


def _conv_tile(xin, w_ref, b_ref, cols, pad_lo, pad_hi):
    R = xin.shape[0]
    half = CONV_WIDTH // 2
    t_idx = lax.broadcasted_iota(jnp.int32, (R, 1), 0)
    acc = b_ref[:, cols] + w_ref[half:half + 1, cols] * xin
    for j in range(CONV_WIDTH):
        off = j - half
        if off == 0:
            continue
        rolled = pltpu.roll(xin, (-off) % R, axis=0)
        if off < 0 and pad_lo:
            rolled = jnp.where(t_idx >= -off, rolled, 0.0)
        if off > 0 and pad_hi:
            rolled = jnp.where(t_idx < R - off, rolled, 0.0)
        acc = acc + w_ref[j:j + 1, cols] * rolled
    return acc * _sigmoid(acc)


def _ssd_kernel(xbc_ref, z_ref, dt_ref, cw_ref, cb_ref, dtb_ref, alog_ref, dsk_ref, nw_ref, tri_ref,
                o_ref, xc_ref, ylo_ref, yhi_ref, st_ref, bt_ref, dtt_ref, cumrow_ref, cumcol_ref, gm_ref):
    y_refs = (ylo_ref, yhi_ref)
    S = xbc_ref.shape[0]
    n_chunks = S // CHUNK
    pair_tiles = SSD_WIDTH // LANES
    tiles_per_group = pair_tiles // SSD_GROUPS
    n_heads_all = N_DIRS * SSD_HEADS
    b_off = SSD_WIDTH
    c_off = SSD_WIDTH + SSD_GROUPS * D_STATE
    lane = lax.broadcasted_iota(jnp.int32, (1, LANES), 1)
    low = lane < HALF_LANES

    edge = 2 * SUBLANES
    for c in range(CONV_CH // LANES):
        cols = slice(c * LANES, (c + 1) * LANES)
        xc_ref[:, cols] = _conv_tile(xbc_ref[:, cols], cw_ref, cb_ref, cols, False, False)
        xc_ref[0:edge // 2, cols] = _conv_tile(xbc_ref[0:edge, cols], cw_ref, cb_ref, cols, True, False)[0:edge // 2]
        xc_ref[S - edge // 2:S, cols] = _conv_tile(xbc_ref[S - edge:S, cols], cw_ref, cb_ref, cols,
                                                   False, True)[edge // 2:edge]

    for y_half in y_refs:
        y_half[...] = jnp.zeros_like(y_half)
    st_ref[...] = jnp.zeros_like(st_ref)

    a_col = -jnp.exp(alog_ref[...])
    dtb_col = dtb_ref[...]
    r_i = lax.broadcasted_iota(jnp.int32, (CHUNK, CHUNK), 0)
    c_i = lax.broadcasted_iota(jnp.int32, (CHUNK, CHUNK), 1)
    masks = (c_i <= r_i, c_i >= r_i)
    zero_rows = jnp.zeros((CHUNK - n_heads_all, CHUNK), BF16)
    fwd_rows = lax.broadcasted_iota(jnp.int32, (n_heads_all, 1), 0) < SSD_HEADS
    fwd_lanes = lane < SSD_HEADS

    def prep_chunk(c):
        rows = pl.ds(pl.multiple_of(c * CHUNK, CHUNK), CHUNK)
        dt_all = _softplus(dt_ref[rows, :].T[0:n_heads_all, :] + dtb_col)
        dtt_ref[c] = dt_all
        pieces = _split3(dt_all * a_col)
        padded = [jnp.concatenate([pc, zero_rows], axis=0) for pc in pieces]
        cumrow_ref[c] = jnp.where(fwd_rows, _dot3(pieces, tri_ref[1], "piece_lhs"),
                                  _dot3(pieces, tri_ref[0], "piece_lhs"))
        cumcol_ref[c] = jnp.where(fwd_lanes, _dot3(padded, tri_ref[0], "piece_rhs"),
                                  _dot3(padded, tri_ref[1], "piece_rhs"))
        for g in range(SSD_GROUPS):
            bt = xc_ref[rows, b_off + g * D_STATE:b_off + (g + 1) * D_STATE].T
            bt_ref[c, g * D_STATE:(g + 1) * D_STATE, :] = bt
            c_b = xc_ref[rows, c_off + g * D_STATE:c_off + (g + 1) * D_STATE].astype(BF16)
            gmat = jnp.dot(c_b, bt.astype(BF16), preferred_element_type=F32)
            for d in range(N_DIRS):
                k = d * SSD_GROUPS + g
                gm_ref[c, k * CHUNK:(k + 1) * CHUNK, :] = jnp.where(masks[d], gmat, 0.0).astype(BF16)

    prep_unroll = 4

    def prep(i, carry):
        for u in range(prep_unroll):
            prep_chunk(i * prep_unroll + u)
        return carry

    lax.fori_loop(0, n_chunks // prep_unroll, prep, 0)

    def one_chunk(c, d, y_half, c_local):
        rows = pl.ds(pl.multiple_of(c * CHUNK, CHUNK), CHUNK)
        rows_local = pl.ds(pl.multiple_of(c_local * CHUNK, CHUNK), CHUNK)
        hs = slice(d * SSD_HEADS, (d + 1) * SSD_HEADS)
        dt_d = dtt_ref[c][hs]
        cum_row = cumrow_ref[c][hs]
        cum_col = cumcol_ref[c]
        tot = jnp.sum(dt_d * a_col[hs], axis=1, keepdims=True)
        w_row = dt_d * jnp.exp(tot - cum_row)
        for g in range(SSD_GROUPS):
            c_b = xc_ref[rows, c_off + g * D_STATE:c_off + (g + 1) * D_STATE].astype(BF16)
            bt_g = bt_ref[c, g * D_STATE:(g + 1) * D_STATE, :]
            k = d * SSD_GROUPS + g
            gm_b = gm_ref[c, k * CHUNK:(k + 1) * CHUNK, :]
            for tt in range(tiles_per_group):
                t = g * tiles_per_group + tt
                cols = slice(t * LANES, (t + 1) * LANES)
                x_t = xc_ref[rows, cols].astype(BF16)
                st = st_ref[d, :, cols]
                rhs = jnp.concatenate([x_t, st.astype(BF16)], axis=0)
                zero_x = jnp.zeros_like(x_t)
                x_by_head = jnp.concatenate([jnp.where(low, x_t, zero_x), jnp.where(low, zero_x, x_t)], axis=0)
                ys, bts = [], []
                for hh in (2 * t, 2 * t + 1):
                    col = d * SSD_HEADS + hh
                    colb = jnp.broadcast_to(cum_col[:, col:col + 1], (CHUNK, CHUNK))
                    seg = jnp.minimum(colb - cum_row[hh:hh + 1, :], 0.0)
                    w = (jnp.exp(seg) * dt_d[hh:hh + 1, :]).astype(BF16) * gm_b
                    a = jnp.exp(colb).astype(BF16) * c_b
                    lhs = jnp.concatenate([w, a], axis=1)
                    ys.append(jnp.dot(lhs, rhs, preferred_element_type=F32))
                    bts.append((bt_g * w_row[hh:hh + 1, :]).astype(BF16))
                y_half[rows_local, cols] = y_half[rows_local, cols] + jnp.where(low, ys[0], ys[1])
                decay = jnp.exp(jnp.where(low, tot[2 * t:2 * t + 1, :], tot[2 * t + 1:2 * t + 2, :]))
                st_ref[d, :, cols] = st * decay + jnp.dot(jnp.concatenate(bts, axis=1), x_by_head,
                                                          preferred_element_type=F32)

    half_chunks = n_chunks // 2

    def body_first(c, carry):
        one_chunk(c, 0, y_refs[0], c)
        one_chunk(n_chunks - 1 - c, 1, y_refs[1], half_chunks - 1 - c)
        return carry

    def body_second(c, carry):
        one_chunk(c, 0, y_refs[1], c - half_chunks)
        one_chunk(n_chunks - 1 - c, 1, y_refs[0], n_chunks - 1 - c)
        return carry

    lax.fori_loop(0, half_chunks, body_first, 0)
    lax.fori_loop(half_chunks, n_chunks, body_second, 0)

    group_w = SSD_WIDTH // SSD_GROUPS
    for half, y_half in enumerate(y_refs):
        def fin(c, carry, half=half, y_half=y_half):
            rows = pl.ds(pl.multiple_of((half * half_chunks + c) * CHUNK, CHUNK), CHUNK)
            rows_local = pl.ds(pl.multiple_of(c * CHUNK, CHUNK), CHUNK)
            zz = z_ref[rows, :]
            y = (y_half[rows_local, :] + dsk_ref[...] * xc_ref[rows, 0:SSD_WIDTH]) * (zz * _sigmoid(zz))
            outs = []
            for g in range(SSD_GROUPS):
                yg = y[:, g * group_w:(g + 1) * group_w]
                ms = jnp.mean(yg * yg, axis=-1, keepdims=True)
                outs.append(yg * lax.rsqrt(ms + RMS_EPS))
            o_ref[rows, :] = (jnp.concatenate(outs, axis=-1) * nw_ref[...]).astype(BF16)
            return carry

        lax.fori_loop(0, half_chunks, fin, 0)


def _ssd_call(xbc, z, dt, cw, cb, dtb, alog, dsk, nw, tri, batch):
    T = xbc.shape[0]
    n_chunks = SEQ // CHUNK
    seq = lambda b: (b, 0)
    return pl.pallas_call(
        _ssd_kernel, grid=(batch,),
        in_specs=[pl.BlockSpec((SEQ, CONV_CH), seq), pl.BlockSpec((SEQ, SSD_WIDTH), seq),
                  pl.BlockSpec((SEQ, DT_PAD), seq),
                  _const_spec(cw.shape), _const_spec(cb.shape), _const_spec(dtb.shape),
                  _const_spec(alog.shape), _const_spec(dsk.shape), _const_spec(nw.shape),
                  _const_spec(tri.shape)],
        out_specs=pl.BlockSpec((SEQ, SSD_WIDTH), seq),
        out_shape=jax.ShapeDtypeStruct((T, SSD_WIDTH), BF16),
        scratch_shapes=[pltpu.VMEM((SEQ, CONV_CH), F32),
                        pltpu.VMEM((SEQ // 2, SSD_WIDTH), F32),
                        pltpu.VMEM((SEQ // 2, SSD_WIDTH), F32),
                        pltpu.VMEM((N_DIRS, D_STATE, SSD_WIDTH), F32),
                        pltpu.VMEM((n_chunks, SSD_GROUPS * D_STATE, CHUNK), F32),
                        pltpu.VMEM((n_chunks, N_DIRS * SSD_HEADS, CHUNK), F32),
                        pltpu.VMEM((n_chunks, N_DIRS * SSD_HEADS, CHUNK), F32),
                        pltpu.VMEM((n_chunks, CHUNK, LANES), F32),
                        pltpu.VMEM((n_chunks, N_DIRS * SSD_GROUPS * CHUNK, CHUNK), BF16)],
        name="ssd_bidir",
        compiler_params=pltpu.CompilerParams(dimension_semantics=("arbitrary",),
                                             vmem_limit_bytes=VMEM_LIMIT),
    )(xbc, z, dt, cw, cb, dtb, alog, dsk, nw, tri)


def _back_kernel(x1_ref, att_ref, ssd_ref, p_ref, wout_ref, wgu_ref, wo_ref,
                 lng_ref, lnb_ref, wp_ref, wgate_ref, bgate_ref, o_ref, pre_ref):
    i = pl.program_id(0)
    last = pl.num_programs(0) - 1
    subs = [slice(r0, r0 + NORM_SUB) for r0 in range(0, x1_ref.shape[0], NORM_SUB)]

    def embed(rows):
        x3 = _layer_norm(pre_ref[rows, :], lng_ref[1:2, :], lnb_ref[1:2, :])
        e = jnp.dot(p_ref[rows, :].astype(BF16), wp_ref[...], preferred_element_type=F32)
        gate = _sigmoid(jnp.dot(x3.astype(BF16), wgate_ref[...], preferred_element_type=F32) + bgate_ref[...])
        o_ref[rows, :] = _layer_norm(ALPHA * x3 + gate * e, lng_ref[2:3, :], lnb_ref[2:3, :])

    late_steps = [functools.partial(embed, rows) for rows in subs]

    def ffn_stage():
        x2_parts = []
        for rows in subs:
            mix = (jnp.dot(att_ref[rows, :], wout_ref[0:ATT_WIDTH, :], preferred_element_type=F32)
                   + jnp.dot(ssd_ref[rows, :], wout_ref[ATT_WIDTH:, :], preferred_element_type=F32))
            x2_parts.append(_layer_norm(ALPHA * x1_ref[rows, :] + mix, lng_ref[0:1, :], lnb_ref[0:1, :]))
        x2 = jnp.concatenate(x2_parts, axis=0)
        f = _swiglu(x2.astype(BF16), wgu_ref, wo_ref, before=late_steps)
        pre_ref[...] = ALPHA * x2 + 0.5 * f

    @pl.when(i == 0)
    def _():
        pre_ref[...] = jnp.zeros_like(pre_ref)

    @pl.when(i < last)
    def _():
        ffn_stage()

    @pl.when(i == last)
    def _():
        for step in late_steps:
            step()


def _back_call(x1, att, ssd, p2d, wout, wgu, wo, lng, lnb, wp, wgate, bgate):
    T = x1.shape[0]
    tm = TOKEN_TILE
    n_tiles = T // tm
    cur = lambda i: (jnp.minimum(i, n_tiles - 1), 0)
    late = lambda i: (jnp.maximum(i - 1, 0), 0)
    consts = (wout, wgu, wo, lng, lnb, wp, wgate, bgate)
    return pl.pallas_call(
        _back_kernel, grid=(n_tiles + 1,),
        in_specs=[pl.BlockSpec((tm, D_MODEL), cur), pl.BlockSpec((tm, ATT_WIDTH), cur),
                  pl.BlockSpec((tm, SSD_WIDTH), cur), pl.BlockSpec((tm, PLE_DIM), late)]
                 + [_const_spec(c.shape) for c in consts],
        out_specs=pl.BlockSpec((tm, D_MODEL), late),
        out_shape=jax.ShapeDtypeStruct((T, D_MODEL), F32),
        scratch_shapes=[pltpu.VMEM((tm, D_MODEL), F32)],
        name="back_outproj_ffn_ple",
        compiler_params=pltpu.CompilerParams(dimension_semantics=("arbitrary",),
                                             vmem_limit_bytes=VMEM_LIMIT),
    )(x1, att, ssd, p2d, *consts)


def _in_proj_weight(w):
    d = w.shape[0]
    half_blk = HEAD_DIM // 2
    n_q = N_Q_HEADS * HEAD_DIM
    n_kv = N_KV_HEADS * HEAD_DIM
    wq = w[:, :n_q].reshape(d, N_Q_HEADS // 2, 2, half_blk, 2).transpose(0, 1, 4, 2, 3).reshape(d, n_q)
    wk = w[:, n_q:n_q + n_kv].reshape(d, N_KV_HEADS, half_blk, 2).transpose(0, 1, 3, 2)
    wk = jnp.broadcast_to(wk[:, :, :, None, :], (d, N_KV_HEADS, 2, 2, half_blk)).reshape(d, N_KV_HEADS * LANES)
    wv = w[:, n_q + n_kv:n_q + 2 * n_kv].reshape(d, N_KV_HEADS, 1, HEAD_DIM)
    wv = jnp.broadcast_to(wv, (d, N_KV_HEADS, 2, HEAD_DIM)).reshape(d, N_KV_HEADS * LANES)
    rest = w[:, n_q + 2 * n_kv:]
    pad = jnp.zeros((d, DT_PAD - N_DIRS * SSD_HEADS), w.dtype)
    return jnp.concatenate([wq, wk, wv, rest, pad], axis=1).astype(BF16)


def _rope_tiles():
    rows = SEQ // GRID_W
    row = jnp.repeat(jnp.arange(rows, dtype=F32), GRID_W)
    col = jnp.tile(jnp.arange(GRID_W, dtype=F32), rows)
    inv = ROPE_THETA ** (-jnp.arange(0, ROPE_AXIS_DIM, 2, dtype=F32) / ROPE_AXIS_DIM)
    ang = jnp.concatenate([row[:, None] * inv, col[:, None] * inv], axis=-1)
    cos, sin = jnp.cos(ang), jnp.sin(ang)
    cos_t = jnp.concatenate([cos, cos, cos, cos], axis=-1)
    sin_t = jnp.concatenate([-sin, -sin, sin, sin], axis=-1)
    return cos_t, sin_t


def _norm_tile(w):
    lanes = np.arange(LANES)
    idx = 2 * (lanes % (HEAD_DIM // 2)) + lanes // HALF_LANES
    return w[idx][None, :]


def kernel(x, p, ln_g, ln_b, ffn1_w_in, ffn1_w_out, w_in, q_norm, k_norm, conv_w, conv_b, dt_bias,
           a_log, d_skip, ssd_norm, w_out, ffn2_w_in, ffn2_w_out, ple_w, ple_gate_w, ple_gate_b):
    B, S, D = x.shape
    assert (S, D) == (SEQ, D_MODEL) and ln_g.shape[0] == DEPTH == 1
    T = B * S
    x2d = x.reshape(T, D)
    p2d = p[0].reshape(T, PLE_DIM)

    cos_t, sin_t = _rope_tiles()
    lanes = np.arange(LANES)
    same_head = ((lanes[:, None] // (HEAD_DIM // 2)) % 2) == ((lanes[None, :] // (HEAD_DIM // 2)) % 2)
    two_tiles = np.eye(2, dtype=np.float32)
    mq = jnp.asarray(np.kron(two_tiles, np.where(same_head, 1.0 / HEAD_DIM, 0.0)), BF16)
    mk = jnp.asarray(np.kron(two_tiles, np.full((LANES, LANES), 1.0 / LANES)), BF16)
    ones = np.ones((CHUNK, CHUNK), np.float32)
    tri = jnp.asarray(np.stack([np.tril(ones), np.triu(ones)]), BF16)

    i = 0
    wcat = _in_proj_weight(w_in[i])
    qw = _norm_tile(q_norm[i]) * (HEAD_DIM ** -0.5 * LOG2E)
    kw = _norm_tile(k_norm[i])
    x1, qh, kh, vh, z, xbc, dtr = _front_call(
        x2d, ffn1_w_in[i].astype(BF16), ffn1_w_out[i].astype(BF16), ln_g[i, 0:1], ln_b[i, 0:1],
        wcat, cos_t, sin_t, qw, kw, mq, mk)

    att = _attn_call(qh, kh, vh, B)

    dsk = jnp.repeat(d_skip[i], SSD_HEAD_DIM)[None, :]
    ssd = _ssd_call(xbc, z, dtr, conv_w[i], conv_b[i][None, :], dt_bias[i].reshape(-1, 1), a_log[i].reshape(-1, 1),
                    dsk, ssd_norm[i][None, :], tri, B)

    out = _back_call(x1, att, ssd, p2d, w_out[i].astype(BF16),
                     ffn2_w_in[i].astype(BF16), ffn2_w_out[i].astype(BF16), ln_g[i, 1:4], ln_b[i, 1:4],
                     ple_w[i].astype(BF16), ple_gate_w[i].astype(BF16), ple_gate_b[i][None, :])
    return out.reshape(B, S, D)
```

```python
import functools
import math

import numpy as np
import jax
import jax.numpy as jnp
from jax import lax
from jax.experimental import pallas as pl
from jax.experimental.pallas import tpu as pltpu

F32 = jnp.float32
BF16 = jnp.bfloat16

D_MODEL = 1024
SEQ = 2048
DEPTH = 1
HEAD_DIM = 64
N_Q_HEADS = 8
N_KV_HEADS = 2
ROPE_AXIS_DIM = HEAD_DIM // 2
ROPE_THETA = 10000.0
GRID_W = 64
ATT_WIDTH = N_Q_HEADS * HEAD_DIM
SSD_WIDTH = 512
SSD_HEAD_DIM = 64
SSD_HEADS = 8
SSD_GROUPS = 2
D_STATE = 128
CONV_WIDTH = 5
CONV_CH = SSD_WIDTH + 2 * SSD_GROUPS * D_STATE
CHUNK = 128
N_DIRS = 2
D_FF = 2816
PLE_DIM = 256
ALPHA = (2.0 * DEPTH) ** 0.25
LN_EPS = 1e-5
RMS_EPS = 1e-6
LOG2E = 1.4426950408889634

LANES = 128
HALF_LANES = LANES // 2
SUBLANES = 8
TOKEN_TILE = 512
NORM_SUB = 256
Q_TILE = 1024
Q_UNIT = 512
FF_CHUNKS = ((0, 512), (512, 512), (1024, 512), (1536, 512), (2048, 768))
DT_PAD = LANES
Q_OFF = 0
K_OFF = Q_OFF + ATT_WIDTH
V_OFF = K_OFF + N_KV_HEADS * LANES
Z_OFF = V_OFF + N_KV_HEADS * LANES
XBC_OFF = Z_OFF + SSD_WIDTH
DT_OFF = XBC_OFF + CONV_CH
IN_COLS = DT_OFF + DT_PAD
VMEM_LIMIT = 56 * 1024 * 1024


def _const_spec(shape):
    nd = len(shape)
    return pl.BlockSpec(shape, lambda *_: (0,) * nd, pipeline_mode=pl.Buffered(1))


def _layer_norm(y, g, b):
    mu = jnp.mean(y, axis=-1, keepdims=True)
    yc = y - mu
    var = jnp.mean(yc * yc, axis=-1, keepdims=True)
    return yc * lax.rsqrt(var + LN_EPS) * g + b


def _sigmoid(x):
    return 1.0 / (1.0 + jnp.exp(-x))


def _swiglu(xb, wgu_ref, wo_ref, before=()):
    acc = None
    for k, (c0, cw) in enumerate(FF_CHUNKS):
        if k < len(before):
            before[k]()
        hg = jnp.dot(xb, wgu_ref[:, c0:c0 + cw], preferred_element_type=F32)
        hu = jnp.dot(xb, wgu_ref[:, D_FF + c0:D_FF + c0 + cw], preferred_element_type=F32)
        act = (hg * _sigmoid(hg) * hu).astype(BF16)
        part = jnp.dot(act, wo_ref[c0:c0 + cw, :], preferred_element_type=F32)
        acc = part if acc is None else acc + part
    return acc


def _split3(x):
    hi = x.astype(BF16)
    r1 = x - hi.astype(F32)
    mid = r1.astype(BF16)
    lo = (r1 - mid.astype(F32)).astype(BF16)
    return hi, mid, lo


def _norm_rope(t2, seg_mean, w, cos, sin):
    ms = jnp.dot((t2 * t2).astype(BF16), seg_mean, preferred_element_type=F32)
    tn2 = t2 * lax.rsqrt(ms + RMS_EPS)
    outs = []
    for j in range(2):
        tn = tn2[:, j * LANES:(j + 1) * LANES] * w
        outs.append(tn * cos + pltpu.roll(tn, HALF_LANES, axis=1) * sin)
    return outs


def _front_kernel(x_ref, wgu_ref, wo_ref, g_ref, b_ref, win_ref,
                  cos_ref, sin_ref, qw_ref, kw_ref, mq_ref, mk_ref,
                  x1_ref, q_ref, k_ref, v_ref, z_ref, xbc_ref, dt_ref, pre_ref):
    i = pl.program_id(0)
    last = pl.num_programs(0) - 1

    def project(r0):
        rows = slice(r0, r0 + NORM_SUB)
        pair = 2 * LANES
        x1 = _layer_norm(pre_ref[rows, :], g_ref[...], b_ref[...])
        x1_ref[rows, :] = x1
        u = jnp.dot(x1.astype(BF16), win_ref[...], preferred_element_type=F32)
        cos = cos_ref[rows, :]
        sin = sin_ref[rows, :]
        for jp in range(ATT_WIDTH // pair):
            tiles = _norm_rope(u[:, Q_OFF + jp * pair:Q_OFF + (jp + 1) * pair], mq_ref[...], qw_ref[...], cos, sin)
            for j, tile in enumerate(tiles):
                c0 = jp * pair + j * LANES
                q_ref[rows, c0:c0 + LANES] = tile.astype(BF16)
        tiles = _norm_rope(u[:, K_OFF:K_OFF + pair], mk_ref[...], kw_ref[...], cos, sin)
        for j, tile in enumerate(tiles):
            k_ref[rows, j * LANES:(j + 1) * LANES] = tile.astype(BF16)
        v_ref[rows, :] = u[:, V_OFF:Z_OFF].astype(BF16)
        z_ref[rows, :] = u[:, Z_OFF:XBC_OFF]
        xbc_ref[rows, :] = u[:, XBC_OFF:DT_OFF]
        dt_ref[rows, :] = u[:, DT_OFF:IN_COLS]

    late_steps = [functools.partial(project, r0) for r0 in range(0, pre_ref.shape[0], NORM_SUB)]

    @pl.when(i == 0)
    def _():
        pre_ref[...] = jnp.zeros_like(pre_ref)

    @pl.when(i < last)
    def _():
        x = x_ref[...]
        f = _swiglu(x.astype(BF16), wgu_ref, wo_ref, before=late_steps)
        pre_ref[...] = ALPHA * x + 0.5 * f

    @pl.when(i == last)
    def _():
        for step in late_steps:
            step()


def _front_call(x2d, wgu, wo, g, b, wcat, cos_t, sin_t, qw, kw, mq, mk):
    T = x2d.shape[0]
    tm = TOKEN_TILE
    n_tiles = T // tm
    pos_blocks = SEQ // tm
    cur = lambda i: (jnp.minimum(i, n_tiles - 1), 0)
    late = lambda i: (jnp.maximum(i - 1, 0), 0)
    pos = lambda i: (jnp.maximum(i - 1, 0) % pos_blocks, 0)
    out_shape = (
        jax.ShapeDtypeStruct((T, D_MODEL), F32),
        jax.ShapeDtypeStruct((T, ATT_WIDTH), BF16),
        jax.ShapeDtypeStruct((T, N_KV_HEADS * LANES), BF16),
        jax.ShapeDtypeStruct((T, N_KV_HEADS * LANES), BF16),
        jax.ShapeDtypeStruct((T, SSD_WIDTH), F32),
        jax.ShapeDtypeStruct((T, CONV_CH), F32),
        jax.ShapeDtypeStruct((T, DT_PAD), F32),
    )
    in_specs = [
        pl.BlockSpec((tm, D_MODEL), cur),
        _const_spec(wgu.shape), _const_spec(wo.shape),
        _const_spec(g.shape), _const_spec(b.shape), _const_spec(wcat.shape),
        pl.BlockSpec((tm, LANES), pos), pl.BlockSpec((tm, LANES), pos),
        _const_spec(qw.shape), _const_spec(kw.shape), _const_spec(mq.shape), _const_spec(mk.shape),
    ]
    out_specs = tuple(pl.BlockSpec((tm, s.shape[1]), late) for s in out_shape)
    return pl.pallas_call(
        _front_kernel, grid=(n_tiles + 1,), in_specs=in_specs, out_specs=out_specs, out_shape=out_shape,
        scratch_shapes=[pltpu.VMEM((tm, D_MODEL), F32)],
        name="front_ffn_inproj",
        compiler_params=pltpu.CompilerParams(dimension_semantics=("arbitrary",),
                                             vmem_limit_bytes=VMEM_LIMIT),
    )(x2d, wgu, wo, g, b, wcat, cos_t, sin_t, qw, kw, mq, mk)


def _attn_kernel(q_ref, k_ref, v_ref, o_ref):
    tq = q_ref.shape[0]
    lane = lax.broadcasted_iota(jnp.int32, (1, LANES), 1)
    low = lane < HALF_LANES
    q_per_kv_tiles = (N_Q_HEADS // N_KV_HEADS) * HEAD_DIM // LANES
    for kv in range(N_KV_HEADS):
        kt = k_ref[:, kv * LANES:(kv + 1) * LANES]
        vt = v_ref[:, kv * LANES:(kv + 1) * LANES]
        zero = jnp.zeros_like(vt)
        v_lo = jnp.where(low, vt, zero)
        v_hi = jnp.where(low, zero, vt)
        blk = (lane // (HEAD_DIM // 2)) % 2
        for jj in range(q_per_kv_tiles):
            j = kv * q_per_kv_tiles + jj
            cols = slice(j * LANES, (j + 1) * LANES)
            for r0 in range(0, tq, Q_UNIT):
                rows = slice(r0, r0 + Q_UNIT)
                qt = q_ref[rows, cols]
                zq = jnp.zeros_like(qt)
                out = None
                for a, v_a in ((0, v_lo), (1, v_hi)):
                    qa = jnp.where(blk == a, qt, zq)
                    s = lax.dot_general(qa, kt, (((1,), (1,)), ((), ())), preferred_element_type=F32)
                    m = jnp.max(s, axis=-1, keepdims=True)
                    p = jnp.exp2(s - m)
                    l = jnp.sum(p, axis=-1, keepdims=True)
                    o = jnp.dot(p.astype(BF16), v_a, preferred_element_type=F32) * (1.0 / l)
                    out = o if out is None else out + o
                o_ref[rows, cols] = out.astype(BF16)


def _attn_call(q, k, v, batch):
    T = q.shape[0]
    nq = SEQ // Q_TILE
    return pl.pallas_call(
        _attn_kernel, grid=(batch, nq),
        in_specs=[pl.BlockSpec((Q_TILE, ATT_WIDTH), lambda b, i: (b * nq + i, 0)),
                  pl.BlockSpec((SEQ, N_KV_HEADS * LANES), lambda b, i: (b, 0)),
                  pl.BlockSpec((SEQ, N_KV_HEADS * LANES), lambda b, i: (b, 0))],
        out_specs=pl.BlockSpec((Q_TILE, ATT_WIDTH), lambda b, i: (b * nq + i, 0)),
        out_shape=jax.ShapeDtypeStruct((T, ATT_WIDTH), BF16),
        name="attention",
        compiler_params=pltpu.CompilerParams(dimension_semantics=("arbitrary", "arbitrary"),
                                             vmem_limit_bytes=VMEM_LIMIT),
    )(q, k, v)


def _softplus(x):
    return jnp.maximum(x, 0.0) + jnp.log1p(jnp.exp(-jnp.abs(x)))


def _dot3(a_pieces, b, dims):
    out = None
    for piece in a_pieces:
        lhs, rhs = (piece, b) if dims == "piece_lhs" else (b, piece)
        contract = (((1,), (0,)), ((), ())) if dims == "piece_lhs" else (((1,), (1,)), ((), ()))
        term = lax.dot_general(lhs, rhs, contract, preferred_element_type=F32)
        out = term if out is None else out + term
    return out


def _conv_tile(xin, w_ref, b_ref, cols, pad_lo, pad_hi):
    R = xin.shape[0]
    half = CONV_WIDTH // 2
    t_idx = lax.broadcasted_iota(jnp.int32, (R, 1), 0)
    acc = b_ref[:, cols] + w_ref[half:half + 1, cols] * xin
    for j in range(CONV_WIDTH):
        off = j - half
        if off == 0:
            continue
        rolled = pltpu.roll(xin, (-off) % R, axis=0)
        if off < 0 and pad_lo:
            rolled = jnp.where(t_idx >= -off, rolled, 0.0)
        if off > 0 and pad_hi:
            rolled = jnp.where(t_idx < R - off, rolled, 0.0)
        acc = acc + w_ref[j:j + 1, cols] * rolled
    return acc * _sigmoid(acc)


def _ssd_kernel(xbc_ref, z_ref, dt_ref, cw_ref, cb_ref, dtb_ref, alog_ref, dsk_ref, nw_ref, tri_ref,
                o_ref, xc_ref, ylo_ref, yhi_ref, st_ref, bt_ref, dtt_ref, cumrow_ref, cumcol_ref, gm_ref):
    y_refs = (ylo_ref, yhi_ref)
    S = xbc_ref.shape[0]
    n_chunks = S // CHUNK
    pair_tiles = SSD_WIDTH // LANES
    tiles_per_group = pair_tiles // SSD_GROUPS
    n_heads_all = N_DIRS * SSD_HEADS
    b_off = SSD_WIDTH
    c_off = SSD_WIDTH + SSD_GROUPS * D_STATE
    lane = lax.broadcasted_iota(jnp.int32, (1, LANES), 1)
    low = lane < HALF_LANES

    edge = 2 * SUBLANES
    for c in range(CONV_CH // LANES):
        cols = slice(c * LANES, (c + 1) * LANES)
        xc_ref[:, cols] = _conv_tile(xbc_ref[:, cols], cw_ref, cb_ref, cols, False, False)
        xc_ref[0:edge // 2, cols] = _conv_tile(xbc_ref[0:edge, cols], cw_ref, cb_ref, cols, True, False)[0:edge // 2]
        xc_ref[S - edge // 2:S, cols] = _conv_tile(xbc_ref[S - edge:S, cols], cw_ref, cb_ref, cols,
                                                   False, True)[edge // 2:edge]

    for y_half in y_refs:
        y_half[...] = jnp.zeros_like(y_half)
    st_ref[...] = jnp.zeros_like(st_ref)

    a_col = -jnp.exp(alog_ref[...])
    dtb_col = dtb_ref[...]
    r_i = lax.broadcasted_iota(jnp.int32, (CHUNK, CHUNK), 0)
    c_i = lax.broadcasted_iota(jnp.int32, (CHUNK, CHUNK), 1)
    masks = (c_i <= r_i, c_i >= r_i)
    zero_rows = jnp.zeros((CHUNK - n_heads_all, CHUNK), BF16)
    fwd_rows = lax.broadcasted_iota(jnp.int32, (n_heads_all, 1), 0) < SSD_HEADS
    fwd_lanes = lane < SSD_HEADS

    def prep_chunk(c):
        rows = pl.ds(pl.multiple_of(c * CHUNK, CHUNK), CHUNK)
        dt_all = _softplus(dt_ref[rows, :].T[0:n_heads_all, :] + dtb_col)
        dtt_ref[c] = dt_all
        pieces = _split3(dt_all * a_col)
        padded = [jnp.concatenate([pc, zero_rows], axis=0) for pc in pieces]
        cumrow_ref[c] = jnp.where(fwd_rows, _dot3(pieces, tri_ref[1], "piece_lhs"),
                                  _dot3(pieces, tri_ref[0], "piece_lhs"))
        cumcol_ref[c] = jnp.where(fwd_lanes, _dot3(padded, tri_ref[0], "piece_rhs"),
                                  _dot3(padded, tri_ref[1], "piece_rhs"))
        for g in range(SSD_GROUPS):
            bt = xc_ref[rows, b_off + g * D_STATE:b_off + (g + 1) * D_STATE].T
            bt_ref[c, g * D_STATE:(g + 1) * D_STATE, :] = bt
            c_b = xc_ref[rows, c_off + g * D_STATE:c_off + (g + 1) * D_STATE].astype(BF16)
            gmat = jnp.dot(c_b, bt.astype(BF16), preferred_element_type=F32)
            for d in range(N_DIRS):
                k = d * SSD_GROUPS + g
                gm_ref[c, k * CHUNK:(k + 1) * CHUNK, :] = jnp.where(masks[d], gmat, 0.0).astype(BF16)

    prep_unroll = 4

    def prep(i, carry):
        for u in range(prep_unroll):
            prep_chunk(i * prep_unroll + u)
        return carry

    lax.fori_loop(0, n_chunks // prep_unroll, prep, 0)

    def one_chunk(c, d, y_half, c_local):
        rows = pl.ds(pl.multiple_of(c * CHUNK, CHUNK), CHUNK)
        rows_local = pl.ds(pl.multiple_of(c_local * CHUNK, CHUNK), CHUNK)
        hs = slice(d * SSD_HEADS, (d + 1) * SSD_HEADS)
        dt_d = dtt_ref[c][hs]
        cum_row = cumrow_ref[c][hs]
        cum_col = cumcol_ref[c]
        tot = jnp.sum(dt_d * a_col[hs], axis=1, keepdims=True)
        w_row = dt_d * jnp.exp(tot - cum_row)
        for g in range(SSD_GROUPS):
            c_b = xc_ref[rows, c_off + g * D_STATE:c_off + (g + 1) * D_STATE].astype(BF16)
            bt_g = bt_ref[c, g * D_STATE:(g + 1) * D_STATE, :]
            k = d * SSD_GROUPS + g
            gm_b = gm_ref[c, k * CHUNK:(k + 1) * CHUNK, :]
            for tt in range(tiles_per_group):
                t = g * tiles_per_group + tt
                cols = slice(t * LANES, (t + 1) * LANES)
                x_t = xc_ref[rows, cols].astype(BF16)
                st = st_ref[d, :, cols]
                rhs = jnp.concatenate([x_t, st.astype(BF16)], axis=0)
                zero_x = jnp.zeros_like(x_t)
                x_by_head = jnp.concatenate([jnp.where(low, x_t, zero_x), jnp.where(low, zero_x, x_t)], axis=0)
                ys, bts = [], []
                for hh in (2 * t, 2 * t + 1):
                    col = d * SSD_HEADS + hh
                    colb = jnp.broadcast_to(cum_col[:, col:col + 1], (CHUNK, CHUNK))
                    seg = jnp.minimum(colb - cum_row[hh:hh + 1, :], 0.0)
                    w = (jnp.exp(seg) * dt_d[hh:hh + 1, :]).astype(BF16) * gm_b
                    a = jnp.exp(colb).astype(BF16) * c_b
                    lhs = jnp.concatenate([w, a], axis=1)
                    ys.append(jnp.dot(lhs, rhs, preferred_element_type=F32))
                    bts.append((bt_g * w_row[hh:hh + 1, :]).astype(BF16))
                y_half[rows_local, cols] = y_half[rows_local, cols] + jnp.where(low, ys[0], ys[1])
                decay = jnp.exp(jnp.where(low, tot[2 * t:2 * t + 1, :], tot[2 * t + 1:2 * t + 2, :]))
                st_ref[d, :, cols] = st * decay + jnp.dot(jnp.concatenate(bts, axis=1), x_by_head,
                                                          preferred_element_type=F32)

    half_chunks = n_chunks // 2

    def body_first(c, carry):
        one_chunk(c, 0, y_refs[0], c)
        one_chunk(n_chunks - 1 - c, 1, y_refs[1], half_chunks - 1 - c)
        return carry

    def body_second(c, carry):
        one_chunk(c, 0, y_refs[1], c - half_chunks)
        one_chunk(n_chunks - 1 - c, 1, y_refs[0], n_chunks - 1 - c)
        return carry

    lax.fori_loop(0, half_chunks, body_first, 0)
    lax.fori_loop(half_chunks, n_chunks, body_second, 0)

    group_w = SSD_WIDTH // SSD_GROUPS
    for half, y_half in enumerate(y_refs):
        def fin(c, carry, half=half, y_half=y_half):
            rows = pl.ds(pl.multiple_of((half * half_chunks + c) * CHUNK, CHUNK), CHUNK)
            rows_local = pl.ds(pl.multiple_of(c * CHUNK, CHUNK), CHUNK)
            zz = z_ref[rows, :]
            y = (y_half[rows_local, :] + dsk_ref[...] * xc_ref[rows, 0:SSD_WIDTH]) * (zz * _sigmoid(zz))
            outs = []
            for g in range(SSD_GROUPS):
                yg = y[:, g * group_w:(g + 1) * group_w]
                ms = jnp.mean(yg * yg, axis=-1, keepdims=True)
                outs.append(yg * lax.rsqrt(ms + RMS_EPS))
            o_ref[rows, :] = (jnp.concatenate(outs, axis=-1) * nw_ref[...]).astype(BF16)
            return carry

        lax.fori_loop(0, half_chunks, fin, 0)


def _ssd_call(xbc, z, dt, cw, cb, dtb, alog, dsk, nw, tri, batch):
    T = xbc.shape[0]
    n_chunks = SEQ // CHUNK
    seq = lambda b: (b, 0)
    return pl.pallas_call(
        _ssd_kernel, grid=(batch,),
        in_specs=[pl.BlockSpec((SEQ, CONV_CH), seq), pl.BlockSpec((SEQ, SSD_WIDTH), seq),
                  pl.BlockSpec((SEQ, DT_PAD), seq),
                  _const_spec(cw.shape), _const_spec(cb.shape), _const_spec(dtb.shape),
                  _const_spec(alog.shape), _const_spec(dsk.shape), _const_spec(nw.shape),
                  _const_spec(tri.shape)],
        out_specs=pl.BlockSpec((SEQ, SSD_WIDTH), seq),
        out_shape=jax.ShapeDtypeStruct((T, SSD_WIDTH), BF16),
        scratch_shapes=[pltpu.VMEM((SEQ, CONV_CH), F32),
                        pltpu.VMEM((SEQ // 2, SSD_WIDTH), F32),
                        pltpu.VMEM((SEQ // 2, SSD_WIDTH), F32),
                        pltpu.VMEM((N_DIRS, D_STATE, SSD_WIDTH), F32),
                        pltpu.VMEM((n_chunks, SSD_GROUPS * D_STATE, CHUNK), F32),
                        pltpu.VMEM((n_chunks, N_DIRS * SSD_HEADS, CHUNK), F32),
                        pltpu.VMEM((n_chunks, N_DIRS * SSD_HEADS, CHUNK), F32),
                        pltpu.VMEM((n_chunks, CHUNK, LANES), F32),
                        pltpu.VMEM((n_chunks, N_DIRS * SSD_GROUPS * CHUNK, CHUNK), BF16)],
        name="ssd_bidir",
        compiler_params=pltpu.CompilerParams(dimension_semantics=("arbitrary",),
                                             vmem_limit_bytes=VMEM_LIMIT),
    )(xbc, z, dt, cw, cb, dtb, alog, dsk, nw, tri)


def _back_kernel(x1_ref, att_ref, ssd_ref, p_ref, wout_ref, wgu_ref, wo_ref,
                 lng_ref, lnb_ref, wp_ref, wgate_ref, bgate_ref, o_ref, pre_ref):
    i = pl.program_id(0)
    last = pl.num_programs(0) - 1
    subs = [slice(r0, r0 + NORM_SUB) for r0 in range(0, x1_ref.shape[0], NORM_SUB)]

    def embed(rows):
        x3 = _layer_norm(pre_ref[rows, :], lng_ref[1:2, :], lnb_ref[1:2, :])
        e = jnp.dot(p_ref[rows, :].astype(BF16), wp_ref[...], preferred_element_type=F32)
        gate = _sigmoid(jnp.dot(x3.astype(BF16), wgate_ref[...], preferred_element_type=F32) + bgate_ref[...])
        o_ref[rows, :] = _layer_norm(ALPHA * x3 + gate * e, lng_ref[2:3, :], lnb_ref[2:3, :])

    late_steps = [functools.partial(embed, rows) for rows in subs]

    def ffn_stage():
        x2_parts = []
        for rows in subs:
            mix = (jnp.dot(att_ref[rows, :], wout_ref[0:ATT_WIDTH, :], preferred_element_type=F32)
                   + jnp.dot(ssd_ref[rows, :], wout_ref[ATT_WIDTH:, :], preferred_element_type=F32))
            x2_parts.append(_layer_norm(ALPHA * x1_ref[rows, :] + mix, lng_ref[0:1, :], lnb_ref[0:1, :]))
        x2 = jnp.concatenate(x2_parts, axis=0)
        f = _swiglu(x2.astype(BF16), wgu_ref, wo_ref, before=late_steps)
        pre_ref[...] = ALPHA * x2 + 0.5 * f

    @pl.when(i == 0)
    def _():
        pre_ref[...] = jnp.zeros_like(pre_ref)

    @pl.when(i < last)
    def _():
        ffn_stage()

    @pl.when(i == last)
    def _():
        for step in late_steps:
            step()


def _back_call(x1, att, ssd, p2d, wout, wgu, wo, lng, lnb, wp, wgate, bgate):
    T = x1.shape[0]
    tm = TOKEN_TILE
    n_tiles = T // tm
    cur = lambda i: (jnp.minimum(i, n_tiles - 1), 0)
    late = lambda i: (jnp.maximum(i - 1, 0), 0)
    consts = (wout, wgu, wo, lng, lnb, wp, wgate, bgate)
    return pl.pallas_call(
        _back_kernel, grid=(n_tiles + 1,),
        in_specs=[pl.BlockSpec((tm, D_MODEL), cur), pl.BlockSpec((tm, ATT_WIDTH), cur),
                  pl.BlockSpec((tm, SSD_WIDTH), cur), pl.BlockSpec((tm, PLE_DIM), late)]
                 + [_const_spec(c.shape) for c in consts],
        out_specs=pl.BlockSpec((tm, D_MODEL), late),
        out_shape=jax.ShapeDtypeStruct((T, D_MODEL), F32),
        scratch_shapes=[pltpu.VMEM((tm, D_MODEL), F32)],
        name="back_outproj_ffn_ple",
        compiler_params=pltpu.CompilerParams(dimension_semantics=("arbitrary",),
                                             vmem_limit_bytes=VMEM_LIMIT),
    )(x1, att, ssd, p2d, *consts)


def _in_proj_weight(w):
    d = w.shape[0]
    half_blk = HEAD_DIM // 2
    n_q = N_Q_HEADS * HEAD_DIM
    n_kv = N_KV_HEADS * HEAD_DIM
    wq = w[:, :n_q].reshape(d, N_Q_HEADS // 2, 2, half_blk, 2).transpose(0, 1, 4, 2, 3).reshape(d, n_q)
    wk = w[:, n_q:n_q + n_kv].reshape(d, N_KV_HEADS, half_blk, 2).transpose(0, 1, 3, 2)
    wk = jnp.broadcast_to(wk[:, :, :, None, :], (d, N_KV_HEADS, 2, 2, half_blk)).reshape(d, N_KV_HEADS * LANES)
    wv = w[:, n_q + n_kv:n_q + 2 * n_kv].reshape(d, N_KV_HEADS, 1, HEAD_DIM)
    wv = jnp.broadcast_to(wv, (d, N_KV_HEADS, 2, HEAD_DIM)).reshape(d, N_KV_HEADS * LANES)
    rest = w[:, n_q + 2 * n_kv:]
    pad = jnp.zeros((d, DT_PAD - N_DIRS * SSD_HEADS), w.dtype)
    return jnp.concatenate([wq, wk, wv, rest, pad], axis=1).astype(BF16)


def _rope_tiles():
    rows = SEQ // GRID_W
    row = jnp.repeat(jnp.arange(rows, dtype=F32), GRID_W)
    col = jnp.tile(jnp.arange(GRID_W, dtype=F32), rows)
    inv = ROPE_THETA ** (-jnp.arange(0, ROPE_AXIS_DIM, 2, dtype=F32) / ROPE_AXIS_DIM)
    ang = jnp.concatenate([row[:, None] * inv, col[:, None] * inv], axis=-1)
    cos, sin = jnp.cos(ang), jnp.sin(ang)
    cos_t = jnp.concatenate([cos, cos, cos, cos], axis=-1)
    sin_t = jnp.concatenate([-sin, -sin, sin, sin], axis=-1)
    return cos_t, sin_t


def _norm_tile(w):
    lanes = np.arange(LANES)
    idx = 2 * (lanes % (HEAD_DIM // 2)) + lanes // HALF_LANES
    return w[idx][None, :]


def kernel(x, p, ln_g, ln_b, ffn1_w_in, ffn1_w_out, w_in, q_norm, k_norm, conv_w, conv_b, dt_bias,
           a_log, d_skip, ssd_norm, w_out, ffn2_w_in, ffn2_w_out, ple_w, ple_gate_w, ple_gate_b):
    B, S, D = x.shape
    assert (S, D) == (SEQ, D_MODEL) and ln_g.shape[0] == DEPTH == 1
    T = B * S
    x2d = x.reshape(T, D)
    p2d = p[0].reshape(T, PLE_DIM)

    cos_t, sin_t = _rope_tiles()
    lanes = np.arange(LANES)
    same_head = ((lanes[:, None] // (HEAD_DIM // 2)) % 2) == ((lanes[None, :] // (HEAD_DIM // 2)) % 2)
    two_tiles = np.eye(2, dtype=np.float32)
    mq = jnp.asarray(np.kron(two_tiles, np.where(same_head, 1.0 / HEAD_DIM, 0.0)), BF16)
    mk = jnp.asarray(np.kron(two_tiles, np.full((LANES, LANES), 1.0 / LANES)), BF16)
    ones = np.ones((CHUNK, CHUNK), np.float32)
    tri = jnp.asarray(np.stack([np.tril(ones), np.triu(ones)]), BF16)

    i = 0
    wcat = _in_proj_weight(w_in[i])
    qw = _norm_tile(q_norm[i]) * (HEAD_DIM ** -0.5 * LOG2E)
    kw = _norm_tile(k_norm[i])
    x1, qh, kh, vh, z, xbc, dtr = _front_call(
        x2d, ffn1_w_in[i].astype(BF16), ffn1_w_out[i].astype(BF16), ln_g[i, 0:1], ln_b[i, 0:1],
        wcat, cos_t, sin_t, qw, kw, mq, mk)

    att = _attn_call(qh, kh, vh, B)

    dsk = jnp.repeat(d_skip[i], SSD_HEAD_DIM)[None, :]
    ssd = _ssd_call(xbc, z, dtr, conv_w[i], conv_b[i][None, :], dt_bias[i].reshape(-1, 1), a_log[i].reshape(-1, 1),
                    dsk, ssd_norm[i][None, :], tri, B)

    out = _back_call(x1, att, ssd, p2d, w_out[i].astype(BF16),
                     ffn2_w_in[i].astype(BF16), ffn2_w_out[i].astype(BF16), ln_g[i, 1:4], ln_b[i, 1:4],
                     ple_w[i].astype(BF16), ple_gate_w[i].astype(BF16), ple_gate_b[i][None, :])
    return out.reshape(B, S, D)
```

```python
import functools
import math

import numpy as np
import jax
import jax.numpy as jnp
from jax import lax
from jax.experimental import pallas as pl
from jax.experimental.pallas import tpu as pltpu

F32 = jnp.float32
BF16 = jnp.bfloat16

D_MODEL = 1024
SEQ = 2048
DEPTH = 1
HEAD_DIM = 64
N_Q_HEADS = 8
N_KV_HEADS = 2
ROPE_AXIS_DIM = HEAD_DIM // 2
ROPE_THETA = 10000.0
GRID_W = 64
ATT_WIDTH = N_Q_HEADS * HEAD_DIM
SSD_WIDTH = 512
SSD_HEAD_DIM = 64
SSD_HEADS = 8
SSD_GROUPS = 2
D_STATE = 128
CONV_WIDTH = 5
CONV_CH = SSD_WIDTH + 2 * SSD_GROUPS * D_STATE
CHUNK = 128
N_DIRS = 2
D_FF = 2816
PLE_DIM = 256
ALPHA = (2.0 * DEPTH) ** 0.25
LN_EPS = 1e-5
RMS_EPS = 1e-6
LOG2E = 1.4426950408889634

LANES = 128
HALF_LANES = LANES // 2
SUBLANES = 8
TOKEN_TILE = 512
NORM_SUB = 256
Q_TILE = 1024
Q_UNIT = 512
FF_CHUNKS_FRONT = ((0, 512), (512, 512), (1024, 512), (1536, 512), (2048, 768))
FF_CHUNKS_BACK = ((0, 768), (768, 768), (1536, 768), (2304, 512))
DT_PAD = LANES
Q_OFF = 0
K_OFF = Q_OFF + ATT_WIDTH
V_OFF = K_OFF + N_KV_HEADS * LANES
Z_OFF = V_OFF + N_KV_HEADS * LANES
XBC_OFF = Z_OFF + SSD_WIDTH
DT_OFF = XBC_OFF + CONV_CH
IN_COLS = DT_OFF + DT_PAD
VMEM_LIMIT = 56 * 1024 * 1024


def _const_spec(shape):
    nd = len(shape)
    return pl.BlockSpec(shape, lambda *_: (0,) * nd, pipeline_mode=pl.Buffered(1))


def _layer_norm(y, g, b):
    mu = jnp.mean(y, axis=-1, keepdims=True)
    yc = y - mu
    var = jnp.mean(yc * yc, axis=-1, keepdims=True)
    return yc * lax.rsqrt(var + LN_EPS) * g + b


def _sigmoid(x):
    return 1.0 / (1.0 + jnp.exp(-x))


def _swiglu(xb, wgu_ref, wo_ref, chunks, before=()):
    acc = None
    for k, (c0, cw) in enumerate(chunks):
        if k < len(before):
            before[k]()
        hg = jnp.dot(xb, wgu_ref[:, c0:c0 + cw], preferred_element_type=F32)
        hu = jnp.dot(xb, wgu_ref[:, D_FF + c0:D_FF + c0 + cw], preferred_element_type=F32)
        act = (hg * _sigmoid(hg) * hu).astype(BF16)
        part = jnp.dot(act, wo_ref[c0:c0 + cw, :], preferred_element_type=F32)
        acc = part if acc is None else acc + part
    return acc


def _split3(x):
    hi = x.astype(BF16)
    r1 = x - hi.astype(F32)
    mid = r1.astype(BF16)
    lo = (r1 - mid.astype(F32)).astype(BF16)
    return hi, mid, lo


def _norm_rope(t2, seg_mean, w, cos, sin):
    ms = jnp.dot((t2 * t2).astype(BF16), seg_mean, preferred_element_type=F32)
    tn2 = t2 * lax.rsqrt(ms + RMS_EPS)
    outs = []
    for j in range(2):
        tn = tn2[:, j * LANES:(j + 1) * LANES] * w
        outs.append(tn * cos + pltpu.roll(tn, HALF_LANES, axis=1) * sin)
    return outs


def _front_kernel(x_ref, wgu_ref, wo_ref, g_ref, b_ref, win_ref,
                  cos_ref, sin_ref, qw_ref, kw_ref, mq_ref, mk_ref,
                  x1_ref, q_ref, k_ref, v_ref, z_ref, xbc_ref, dt_ref, pre_ref):
    i = pl.program_id(0)
    last = pl.num_programs(0) - 1

    def project(r0):
        rows = slice(r0, r0 + NORM_SUB)
        pair = 2 * LANES
        x1 = _layer_norm(pre_ref[rows, :], g_ref[...], b_ref[...])
        x1_ref[rows, :] = x1
        u = jnp.dot(x1.astype(BF16), win_ref[...], preferred_element_type=F32)
        cos = cos_ref[rows, :]
        sin = sin_ref[rows, :]
        for jp in range(ATT_WIDTH // pair):
            tiles = _norm_rope(u[:, Q_OFF + jp * pair:Q_OFF + (jp + 1) * pair], mq_ref[...], qw_ref[...], cos, sin)
            for j, tile in enumerate(tiles):
                c0 = jp * pair + j * LANES
                q_ref[rows, c0:c0 + LANES] = tile.astype(BF16)
        tiles = _norm_rope(u[:, K_OFF:K_OFF + pair], mk_ref[...], kw_ref[...], cos, sin)
        for j, tile in enumerate(tiles):
            k_ref[rows, j * LANES:(j + 1) * LANES] = tile.astype(BF16)
        v_ref[rows, :] = u[:, V_OFF:Z_OFF].astype(BF16)
        z_ref[rows, :] = u[:, Z_OFF:XBC_OFF]
        xbc_ref[rows, :] = u[:, XBC_OFF:DT_OFF]
        dt_ref[rows, :] = u[:, DT_OFF:IN_COLS]

    late_steps = [functools.partial(project, r0) for r0 in range(0, pre_ref.shape[0], NORM_SUB)]

    @pl.when(i == 0)
    def _():
        pre_ref[...] = jnp.zeros_like(pre_ref)

    @pl.when(i < last)
    def _():
        x = x_ref[...]
        f = _swiglu(x.astype(BF16), wgu_ref, wo_ref, FF_CHUNKS_FRONT, before=late_steps)
        pre_ref[...] = ALPHA * x + 0.5 * f

    @pl.when(i == last)
    def _():
        for step in late_steps:
            step()


def _front_call(x2d, wgu, wo, g, b, wcat, cos_t, sin_t, qw, kw, mq, mk):
    T = x2d.shape[0]
    tm = TOKEN_TILE
    n_tiles = T // tm
    pos_blocks = SEQ // tm
    cur = lambda i: (jnp.minimum(i, n_tiles - 1), 0)
    late = lambda i: (jnp.maximum(i - 1, 0), 0)
    pos = lambda i: (jnp.maximum(i - 1, 0) % pos_blocks, 0)
    out_shape = (
        jax.ShapeDtypeStruct((T, D_MODEL), F32),
        jax.ShapeDtypeStruct((T, ATT_WIDTH), BF16),
        jax.ShapeDtypeStruct((T, N_KV_HEADS * LANES), BF16),
        jax.ShapeDtypeStruct((T, N_KV_HEADS * LANES), BF16),
        jax.ShapeDtypeStruct((T, SSD_WIDTH), F32),
        jax.ShapeDtypeStruct((T, CONV_CH), F32),
        jax.ShapeDtypeStruct((T, DT_PAD), F32),
    )
    in_specs = [
        pl.BlockSpec((tm, D_MODEL), cur),
        _const_spec(wgu.shape), _const_spec(wo.shape),
        _const_spec(g.shape), _const_spec(b.shape), _const_spec(wcat.shape),
        pl.BlockSpec((tm, LANES), pos), pl.BlockSpec((tm, LANES), pos),
        _const_spec(qw.shape), _const_spec(kw.shape), _const_spec(mq.shape), _const_spec(mk.shape),
    ]
    out_specs = tuple(pl.BlockSpec((tm, s.shape[1]), late) for s in out_shape)
    return pl.pallas_call(
        _front_kernel, grid=(n_tiles + 1,), in_specs=in_specs, out_specs=out_specs, out_shape=out_shape,
        scratch_shapes=[pltpu.VMEM((tm, D_MODEL), F32)],
        name="front_ffn_inproj",
        compiler_params=pltpu.CompilerParams(dimension_semantics=("arbitrary",),
                                             vmem_limit_bytes=VMEM_LIMIT),
    )(x2d, wgu, wo, g, b, wcat, cos_t, sin_t, qw, kw, mq, mk)


def _attn_kernel(q_ref, k_ref, v_ref, o_ref):
    tq = q_ref.shape[0]
    lane = lax.broadcasted_iota(jnp.int32, (1, LANES), 1)
    low = lane < HALF_LANES
    q_per_kv_tiles = (N_Q_HEADS // N_KV_HEADS) * HEAD_DIM // LANES
    for kv in range(N_KV_HEADS):
        kt = k_ref[:, kv * LANES:(kv + 1) * LANES]
        vt = v_ref[:, kv * LANES:(kv + 1) * LANES]
        zero = jnp.zeros_like(vt)
        v_lo = jnp.where(low, vt, zero)
        v_hi = jnp.where(low, zero, vt)
        blk = (lane // (HEAD_DIM // 2)) % 2
        for jj in range(q_per_kv_tiles):
            j = kv * q_per_kv_tiles + jj
            cols = slice(j * LANES, (j + 1) * LANES)
            for r0 in range(0, tq, Q_UNIT):
                rows = slice(r0, r0 + Q_UNIT)
                qt = q_ref[rows, cols]
                zq = jnp.zeros_like(qt)
                out = None
                for a, v_a in ((0, v_lo), (1, v_hi)):
                    qa = jnp.where(blk == a, qt, zq)
                    s = lax.dot_general(qa, kt, (((1,), (1,)), ((), ())), preferred_element_type=F32)
                    m = jnp.max(s, axis=-1, keepdims=True)
                    p = jnp.exp2(s - m)
                    l = jnp.sum(p, axis=-1, keepdims=True)
                    o = jnp.dot(p.astype(BF16), v_a, preferred_element_type=F32) * (1.0 / l)
                    out = o if out is None else out + o
                o_ref[rows, cols] = out.astype(BF16)


def _attn_call(q, k, v, batch):
    T = q.shape[0]
    nq = SEQ // Q_TILE
    return pl.pallas_call(
        _attn_kernel, grid=(batch, nq),
        in_specs=[pl.BlockSpec((Q_TILE, ATT_WIDTH), lambda b, i: (b * nq + i, 0)),
                  pl.BlockSpec((SEQ, N_KV_HEADS * LANES), lambda b, i: (b, 0)),
                  pl.BlockSpec((SEQ, N_KV_HEADS * LANES), lambda b, i: (b, 0))],
        out_specs=pl.BlockSpec((Q_TILE, ATT_WIDTH), lambda b, i: (b * nq + i, 0)),
        out_shape=jax.ShapeDtypeStruct((T, ATT_WIDTH), BF16),
        name="attention",
        compiler_params=pltpu.CompilerParams(dimension_semantics=("arbitrary", "arbitrary"),
                                             vmem_limit_bytes=VMEM_LIMIT),
    )(q, k, v)


def _softplus(x):
    return jnp.maximum(x, 0.0) + jnp.log1p(jnp.exp(-jnp.abs(x)))


def _dot3(a_pieces, b, dims):
    out = None
    for piece in a_pieces:
        lhs, rhs = (piece, b) if dims == "piece_lhs" else (b, piece)
        contract = (((1,), (0,)), ((), ())) if dims == "piece_lhs" else (((1,), (1,)), ((), ()))
        term = lax.dot_general(lhs, rhs, contract, preferred_element_type=F32)
        out = term if out is None else out + term
    return out


def _conv_tile(xin, w_ref, b_ref, cols, pad_lo, pad_hi):
    R = xin.shape[0]
    half = CONV_WIDTH // 2
    t_idx = lax.broadcasted_iota(jnp.int32, (R, 1), 0)
    acc = b_ref[:, cols] + w_ref[half:half + 1, cols] * xin
    for j in range(CONV_WIDTH):
        off = j - half
        if off == 0:
            continue
        rolled = pltpu.roll(xin, (-off) % R, axis=0)
        if off < 0 and pad_lo:
            rolled = jnp.where(t_idx >= -off, rolled, 0.0)
        if off > 0 and pad_hi:
            rolled = jnp.where(t_idx < R - off, rolled, 0.0)
        acc = acc + w_ref[j:j + 1, cols] * rolled
    return acc * _sigmoid(acc)


def _ssd_kernel(xbc_ref, z_ref, dt_ref, cw_ref, cb_ref, dtb_ref, alog_ref, dsk_ref, nw_ref, tri_ref,
                o_ref, xc_ref, ylo_ref, yhi_ref, st_ref, bt_ref, dtt_ref, cumrow_ref, cumcol_ref, gm_ref):
    y_refs = (ylo_ref, yhi_ref)
    S = xbc_ref.shape[0]
    n_chunks = S // CHUNK
    pair_tiles = SSD_WIDTH // LANES
    tiles_per_group = pair_tiles // SSD_GROUPS
    n_heads_all = N_DIRS * SSD_HEADS
    b_off = SSD_WIDTH
    c_off = SSD_WIDTH + SSD_GROUPS * D_STATE
    lane = lax.broadcasted_iota(jnp.int32, (1, LANES), 1)
    low = lane < HALF_LANES

    edge = 2 * SUBLANES
    for c in range(CONV_CH // LANES):
        cols = slice(c * LANES, (c + 1) * LANES)
        xc_ref[:, cols] = _conv_tile(xbc_ref[:, cols], cw_ref, cb_ref, cols, False, False)
        xc_ref[0:edge // 2, cols] = _conv_tile(xbc_ref[0:edge, cols], cw_ref, cb_ref, cols, True, False)[0:edge // 2]
        xc_ref[S - edge // 2:S, cols] = _conv_tile(xbc_ref[S - edge:S, cols], cw_ref, cb_ref, cols,
                                                   False, True)[edge // 2:edge]

    for y_half in y_refs:
        y_half[...] = jnp.zeros_like(y_half)
    st_ref[...] = jnp.zeros_like(st_ref)

    a_col = -jnp.exp(alog_ref[...])
    dtb_col = dtb_ref[...]
    r_i = lax.broadcasted_iota(jnp.int32, (CHUNK, CHUNK), 0)
    c_i = lax.broadcasted_iota(jnp.int32, (CHUNK, CHUNK), 1)
    masks = (c_i <= r_i, c_i >= r_i)
    zero_rows = jnp.zeros((CHUNK - n_heads_all, CHUNK), BF16)
    fwd_rows = lax.broadcasted_iota(jnp.int32, (n_heads_all, 1), 0) < SSD_HEADS
    fwd_lanes = lane < SSD_HEADS

    def prep_chunk(c):
        rows = pl.ds(pl.multiple_of(c * CHUNK, CHUNK), CHUNK)
        dt_all = _softplus(dt_ref[rows, :].T[0:n_heads_all, :] + dtb_col)
        dtt_ref[c] = dt_all
        pieces = _split3(dt_all * a_col)
        padded = [jnp.concatenate([pc, zero_rows], axis=0) for pc in pieces]
        cumrow_ref[c] = jnp.where(fwd_rows, _dot3(pieces, tri_ref[1], "piece_lhs"),
                                  _dot3(pieces, tri_ref[0], "piece_lhs"))
        cumcol_ref[c] = jnp.where(fwd_lanes, _dot3(padded, tri_ref[0], "piece_rhs"),
                                  _dot3(padded, tri_ref[1], "piece_rhs"))
        for g in range(SSD_GROUPS):
            bt = xc_ref[rows, b_off + g * D_STATE:b_off + (g + 1) * D_STATE].T
            bt_ref[c, g * D_STATE:(g + 1) * D_STATE, :] = bt
            c_b = xc_ref[rows, c_off + g * D_STATE:c_off + (g + 1) * D_STATE].astype(BF16)
            gmat = jnp.dot(c_b, bt.astype(BF16), preferred_element_type=F32)
            for d in range(N_DIRS):
                k = d * SSD_GROUPS + g
                gm_ref[c, k * CHUNK:(k + 1) * CHUNK, :] = jnp.where(masks[d], gmat, 0.0).astype(BF16)

    prep_unroll = 4

    def prep(i, carry):
        for u in range(prep_unroll):
            prep_chunk(i * prep_unroll + u)
        return carry

    lax.fori_loop(0, n_chunks // prep_unroll, prep, 0)

    def one_chunk(c, d, y_half, c_local):
        rows = pl.ds(pl.multiple_of(c * CHUNK, CHUNK), CHUNK)
        rows_local = pl.ds(pl.multiple_of(c_local * CHUNK, CHUNK), CHUNK)
        hs = slice(d * SSD_HEADS, (d + 1) * SSD_HEADS)
        dt_d = dtt_ref[c][hs]
        cum_row = cumrow_ref[c][hs]
        cum_col = cumcol_ref[c]
        tot = jnp.sum(dt_d * a_col[hs], axis=1, keepdims=True)
        w_row = dt_d * jnp.exp(tot - cum_row)
        for g in range(SSD_GROUPS):
            c_b = xc_ref[rows, c_off + g * D_STATE:c_off + (g + 1) * D_STATE].astype(BF16)
            bt_g = bt_ref[c, g * D_STATE:(g + 1) * D_STATE, :]
            k = d * SSD_GROUPS + g
            gm_b = gm_ref[c, k * CHUNK:(k + 1) * CHUNK, :]
            for tt in range(tiles_per_group):
                t = g * tiles_per_group + tt
                cols = slice(t * LANES, (t + 1) * LANES)
                x_t = xc_ref[rows, cols].astype(BF16)
                st = st_ref[d, :, cols]
                rhs = jnp.concatenate([x_t, st.astype(BF16)], axis=0)
                zero_x = jnp.zeros_like(x_t)
                x_by_head = jnp.concatenate([jnp.where(low, x_t, zero_x), jnp.where(low, zero_x, x_t)], axis=0)
                ys, bts = [], []
                for hh in (2 * t, 2 * t + 1):
                    col = d * SSD_HEADS + hh
                    colb = jnp.broadcast_to(cum_col[:, col:col + 1], (CHUNK, CHUNK))
                    seg = jnp.minimum(colb - cum_row[hh:hh + 1, :], 0.0)
                    w = (jnp.exp(seg) * dt_d[hh:hh + 1, :]).astype(BF16) * gm_b
                    a = jnp.exp(colb).astype(BF16) * c_b
                    lhs = jnp.concatenate([w, a], axis=1)
                    ys.append(jnp.dot(lhs, rhs, preferred_element_type=F32))
                    bts.append((bt_g * w_row[hh:hh + 1, :]).astype(BF16))
                y_half[rows_local, cols] = y_half[rows_local, cols] + jnp.where(low, ys[0], ys[1])
                decay = jnp.exp(jnp.where(low, tot[2 * t:2 * t + 1, :], tot[2 * t + 1:2 * t + 2, :]))
                st_ref[d, :, cols] = st * decay + jnp.dot(jnp.concatenate(bts, axis=1), x_by_head,
                                                          preferred_element_type=F32)

    half_chunks = n_chunks // 2

    def body_first(c, carry):
        one_chunk(c, 0, y_refs[0], c)
        one_chunk(n_chunks - 1 - c, 1, y_refs[1], half_chunks - 1 - c)
        return carry

    def body_second(c, carry):
        one_chunk(c, 0, y_refs[1], c - half_chunks)
        one_chunk(n_chunks - 1 - c, 1, y_refs[0], n_chunks - 1 - c)
        return carry

    lax.fori_loop(0, half_chunks, body_first, 0)
    lax.fori_loop(half_chunks, n_chunks, body_second, 0)

    group_w = SSD_WIDTH // SSD_GROUPS
    fin_unroll = 2
    for half, y_half in enumerate(y_refs):
        def fin(i, carry, half=half, y_half=y_half):
            for u in range(fin_unroll):
                c = i * fin_unroll + u
                rows = pl.ds(pl.multiple_of((half * half_chunks + c) * CHUNK, CHUNK), CHUNK)
                rows_local = pl.ds(pl.multiple_of(c * CHUNK, CHUNK), CHUNK)
                zz = z_ref[rows, :]
                y = (y_half[rows_local, :] + dsk_ref[...] * xc_ref[rows, 0:SSD_WIDTH]) * (zz * _sigmoid(zz))
                outs = []
                for g in range(SSD_GROUPS):
                    yg = y[:, g * group_w:(g + 1) * group_w]
                    ms = jnp.mean(yg * yg, axis=-1, keepdims=True)
                    outs.append(yg * lax.rsqrt(ms + RMS_EPS))
                o_ref[rows, :] = (jnp.concatenate(outs, axis=-1) * nw_ref[...]).astype(BF16)
            return carry

        lax.fori_loop(0, half_chunks // fin_unroll, fin, 0)


def _ssd_call(xbc, z, dt, cw, cb, dtb, alog, dsk, nw, tri, batch):
    T = xbc.shape[0]
    n_chunks = SEQ // CHUNK
    seq = lambda b: (b, 0)
    return pl.pallas_call(
        _ssd_kernel, grid=(batch,),
        in_specs=[pl.BlockSpec((SEQ, CONV_CH), seq), pl.BlockSpec((SEQ, SSD_WIDTH), seq),
                  pl.BlockSpec((SEQ, DT_PAD), seq),
                  _const_spec(cw.shape), _const_spec(cb.shape), _const_spec(dtb.shape),
                  _const_spec(alog.shape), _const_spec(dsk.shape), _const_spec(nw.shape),
                  _const_spec(tri.shape)],
        out_specs=pl.BlockSpec((SEQ, SSD_WIDTH), seq),
        out_shape=jax.ShapeDtypeStruct((T, SSD_WIDTH), BF16),
        scratch_shapes=[pltpu.VMEM((SEQ, CONV_CH), F32),
                        pltpu.VMEM((SEQ // 2, SSD_WIDTH), F32),
                        pltpu.VMEM((SEQ // 2, SSD_WIDTH), F32),
                        pltpu.VMEM((N_DIRS, D_STATE, SSD_WIDTH), F32),
                        pltpu.VMEM((n_chunks, SSD_GROUPS * D_STATE, CHUNK), F32),
                        pltpu.VMEM((n_chunks, N_DIRS * SSD_HEADS, CHUNK), F32),
                        pltpu.VMEM((n_chunks, N_DIRS * SSD_HEADS, CHUNK), F32),
                        pltpu.VMEM((n_chunks, CHUNK, LANES), F32),
                        pltpu.VMEM((n_chunks, N_DIRS * SSD_GROUPS * CHUNK, CHUNK), BF16)],
        name="ssd_bidir",
        compiler_params=pltpu.CompilerParams(dimension_semantics=("arbitrary",),
                                             vmem_limit_bytes=VMEM_LIMIT),
    )(xbc, z, dt, cw, cb, dtb, alog, dsk, nw, tri)


def _back_kernel(x1_ref, att_ref, ssd_ref, p_ref, wout_ref, wgu_ref, wo_ref,
                 lng_ref, lnb_ref, wp_ref, wgate_ref, bgate_ref, o_ref, pre_ref):
    i = pl.program_id(0)
    last = pl.num_programs(0) - 1
    subs = [slice(r0, r0 + NORM_SUB) for r0 in range(0, x1_ref.shape[0], NORM_SUB)]

    def embed(rows):
        x3 = _layer_norm(pre_ref[rows, :], lng_ref[1:2, :], lnb_ref[1:2, :])
        e = jnp.dot(p_ref[rows, :].astype(BF16), wp_ref[...], preferred_element_type=F32)
        gate = _sigmoid(jnp.dot(x3.astype(BF16), wgate_ref[...], preferred_element_type=F32) + bgate_ref[...])
        o_ref[rows, :] = _layer_norm(ALPHA * x3 + gate * e, lng_ref[2:3, :], lnb_ref[2:3, :])

    late_steps = [functools.partial(embed, rows) for rows in subs]

    def ffn_stage():
        x2_parts = []
        for rows in subs:
            mix = (jnp.dot(att_ref[rows, :], wout_ref[0:ATT_WIDTH, :], preferred_element_type=F32)
                   + jnp.dot(ssd_ref[rows, :], wout_ref[ATT_WIDTH:, :], preferred_element_type=F32))
            x2_parts.append(_layer_norm(ALPHA * x1_ref[rows, :] + mix, lng_ref[0:1, :], lnb_ref[0:1, :]))
        x2 = jnp.concatenate(x2_parts, axis=0)
        f = _swiglu(x2.astype(BF16), wgu_ref, wo_ref, FF_CHUNKS_BACK, before=late_steps)
        pre_ref[...] = ALPHA * x2 + 0.5 * f

    @pl.when(i == 0)
    def _():
        pre_ref[...] = jnp.zeros_like(pre_ref)

    @pl.when(i < last)
    def _():
        ffn_stage()

    @pl.when(i == last)
    def _():
        for step in late_steps:
            step()


def _back_call(x1, att, ssd, p2d, wout, wgu, wo, lng, lnb, wp, wgate, bgate):
    T = x1.shape[0]
    tm = TOKEN_TILE
    n_tiles = T // tm
    cur = lambda i: (jnp.minimum(i, n_tiles - 1), 0)
    late = lambda i: (jnp.maximum(i - 1, 0), 0)
    consts = (wout, wgu, wo, lng, lnb, wp, wgate, bgate)
    return pl.pallas_call(
        _back_kernel, grid=(n_tiles + 1,),
        in_specs=[pl.BlockSpec((tm, D_MODEL), cur), pl.BlockSpec((tm, ATT_WIDTH), cur),
                  pl.BlockSpec((tm, SSD_WIDTH), cur), pl.BlockSpec((tm, PLE_DIM), late)]
                 + [_const_spec(c.shape) for c in consts],
        out_specs=pl.BlockSpec((tm, D_MODEL), late),
        out_shape=jax.ShapeDtypeStruct((T, D_MODEL), F32),
        scratch_shapes=[pltpu.VMEM((tm, D_MODEL), F32)],
        name="back_outproj_ffn_ple",
        compiler_params=pltpu.CompilerParams(dimension_semantics=("arbitrary",),
                                             vmem_limit_bytes=VMEM_LIMIT),
    )(x1, att, ssd, p2d, *consts)


def _in_proj_weight(w):
    d = w.shape[0]
    half_blk = HEAD_DIM // 2
    n_q = N_Q_HEADS * HEAD_DIM
    n_kv = N_KV_HEADS * HEAD_DIM
    wq = w[:, :n_q].reshape(d, N_Q_HEADS // 2, 2, half_blk, 2).transpose(0, 1, 4, 2, 3).reshape(d, n_q)
    wk = w[:, n_q:n_q + n_kv].reshape(d, N_KV_HEADS, half_blk, 2).transpose(0, 1, 3, 2)
    wk = jnp.broadcast_to(wk[:, :, :, None, :], (d, N_KV_HEADS, 2, 2, half_blk)).reshape(d, N_KV_HEADS * LANES)
    wv = w[:, n_q + n_kv:n_q + 2 * n_kv].reshape(d, N_KV_HEADS, 1, HEAD_DIM)
    wv = jnp.broadcast_to(wv, (d, N_KV_HEADS, 2, HEAD_DIM)).reshape(d, N_KV_HEADS * LANES)
    rest = w[:, n_q + 2 * n_kv:]
    pad = jnp.zeros((d, DT_PAD - N_DIRS * SSD_HEADS), w.dtype)
    return jnp.concatenate([wq, wk, wv, rest, pad], axis=1).astype(BF16)


def _rope_tiles():
    rows = SEQ // GRID_W
    row = jnp.repeat(jnp.arange(rows, dtype=F32), GRID_W)
    col = jnp.tile(jnp.arange(GRID_W, dtype=F32), rows)
    inv = ROPE_THETA ** (-jnp.arange(0, ROPE_AXIS_DIM, 2, dtype=F32) / ROPE_AXIS_DIM)
    ang = jnp.concatenate([row[:, None] * inv, col[:, None] * inv], axis=-1)
    cos, sin = jnp.cos(ang), jnp.sin(ang)
    cos_t = jnp.concatenate([cos, cos, cos, cos], axis=-1)
    sin_t = jnp.concatenate([-sin, -sin, sin, sin], axis=-1)
    return cos_t, sin_t


def _norm_tile(w):
    lanes = np.arange(LANES)
    idx = 2 * (lanes % (HEAD_DIM // 2)) + lanes // HALF_LANES
    return w[idx][None, :]


def kernel(x, p, ln_g, ln_b, ffn1_w_in, ffn1_w_out, w_in, q_norm, k_norm, conv_w, conv_b, dt_bias,
           a_log, d_skip, ssd_norm, w_out, ffn2_w_in, ffn2_w_out, ple_w, ple_gate_w, ple_gate_b):
    B, S, D = x.shape
    assert (S, D) == (SEQ, D_MODEL) and ln_g.shape[0] == DEPTH == 1
    T = B * S
    x2d = x.reshape(T, D)
    p2d = p[0].reshape(T, PLE_DIM)

    cos_t, sin_t = _rope_tiles()
    lanes = np.arange(LANES)
    same_head = ((lanes[:, None] // (HEAD_DIM // 2)) % 2) == ((lanes[None, :] // (HEAD_DIM // 2)) % 2)
    two_tiles = np.eye(2, dtype=np.float32)
    mq = jnp.asarray(np.kron(two_tiles, np.where(same_head, 1.0 / HEAD_DIM, 0.0)), BF16)
    mk = jnp.asarray(np.kron(two_tiles, np.full((LANES, LANES), 1.0 / LANES)), BF16)
    ones = np.ones((CHUNK, CHUNK), np.float32)
    tri = jnp.asarray(np.stack([np.tril(ones), np.triu(ones)]), BF16)

    i = 0
    wcat = _in_proj_weight(w_in[i])
    qw = _norm_tile(q_norm[i]) * (HEAD_DIM ** -0.5 * LOG2E)
    kw = _norm_tile(k_norm[i])
    x1, qh, kh, vh, z, xbc, dtr = _front_call(
        x2d, ffn1_w_in[i].astype(BF16), ffn1_w_out[i].astype(BF16), ln_g[i, 0:1], ln_b[i, 0:1],
        wcat, cos_t, sin_t, qw, kw, mq, mk)

    att = _attn_call(qh, kh, vh, B)

    dsk = jnp.repeat(d_skip[i], SSD_HEAD_DIM)[None, :]
    ssd = _ssd_call(xbc, z, dtr, conv_w[i], conv_b[i][None, :], dt_bias[i].reshape(-1, 1), a_log[i].reshape(-1, 1),
                    dsk, ssd_norm[i][None, :], tri, B)

    out = _back_call(x1, att, ssd, p2d, w_out[i].astype(BF16),
                     ffn2_w_in[i].astype(BF16), ffn2_w_out[i].astype(BF16), ln_g[i, 1:4], ln_b[i, 1:4],
                     ple_w[i].astype(BF16), ple_gate_w[i].astype(BF16), ple_gate_b[i][None, :])
    return out.reshape(B, S, D)
```

```python
import functools
import math

import numpy as np
import jax
import jax.numpy as jnp
from jax import lax
from jax.experimental import pallas as pl
from jax.experimental.pallas import tpu as pltpu

F32 = jnp.float32
BF16 = jnp.bfloat16

D_MODEL = 1024
SEQ = 2048
DEPTH = 1
HEAD_DIM = 64
N_Q_HEADS = 8
N_KV_HEADS = 2
ROPE_AXIS_DIM = HEAD_DIM // 2
ROPE_THETA = 10000.0
GRID_W = 64
ATT_WIDTH = N_Q_HEADS * HEAD_DIM
SSD_WIDTH = 512
SSD_HEAD_DIM = 64
SSD_HEADS = 8
SSD_GROUPS = 2
D_STATE = 128
CONV_WIDTH = 5
CONV_CH = SSD_WIDTH + 2 * SSD_GROUPS * D_STATE
CHUNK = 128
N_DIRS = 2
D_FF = 2816
PLE_DIM = 256
ALPHA = (2.0 * DEPTH) ** 0.25
LN_EPS = 1e-5
RMS_EPS = 1e-6
LOG2E = 1.4426950408889634

LANES = 128
HALF_LANES = LANES // 2
SUBLANES = 8
TOKEN_TILE = 512
NORM_SUB = 256
Q_TILE = 1024
Q_UNIT = 512
FF_CHUNKS_FRONT = ((0, 512), (512, 512), (1024, 512), (1536, 512), (2048, 768))
FF_CHUNKS_BACK = ((0, 768), (768, 768), (1536, 768), (2304, 512))
DT_PAD = LANES
Q_OFF = 0
K_OFF = Q_OFF + ATT_WIDTH
V_OFF = K_OFF + N_KV_HEADS * LANES
Z_OFF = V_OFF + N_KV_HEADS * LANES
XBC_OFF = Z_OFF + SSD_WIDTH
DT_OFF = XBC_OFF + CONV_CH
IN_COLS = DT_OFF + DT_PAD
VMEM_LIMIT = 56 * 1024 * 1024


def _const_spec(shape):
    nd = len(shape)
    return pl.BlockSpec(shape, lambda *_: (0,) * nd, pipeline_mode=pl.Buffered(1))


def _layer_norm(y, g, b):
    mu = jnp.mean(y, axis=-1, keepdims=True)
    yc = y - mu
    var = jnp.mean(yc * yc, axis=-1, keepdims=True)
    return yc * lax.rsqrt(var + LN_EPS) * g + b


def _sigmoid(x):
    return 1.0 / (1.0 + jnp.exp(-x))


def _swiglu(xb, wgu_ref, wo_ref, chunks, before=()):
    acc = None
    for k, (c0, cw) in enumerate(chunks):
        if k < len(before):
            before[k]()
        hg = jnp.dot(xb, wgu_ref[:, c0:c0 + cw], preferred_element_type=F32)
        hu = jnp.dot(xb, wgu_ref[:, D_FF + c0:D_FF + c0 + cw], preferred_element_type=F32)
        act = (hg * _sigmoid(hg) * hu).astype(BF16)
        part = jnp.dot(act, wo_ref[c0:c0 + cw, :], preferred_element_type=F32)
        acc = part if acc is None else acc + part
    return acc


def _split3(x):
    hi = x.astype(BF16)
    r1 = x - hi.astype(F32)
    mid = r1.astype(BF16)
    lo = (r1 - mid.astype(F32)).astype(BF16)
    return hi, mid, lo


def _norm_rope(t2, seg_mean, w, cos, sin):
    ms = jnp.dot((t2 * t2).astype(BF16), seg_mean, preferred_element_type=F32)
    tn2 = t2 * lax.rsqrt(ms + RMS_EPS)
    outs = []
    for j in range(2):
        tn = tn2[:, j * LANES:(j + 1) * LANES] * w
        outs.append(tn * cos + pltpu.roll(tn, HALF_LANES, axis=1) * sin)
    return outs


def _front_kernel(x_ref, wgu_ref, wo_ref, g_ref, b_ref, win_ref,
                  cos_ref, sin_ref, qw_ref, kw_ref, mq_ref, mk_ref,
                  x1_ref, q_ref, k_ref, v_ref, z_ref, xbc_ref, dt_ref, pre_ref):
    i = pl.program_id(0)
    last = pl.num_programs(0) - 1

    def project(r0):
        rows = slice(r0, r0 + NORM_SUB)
        pair = 2 * LANES
        x1 = _layer_norm(pre_ref[rows, :], g_ref[...], b_ref[...])
        x1_ref[rows, :] = x1
        u = jnp.dot(x1.astype(BF16), win_ref[...], preferred_element_type=F32)
        cos = cos_ref[rows, :]
        sin = sin_ref[rows, :]
        for jp in range(ATT_WIDTH // pair):
            tiles = _norm_rope(u[:, Q_OFF + jp * pair:Q_OFF + (jp + 1) * pair], mq_ref[...], qw_ref[...], cos, sin)
            for j, tile in enumerate(tiles):
                c0 = jp * pair + j * LANES
                q_ref[rows, c0:c0 + LANES] = tile.astype(BF16)
        tiles = _norm_rope(u[:, K_OFF:K_OFF + pair], mk_ref[...], kw_ref[...], cos, sin)
        for j, tile in enumerate(tiles):
            k_ref[rows, j * LANES:(j + 1) * LANES] = tile.astype(BF16)
        v_ref[rows, :] = u[:, V_OFF:Z_OFF].astype(BF16)
        z_ref[rows, :] = u[:, Z_OFF:XBC_OFF]
        xbc_ref[rows, :] = u[:, XBC_OFF:DT_OFF]
        dt_ref[rows, :] = u[:, DT_OFF:IN_COLS]

    late_steps = [functools.partial(project, r0) for r0 in range(0, pre_ref.shape[0], NORM_SUB)]

    @pl.when(i == 0)
    def _():
        pre_ref[...] = jnp.zeros_like(pre_ref)

    @pl.when(i < last)
    def _():
        x = x_ref[...]
        f = _swiglu(x.astype(BF16), wgu_ref, wo_ref, FF_CHUNKS_FRONT, before=late_steps)
        pre_ref[...] = ALPHA * x + 0.5 * f

    @pl.when(i == last)
    def _():
        for step in late_steps:
            step()


def _front_call(x2d, wgu, wo, g, b, wcat, cos_t, sin_t, qw, kw, mq, mk):
    T = x2d.shape[0]
    tm = TOKEN_TILE
    n_tiles = T // tm
    pos_blocks = SEQ // tm
    cur = lambda i: (jnp.minimum(i, n_tiles - 1), 0)
    late = lambda i: (jnp.maximum(i - 1, 0), 0)
    pos = lambda i: (jnp.maximum(i - 1, 0) % pos_blocks, 0)
    out_shape = (
        jax.ShapeDtypeStruct((T, D_MODEL), F32),
        jax.ShapeDtypeStruct((T, ATT_WIDTH), BF16),
        jax.ShapeDtypeStruct((T, N_KV_HEADS * LANES), BF16),
        jax.ShapeDtypeStruct((T, N_KV_HEADS * LANES), BF16),
        jax.ShapeDtypeStruct((T, SSD_WIDTH), F32),
        jax.ShapeDtypeStruct((T, CONV_CH), F32),
        jax.ShapeDtypeStruct((T, DT_PAD), F32),
    )
    in_specs = [
        pl.BlockSpec((tm, D_MODEL), cur),
        _const_spec(wgu.shape), _const_spec(wo.shape),
        _const_spec(g.shape), _const_spec(b.shape), _const_spec(wcat.shape),
        pl.BlockSpec((tm, LANES), pos), pl.BlockSpec((tm, LANES), pos),
        _const_spec(qw.shape), _const_spec(kw.shape), _const_spec(mq.shape), _const_spec(mk.shape),
    ]
    out_specs = tuple(pl.BlockSpec((tm, s.shape[1]), late) for s in out_shape)
    return pl.pallas_call(
        _front_kernel, grid=(n_tiles + 1,), in_specs=in_specs, out_specs=out_specs, out_shape=out_shape,
        scratch_shapes=[pltpu.VMEM((tm, D_MODEL), F32)],
        name="front_ffn_inproj",
        compiler_params=pltpu.CompilerParams(dimension_semantics=("arbitrary",),
                                             vmem_limit_bytes=VMEM_LIMIT),
    )(x2d, wgu, wo, g, b, wcat, cos_t, sin_t, qw, kw, mq, mk)


def _attn_kernel(q_ref, k_ref, v_ref, o_ref):
    tq = q_ref.shape[0]
    lane = lax.broadcasted_iota(jnp.int32, (1, LANES), 1)
    low = lane < HALF_LANES
    q_per_kv_tiles = (N_Q_HEADS // N_KV_HEADS) * HEAD_DIM // LANES
    for kv in range(N_KV_HEADS):
        kt = k_ref[:, kv * LANES:(kv + 1) * LANES]
        vt = v_ref[:, kv * LANES:(kv + 1) * LANES]
        zero = jnp.zeros_like(vt)
        v_lo = jnp.where(low, vt, zero)
        v_hi = jnp.where(low, zero, vt)
        blk = (lane // (HEAD_DIM // 2)) % 2
        for jj in range(q_per_kv_tiles):
            j = kv * q_per_kv_tiles + jj
            cols = slice(j * LANES, (j + 1) * LANES)
            for r0 in range(0, tq, Q_UNIT):
                rows = slice(r0, r0 + Q_UNIT)
                qt = q_ref[rows, cols]
                zq = jnp.zeros_like(qt)
                out = None
                for a, v_a in ((0, v_lo), (1, v_hi)):
                    qa = jnp.where(blk == a, qt, zq)
                    s = lax.dot_general(qa, kt, (((1,), (1,)), ((), ())), preferred_element_type=F32)
                    m = jnp.max(s, axis=-1, keepdims=True)
                    p = jnp.exp2(s - m)
                    l = jnp.sum(p, axis=-1, keepdims=True)
                    o = jnp.dot(p.astype(BF16), v_a, preferred_element_type=F32) * (1.0 / l)
                    out = o if out is None else out + o
                o_ref[rows, cols] = out.astype(BF16)


def _attn_call(q, k, v, batch):
    T = q.shape[0]
    nq = SEQ // Q_TILE
    return pl.pallas_call(
        _attn_kernel, grid=(batch, nq),
        in_specs=[pl.BlockSpec((Q_TILE, ATT_WIDTH), lambda b, i: (b * nq + i, 0)),
                  pl.BlockSpec((SEQ, N_KV_HEADS * LANES), lambda b, i: (b, 0)),
                  pl.BlockSpec((SEQ, N_KV_HEADS * LANES), lambda b, i: (b, 0))],
        out_specs=pl.BlockSpec((Q_TILE, ATT_WIDTH), lambda b, i: (b * nq + i, 0)),
        out_shape=jax.ShapeDtypeStruct((T, ATT_WIDTH), BF16),
        name="attention",
        compiler_params=pltpu.CompilerParams(dimension_semantics=("arbitrary", "arbitrary"),
                                             vmem_limit_bytes=VMEM_LIMIT),
    )(q, k, v)


def _softplus(x):
    return jnp.maximum(x, 0.0) + jnp.log1p(jnp.exp(-jnp.abs(x)))


def _dot3(a_pieces, b, dims):
    out = None
    for piece in a_pieces:
        lhs, rhs = (piece, b) if dims == "piece_lhs" else (b, piece)
        contract = (((1,), (0,)), ((), ())) if dims == "piece_lhs" else (((1,), (1,)), ((), ()))
        term = lax.dot_general(lhs, rhs, contract, preferred_element_type=F32)
        out = term if out is None else out + term
    return out


def _conv_tile(xin, w_ref, b_ref, cols, pad_lo, pad_hi):
    R = xin.shape[0]
    half = CONV_WIDTH // 2
    t_idx = lax.broadcasted_iota(jnp.int32, (R, 1), 0)
    acc = b_ref[:, cols] + w_ref[half:half + 1, cols] * xin
    for j in range(CONV_WIDTH):
        off = j - half
        if off == 0:
            continue
        rolled = pltpu.roll(xin, (-off) % R, axis=0)
        if off < 0 and pad_lo:
            rolled = jnp.where(t_idx >= -off, rolled, 0.0)
        if off > 0 and pad_hi:
            rolled = jnp.where(t_idx < R - off, rolled, 0.0)
        acc = acc + w_ref[j:j + 1, cols] * rolled
    return acc * _sigmoid(acc)


def _ssd_kernel(xbc_ref, z_ref, dt_ref, cw_ref, cb_ref, dtb_ref, alog_ref, dsk_ref, nw_ref, tri_ref,
                o_ref, xc_ref, ylo_ref, yhi_ref, st_ref, bt_ref, dtt_ref, cumrow_ref, cumcol_ref, gm_ref):
    y_refs = (ylo_ref, yhi_ref)
    S = xbc_ref.shape[0]
    n_chunks = S // CHUNK
    pair_tiles = SSD_WIDTH // LANES
    tiles_per_group = pair_tiles // SSD_GROUPS
    n_heads_all = N_DIRS * SSD_HEADS
    b_off = SSD_WIDTH
    c_off = SSD_WIDTH + SSD_GROUPS * D_STATE
    lane = lax.broadcasted_iota(jnp.int32, (1, LANES), 1)
    low = lane < HALF_LANES

    edge = 2 * SUBLANES
    for c in range(CONV_CH // LANES):
        cols = slice(c * LANES, (c + 1) * LANES)
        xc_ref[:, cols] = _conv_tile(xbc_ref[:, cols], cw_ref, cb_ref, cols, False, False)
        xc_ref[0:edge // 2, cols] = _conv_tile(xbc_ref[0:edge, cols], cw_ref, cb_ref, cols, True, False)[0:edge // 2]
        xc_ref[S - edge // 2:S, cols] = _conv_tile(xbc_ref[S - edge:S, cols], cw_ref, cb_ref, cols,
                                                   False, True)[edge // 2:edge]

    for y_half in y_refs:
        y_half[...] = jnp.zeros_like(y_half)
    st_ref[...] = jnp.zeros_like(st_ref)

    a_col = -jnp.exp(alog_ref[...])
    dtb_col = dtb_ref[...]
    r_i = lax.broadcasted_iota(jnp.int32, (CHUNK, CHUNK), 0)
    c_i = lax.broadcasted_iota(jnp.int32, (CHUNK, CHUNK), 1)
    masks = (c_i <= r_i, c_i >= r_i)
    zero_rows = jnp.zeros((CHUNK - n_heads_all, CHUNK), BF16)
    fwd_rows = lax.broadcasted_iota(jnp.int32, (n_heads_all, 1), 0) < SSD_HEADS
    fwd_lanes = lane < SSD_HEADS

    def prep_chunk(c):
        rows = pl.ds(pl.multiple_of(c * CHUNK, CHUNK), CHUNK)
        dt_all = _softplus(dt_ref[rows, :].T[0:n_heads_all, :] + dtb_col)
        dtt_ref[c] = dt_all
        pieces = _split3(dt_all * a_col)
        padded = [jnp.concatenate([pc, zero_rows], axis=0) for pc in pieces]
        cumrow_ref[c] = jnp.where(fwd_rows, _dot3(pieces, tri_ref[1], "piece_lhs"),
                                  _dot3(pieces, tri_ref[0], "piece_lhs"))
        cumcol_ref[c] = jnp.where(fwd_lanes, _dot3(padded, tri_ref[0], "piece_rhs"),
                                  _dot3(padded, tri_ref[1], "piece_rhs"))
        for g in range(SSD_GROUPS):
            bt = xc_ref[rows, b_off + g * D_STATE:b_off + (g + 1) * D_STATE].T
            bt_ref[c, g * D_STATE:(g + 1) * D_STATE, :] = bt
            c_b = xc_ref[rows, c_off + g * D_STATE:c_off + (g + 1) * D_STATE].astype(BF16)
            gmat = jnp.dot(c_b, bt.astype(BF16), preferred_element_type=F32)
            for d in range(N_DIRS):
                k = d * SSD_GROUPS + g
                gm_ref[c, k * CHUNK:(k + 1) * CHUNK, :] = jnp.where(masks[d], gmat, 0.0).astype(BF16)

    prep_unroll = 4

    def prep(i, carry):
        for u in range(prep_unroll):
            prep_chunk(i * prep_unroll + u)
        return carry

    lax.fori_loop(0, n_chunks // prep_unroll, prep, 0)

    def one_chunk(c, d, y_half, c_local):
        rows = pl.ds(pl.multiple_of(c * CHUNK, CHUNK), CHUNK)
        rows_local = pl.ds(pl.multiple_of(c_local * CHUNK, CHUNK), CHUNK)
        hs = slice(d * SSD_HEADS, (d + 1) * SSD_HEADS)
        dt_d = dtt_ref[c][hs]
        cum_row = cumrow_ref[c][hs]
        cum_col = cumcol_ref[c]
        tot = jnp.sum(dt_d * a_col[hs], axis=1, keepdims=True)
        w_row = dt_d * jnp.exp(tot - cum_row)
        for g in range(SSD_GROUPS):
            c_b = xc_ref[rows, c_off + g * D_STATE:c_off + (g + 1) * D_STATE].astype(BF16)
            bt_g = bt_ref[c, g * D_STATE:(g + 1) * D_STATE, :]
            k = d * SSD_GROUPS + g
            gm_b = gm_ref[c, k * CHUNK:(k + 1) * CHUNK, :]
            for tt in range(tiles_per_group):
                t = g * tiles_per_group + tt
                cols = slice(t * LANES, (t + 1) * LANES)
                x_t = xc_ref[rows, cols].astype(BF16)
                st = st_ref[d, :, cols]
                rhs = jnp.concatenate([x_t, st.astype(BF16)], axis=0)
                zero_x = jnp.zeros_like(x_t)
                x_by_head = jnp.concatenate([jnp.where(low, x_t, zero_x), jnp.where(low, zero_x, x_t)], axis=0)
                ys, bts = [], []
                for hh in (2 * t, 2 * t + 1):
                    col = d * SSD_HEADS + hh
                    colb = jnp.broadcast_to(cum_col[:, col:col + 1], (CHUNK, CHUNK))
                    seg = jnp.minimum(colb - cum_row[hh:hh + 1, :], 0.0)
                    w = (jnp.exp(seg) * dt_d[hh:hh + 1, :]).astype(BF16) * gm_b
                    a = jnp.exp(colb).astype(BF16) * c_b
                    lhs = jnp.concatenate([w, a], axis=1)
                    ys.append(jnp.dot(lhs, rhs, preferred_element_type=F32))
                    bts.append((bt_g * w_row[hh:hh + 1, :]).astype(BF16))
                y_half[rows_local, cols] = y_half[rows_local, cols] + jnp.where(low, ys[0], ys[1])
                decay = jnp.exp(jnp.where(low, tot[2 * t:2 * t + 1, :], tot[2 * t + 1:2 * t + 2, :]))
                st_ref[d, :, cols] = st * decay + jnp.dot(jnp.concatenate(bts, axis=1), x_by_head,
                                                          preferred_element_type=F32)

    half_chunks = n_chunks // 2

    def body_first(c, carry):
        one_chunk(c, 0, y_refs[0], c)
        one_chunk(n_chunks - 1 - c, 1, y_refs[1], half_chunks - 1 - c)
        return carry

    def body_second(c, carry):
        one_chunk(c, 0, y_refs[1], c - half_chunks)
        one_chunk(n_chunks - 1 - c, 1, y_refs[0], n_chunks - 1 - c)
        return carry

    lax.fori_loop(0, half_chunks, body_first, 0)
    lax.fori_loop(half_chunks, n_chunks, body_second, 0)

    group_w = SSD_WIDTH // SSD_GROUPS
    fin_unroll = 2
    for half, y_half in enumerate(y_refs):
        def fin(i, carry, half=half, y_half=y_half):
            for u in range(fin_unroll):
                c = i * fin_unroll + u
                rows = pl.ds(pl.multiple_of((half * half_chunks + c) * CHUNK, CHUNK), CHUNK)
                rows_local = pl.ds(pl.multiple_of(c * CHUNK, CHUNK), CHUNK)
                zz = z_ref[rows, :]
                y = (y_half[rows_local, :] + dsk_ref[...] * xc_ref[rows, 0:SSD_WIDTH]) * (zz * _sigmoid(zz))
                outs = []
                for g in range(SSD_GROUPS):
                    yg = y[:, g * group_w:(g + 1) * group_w]
                    ms = jnp.mean(yg * yg, axis=-1, keepdims=True)
                    outs.append(yg * lax.rsqrt(ms + RMS_EPS))
                o_ref[rows, :] = (jnp.concatenate(outs, axis=-1) * nw_ref[...]).astype(BF16)
            return carry

        lax.fori_loop(0, half_chunks // fin_unroll, fin, 0)


def _ssd_call(xbc, z, dt, cw, cb, dtb, alog, dsk, nw, tri, batch):
    T = xbc.shape[0]
    n_chunks = SEQ // CHUNK
    seq = lambda b: (b, 0)
    return pl.pallas_call(
        _ssd_kernel, grid=(batch,),
        in_specs=[pl.BlockSpec((SEQ, CONV_CH), seq), pl.BlockSpec((SEQ, SSD_WIDTH), seq),
                  pl.BlockSpec((SEQ, DT_PAD), seq),
                  _const_spec(cw.shape), _const_spec(cb.shape), _const_spec(dtb.shape),
                  _const_spec(alog.shape), _const_spec(dsk.shape), _const_spec(nw.shape),
                  _const_spec(tri.shape)],
        out_specs=pl.BlockSpec((SEQ, SSD_WIDTH), seq),
        out_shape=jax.ShapeDtypeStruct((T, SSD_WIDTH), BF16),
        scratch_shapes=[pltpu.VMEM((SEQ, CONV_CH), F32),
                        pltpu.VMEM((SEQ // 2, SSD_WIDTH), F32),
                        pltpu.VMEM((SEQ // 2, SSD_WIDTH), F32),
                        pltpu.VMEM((N_DIRS, D_STATE, SSD_WIDTH), F32),
                        pltpu.VMEM((n_chunks, SSD_GROUPS * D_STATE, CHUNK), F32),
                        pltpu.VMEM((n_chunks, N_DIRS * SSD_HEADS, CHUNK), F32),
                        pltpu.VMEM((n_chunks, N_DIRS * SSD_HEADS, CHUNK), F32),
                        pltpu.VMEM((n_chunks, CHUNK, LANES), F32),
                        pltpu.VMEM((n_chunks, N_DIRS * SSD_GROUPS * CHUNK, CHUNK), BF16)],
        name="ssd_bidir",
        compiler_params=pltpu.CompilerParams(dimension_semantics=("arbitrary",),
                                             vmem_limit_bytes=VMEM_LIMIT),
    )(xbc, z, dt, cw, cb, dtb, alog, dsk, nw, tri)


def _back_kernel(x1_ref, att_ref, ssd_ref, p_ref, wout_ref, wgu_ref, wo_ref,
                 lng_ref, lnb_ref, wp_ref, wgate_ref, bgate_ref, o_ref, pre_ref):
    i = pl.program_id(0)
    last = pl.num_programs(0) - 1
    subs = [slice(r0, r0 + NORM_SUB) for r0 in range(0, x1_ref.shape[0], NORM_SUB)]

    def embed(rows):
        x3 = _layer_norm(pre_ref[rows, :], lng_ref[1:2, :], lnb_ref[1:2, :])
        e = jnp.dot(p_ref[rows, :].astype(BF16), wp_ref[...], preferred_element_type=F32)
        gate = _sigmoid(jnp.dot(x3.astype(BF16), wgate_ref[...], preferred_element_type=F32) + bgate_ref[...])
        o_ref[rows, :] = _layer_norm(ALPHA * x3 + gate * e, lng_ref[2:3, :], lnb_ref[2:3, :])

    late_steps = [functools.partial(embed, rows) for rows in subs]

    def ffn_stage():
        x2_parts = []
        for rows in subs:
            mix = (jnp.dot(att_ref[rows, :], wout_ref[0:ATT_WIDTH, :], preferred_element_type=F32)
                   + jnp.dot(ssd_ref[rows, :], wout_ref[ATT_WIDTH:, :], preferred_element_type=F32))
            x2_parts.append(_layer_norm(ALPHA * x1_ref[rows, :] + mix, lng_ref[0:1, :], lnb_ref[0:1, :]))
        x2 = jnp.concatenate(x2_parts, axis=0)
        f = _swiglu(x2.astype(BF16), wgu_ref, wo_ref, FF_CHUNKS_BACK, before=late_steps)
        pre_ref[...] = ALPHA * x2 + 0.5 * f

    @pl.when(i == 0)
    def _():
        pre_ref[...] = jnp.zeros_like(pre_ref)

    @pl.when(i < last)
    def _():
        ffn_stage()

    @pl.when(i == last)
    def _():
        for step in late_steps:
            step()


def _back_call(x1, att, ssd, p2d, wout, wgu, wo, lng, lnb, wp, wgate, bgate):
    T = x1.shape[0]
    tm = TOKEN_TILE
    n_tiles = T // tm
    cur = lambda i: (jnp.minimum(i, n_tiles - 1), 0)
    late = lambda i: (jnp.maximum(i - 1, 0), 0)
    consts = (wout, wgu, wo, lng, lnb, wp, wgate, bgate)
    return pl.pallas_call(
        _back_kernel, grid=(n_tiles + 1,),
        in_specs=[pl.BlockSpec((tm, D_MODEL), cur), pl.BlockSpec((tm, ATT_WIDTH), cur),
                  pl.BlockSpec((tm, SSD_WIDTH), cur), pl.BlockSpec((tm, PLE_DIM), late)]
                 + [_const_spec(c.shape) for c in consts],
        out_specs=pl.BlockSpec((tm, D_MODEL), late),
        out_shape=jax.ShapeDtypeStruct((T, D_MODEL), F32),
        scratch_shapes=[pltpu.VMEM((tm, D_MODEL), F32)],
        name="back_outproj_ffn_ple",
        compiler_params=pltpu.CompilerParams(dimension_semantics=("arbitrary",),
                                             vmem_limit_bytes=VMEM_LIMIT),
    )(x1, att, ssd, p2d, *consts)


def _in_proj_weight(w):
    half_blk = HEAD_DIM // 2
    n_q = N_Q_HEADS * HEAD_DIM
    n_kv = N_KV_HEADS * HEAD_DIM
    n_dt = N_DIRS * SSD_HEADS
    cols = np.arange(w.shape[1])
    q = cols[:n_q].reshape(N_Q_HEADS // 2, 2, half_blk, 2).transpose(0, 3, 1, 2).reshape(-1)
    k = cols[n_q:n_q + n_kv].reshape(N_KV_HEADS, half_blk, 2).transpose(0, 2, 1)
    k = np.broadcast_to(k[:, :, None, :], (N_KV_HEADS, 2, 2, half_blk)).reshape(-1)
    v = np.broadcast_to(cols[n_q + n_kv:n_q + 2 * n_kv].reshape(N_KV_HEADS, 1, HEAD_DIM),
                        (N_KV_HEADS, 2, HEAD_DIM)).reshape(-1)
    rest = cols[n_q + 2 * n_kv:]
    pad = np.tile(cols[-n_dt:], (DT_PAD - n_dt) // n_dt)
    index = np.concatenate([q, k, v, rest, pad]).astype(np.int32)
    return w.astype(BF16)[:, index]


def _rope_tiles():
    rows = SEQ // GRID_W
    row = jnp.repeat(jnp.arange(rows, dtype=F32), GRID_W)
    col = jnp.tile(jnp.arange(GRID_W, dtype=F32), rows)
    inv = ROPE_THETA ** (-jnp.arange(0, ROPE_AXIS_DIM, 2, dtype=F32) / ROPE_AXIS_DIM)
    ang = jnp.concatenate([row[:, None] * inv, col[:, None] * inv], axis=-1)
    cos, sin = jnp.cos(ang), jnp.sin(ang)
    cos_t = jnp.concatenate([cos, cos, cos, cos], axis=-1)
    sin_t = jnp.concatenate([-sin, -sin, sin, sin], axis=-1)
    return cos_t, sin_t


def _norm_tile(w):
    lanes = np.arange(LANES)
    idx = 2 * (lanes % (HEAD_DIM // 2)) + lanes // HALF_LANES
    return w[idx][None, :]


def kernel(x, p, ln_g, ln_b, ffn1_w_in, ffn1_w_out, w_in, q_norm, k_norm, conv_w, conv_b, dt_bias,
           a_log, d_skip, ssd_norm, w_out, ffn2_w_in, ffn2_w_out, ple_w, ple_gate_w, ple_gate_b):
    B, S, D = x.shape
    assert (S, D) == (SEQ, D_MODEL) and ln_g.shape[0] == DEPTH == 1
    T = B * S
    x2d = x.reshape(T, D)
    p2d = p[0].reshape(T, PLE_DIM)

    cos_t, sin_t = _rope_tiles()
    lanes = np.arange(LANES)
    same_head = ((lanes[:, None] // (HEAD_DIM // 2)) % 2) == ((lanes[None, :] // (HEAD_DIM // 2)) % 2)
    two_tiles = np.eye(2, dtype=np.float32)
    mq = jnp.asarray(np.kron(two_tiles, np.where(same_head, 1.0 / HEAD_DIM, 0.0)), BF16)
    mk = jnp.asarray(np.kron(two_tiles, np.full((LANES, LANES), 1.0 / LANES)), BF16)
    ones = np.ones((CHUNK, CHUNK), np.float32)
    tri = jnp.asarray(np.stack([np.tril(ones), np.triu(ones)]), BF16)

    i = 0
    wcat = _in_proj_weight(w_in[i])
    qw = _norm_tile(q_norm[i]) * (HEAD_DIM ** -0.5 * LOG2E)
    kw = _norm_tile(k_norm[i])
    x1, qh, kh, vh, z, xbc, dtr = _front_call(
        x2d, ffn1_w_in[i].astype(BF16), ffn1_w_out[i].astype(BF16), ln_g[i, 0:1], ln_b[i, 0:1],
        wcat, cos_t, sin_t, qw, kw, mq, mk)

    att = _attn_call(qh, kh, vh, B)

    dsk = jnp.repeat(d_skip[i], SSD_HEAD_DIM)[None, :]
    ssd = _ssd_call(xbc, z, dtr, conv_w[i], conv_b[i][None, :], dt_bias[i].reshape(-1, 1), a_log[i].reshape(-1, 1),
                    dsk, ssd_norm[i][None, :], tri, B)

    out = _back_call(x1, att, ssd, p2d, w_out[i].astype(BF16),
                     ffn2_w_in[i].astype(BF16), ffn2_w_out[i].astype(BF16), ln_g[i, 1:4], ln_b[i, 1:4],
                     ple_w[i].astype(BF16), ple_gate_w[i].astype(BF16), ple_gate_b[i][None, :])
    return out.reshape(B, S, D)
```

```python
import functools
import math

import numpy as np
import jax
import jax.numpy as jnp
from jax import lax
from jax.experimental import pallas as pl
from jax.experimental.pallas import tpu as pltpu

F32 = jnp.float32
BF16 = jnp.bfloat16

D_MODEL = 1024
SEQ = 2048
DEPTH = 1
HEAD_DIM = 64
N_Q_HEADS = 8
N_KV_HEADS = 2
ROPE_AXIS_DIM = HEAD_DIM // 2
ROPE_THETA = 10000.0
GRID_W = 64
ATT_WIDTH = N_Q_HEADS * HEAD_DIM
SSD_WIDTH = 512
SSD_HEAD_DIM = 64
SSD_HEADS = 8
SSD_GROUPS = 2
D_STATE = 128
CONV_WIDTH = 5
CONV_CH = SSD_WIDTH + 2 * SSD_GROUPS * D_STATE
CHUNK = 128
N_DIRS = 2
D_FF = 2816
PLE_DIM = 256
ALPHA = (2.0 * DEPTH) ** 0.25
LN_EPS = 1e-5
RMS_EPS = 1e-6
LOG2E = 1.4426950408889634

LANES = 128
HALF_LANES = LANES // 2
SUBLANES = 8
TOKEN_TILE = 512
NORM_SUB = 256
Q_TILE = 1024
Q_UNIT = 512
FF_CHUNKS_FRONT = ((0, 512), (512, 512), (1024, 512), (1536, 512), (2048, 768))
FF_CHUNKS_BACK = ((0, 768), (768, 768), (1536, 768), (2304, 512))
DT_PAD = LANES
Q_OFF = 0
K_OFF = Q_OFF + ATT_WIDTH
V_OFF = K_OFF + N_KV_HEADS * LANES
Z_OFF = V_OFF + N_KV_HEADS * LANES
XBC_OFF = Z_OFF + SSD_WIDTH
DT_OFF = XBC_OFF + CONV_CH
IN_COLS = DT_OFF + DT_PAD
VMEM_LIMIT = 56 * 1024 * 1024
ATTN_VMEM_LIMIT = 62 * 1024 * 1024


def _const_spec(shape):
    nd = len(shape)
    return pl.BlockSpec(shape, lambda *_: (0,) * nd, pipeline_mode=pl.Buffered(1))


def _layer_norm(y, g, b):
    mu = jnp.mean(y, axis=-1, keepdims=True)
    yc = y - mu
    var = jnp.mean(yc * yc, axis=-1, keepdims=True)
    return yc * lax.rsqrt(var + LN_EPS) * g + b


def _sigmoid(x):
    return 1.0 / (1.0 + jnp.exp(-x))


def _swiglu(xb, wgu_ref, wo_ref, chunks, before=()):
    acc = None
    for k, (c0, cw) in enumerate(chunks):
        if k < len(before):
            before[k]()
        hg = jnp.dot(xb, wgu_ref[:, c0:c0 + cw], preferred_element_type=F32)
        hu = jnp.dot(xb, wgu_ref[:, D_FF + c0:D_FF + c0 + cw], preferred_element_type=F32)
        act = (hg * _sigmoid(hg) * hu).astype(BF16)
        part = jnp.dot(act, wo_ref[c0:c0 + cw, :], preferred_element_type=F32)
        acc = part if acc is None else acc + part
    return acc


def _split3(x):
    hi = x.astype(BF16)
    r1 = x - hi.astype(F32)
    mid = r1.astype(BF16)
    lo = (r1 - mid.astype(F32)).astype(BF16)
    return hi, mid, lo


def _norm_rope(t2, seg_mean, w, cos, sin):
    ms = jnp.dot((t2 * t2).astype(BF16), seg_mean, preferred_element_type=F32)
    tn2 = t2 * lax.rsqrt(ms + RMS_EPS)
    outs = []
    for j in range(2):
        tn = tn2[:, j * LANES:(j + 1) * LANES] * w
        outs.append(tn * cos + pltpu.roll(tn, HALF_LANES, axis=1) * sin)
    return outs


def _front_kernel(x_ref, wgu_ref, wo_ref, g_ref, b_ref, win_ref,
                  cos_ref, sin_ref, qw_ref, kw_ref, mq_ref, mk_ref,
                  x1_ref, q_ref, k_ref, v_ref, z_ref, xbc_ref, dt_ref, pre_ref):
    i = pl.program_id(0)
    last = pl.num_programs(0) - 1

    def project(r0):
        rows = slice(r0, r0 + NORM_SUB)
        pair = 2 * LANES
        x1 = _layer_norm(pre_ref[rows, :], g_ref[...], b_ref[...])
        x1_ref[rows, :] = x1
        u = jnp.dot(x1.astype(BF16), win_ref[...], preferred_element_type=F32)
        cos = cos_ref[rows, :]
        sin = sin_ref[rows, :]
        for jp in range(ATT_WIDTH // pair):
            tiles = _norm_rope(u[:, Q_OFF + jp * pair:Q_OFF + (jp + 1) * pair], mq_ref[...], qw_ref[...], cos, sin)
            for j, tile in enumerate(tiles):
                c0 = jp * pair + j * LANES
                q_ref[rows, c0:c0 + LANES] = tile.astype(BF16)
        tiles = _norm_rope(u[:, K_OFF:K_OFF + pair], mk_ref[...], kw_ref[...], cos, sin)
        for j, tile in enumerate(tiles):
            k_ref[rows, j * LANES:(j + 1) * LANES] = tile.astype(BF16)
        v_ref[rows, :] = u[:, V_OFF:Z_OFF].astype(BF16)
        z_ref[rows, :] = u[:, Z_OFF:XBC_OFF]
        xbc_ref[rows, :] = u[:, XBC_OFF:DT_OFF]
        dt_ref[rows, :] = u[:, DT_OFF:IN_COLS]

    late_steps = [functools.partial(project, r0) for r0 in range(0, pre_ref.shape[0], NORM_SUB)]

    @pl.when(i == 0)
    def _():
        pre_ref[...] = jnp.zeros_like(pre_ref)

    @pl.when(i < last)
    def _():
        x = x_ref[...]
        f = _swiglu(x.astype(BF16), wgu_ref, wo_ref, FF_CHUNKS_FRONT, before=late_steps)
        pre_ref[...] = ALPHA * x + 0.5 * f

    @pl.when(i == last)
    def _():
        for step in late_steps:
            step()


def _front_call(x2d, wgu, wo, g, b, wcat, cos_t, sin_t, qw, kw, mq, mk):
    T = x2d.shape[0]
    tm = TOKEN_TILE
    n_tiles = T // tm
    pos_blocks = SEQ // tm
    cur = lambda i: (jnp.minimum(i, n_tiles - 1), 0)
    late = lambda i: (jnp.maximum(i - 1, 0), 0)
    pos = lambda i: (jnp.maximum(i - 1, 0) % pos_blocks, 0)
    out_shape = (
        jax.ShapeDtypeStruct((T, D_MODEL), F32),
        jax.ShapeDtypeStruct((T, ATT_WIDTH), BF16),
        jax.ShapeDtypeStruct((T, N_KV_HEADS * LANES), BF16),
        jax.ShapeDtypeStruct((T, N_KV_HEADS * LANES), BF16),
        jax.ShapeDtypeStruct((T, SSD_WIDTH), F32),
        jax.ShapeDtypeStruct((T, CONV_CH), F32),
        jax.ShapeDtypeStruct((T, DT_PAD), F32),
    )
    in_specs = [
        pl.BlockSpec((tm, D_MODEL), cur),
        _const_spec(wgu.shape), _const_spec(wo.shape),
        _const_spec(g.shape), _const_spec(b.shape), _const_spec(wcat.shape),
        pl.BlockSpec((tm, LANES), pos), pl.BlockSpec((tm, LANES), pos),
        _const_spec(qw.shape), _const_spec(kw.shape), _const_spec(mq.shape), _const_spec(mk.shape),
    ]
    out_specs = tuple(pl.BlockSpec((tm, s.shape[1]), late) for s in out_shape)
    return pl.pallas_call(
        _front_kernel, grid=(n_tiles + 1,), in_specs=in_specs, out_specs=out_specs, out_shape=out_shape,
        scratch_shapes=[pltpu.VMEM((tm, D_MODEL), F32)],
        name="front_ffn_inproj",
        compiler_params=pltpu.CompilerParams(dimension_semantics=("arbitrary",),
                                             vmem_limit_bytes=VMEM_LIMIT),
    )(x2d, wgu, wo, g, b, wcat, cos_t, sin_t, qw, kw, mq, mk)


def _attn_kernel(q_ref, k_ref, v_ref, o_ref):
    tq = q_ref.shape[0]
    lane = lax.broadcasted_iota(jnp.int32, (1, LANES), 1)
    low = lane < HALF_LANES
    q_per_kv_tiles = (N_Q_HEADS // N_KV_HEADS) * HEAD_DIM // LANES
    for kv in range(N_KV_HEADS):
        kt = k_ref[:, kv * LANES:(kv + 1) * LANES]
        vt = v_ref[:, kv * LANES:(kv + 1) * LANES]
        zero = jnp.zeros_like(vt)
        v_lo = jnp.where(low, vt, zero)
        v_hi = jnp.where(low, zero, vt)
        blk = (lane // (HEAD_DIM // 2)) % 2
        for jj in range(q_per_kv_tiles):
            j = kv * q_per_kv_tiles + jj
            cols = slice(j * LANES, (j + 1) * LANES)
            groups = [(r0, Q_UNIT) for r0 in range(0, tq, Q_UNIT)]
            if j == 0:
                groups = [(0, Q_UNIT // 2), (Q_UNIT // 2, Q_UNIT // 2)] + groups[1:]
            if j == ATT_WIDTH // LANES - 1:
                r_last = groups[-1][0]
                groups = groups[:-1] + [(r_last, Q_UNIT // 2), (r_last + Q_UNIT // 2, Q_UNIT // 2)]
            for r0, n_rows in groups:
                rows = slice(r0, r0 + n_rows)
                qt = q_ref[rows, cols]
                zq = jnp.zeros_like(qt)
                out = None
                for a, v_a in ((0, v_lo), (1, v_hi)):
                    qa = jnp.where(blk == a, qt, zq)
                    s = lax.dot_general(qa, kt, (((1,), (1,)), ((), ())), preferred_element_type=F32)
                    m = jnp.max(s, axis=-1, keepdims=True)
                    p = jnp.exp2(s - m)
                    l = jnp.sum(p, axis=-1, keepdims=True)
                    o = jnp.dot(p.astype(BF16), v_a, preferred_element_type=F32) * (1.0 / l)
                    out = o if out is None else out + o
                o_ref[rows, cols] = out.astype(BF16)


def _attn_call(q, k, v, batch):
    T = q.shape[0]
    nq = SEQ // Q_TILE
    return pl.pallas_call(
        _attn_kernel, grid=(batch, nq),
        in_specs=[pl.BlockSpec((Q_TILE, ATT_WIDTH), lambda b, i: (b * nq + i, 0)),
                  pl.BlockSpec((SEQ, N_KV_HEADS * LANES), lambda b, i: (b, 0)),
                  pl.BlockSpec((SEQ, N_KV_HEADS * LANES), lambda b, i: (b, 0))],
        out_specs=pl.BlockSpec((Q_TILE, ATT_WIDTH), lambda b, i: (b * nq + i, 0)),
        out_shape=jax.ShapeDtypeStruct((T, ATT_WIDTH), BF16),
        name="attention",
        compiler_params=pltpu.CompilerParams(dimension_semantics=("arbitrary", "arbitrary"),
                                             vmem_limit_bytes=ATTN_VMEM_LIMIT),
    )(q, k, v)


def _softplus(x):
    return jnp.maximum(x, 0.0) + jnp.log1p(jnp.exp(-jnp.abs(x)))


def _dot3(a_pieces, b, dims):
    out = None
    for piece in a_pieces:
        lhs, rhs = (piece, b) if dims == "piece_lhs" else (b, piece)
        contract = (((1,), (0,)), ((), ())) if dims == "piece_lhs" else (((1,), (1,)), ((), ()))
        term = lax.dot_general(lhs, rhs, contract, preferred_element_type=F32)
        out = term if out is None else out + term
    return out


def _conv_tile(xin, w_ref, b_ref, cols, pad_lo, pad_hi):
    R = xin.shape[0]
    half = CONV_WIDTH // 2
    t_idx = lax.broadcasted_iota(jnp.int32, (R, 1), 0)
    acc = b_ref[:, cols] + w_ref[half:half + 1, cols] * xin
    for j in range(CONV_WIDTH):
        off = j - half
        if off == 0:
            continue
        rolled = pltpu.roll(xin, (-off) % R, axis=0)
        if off < 0 and pad_lo:
            rolled = jnp.where(t_idx >= -off, rolled, 0.0)
        if off > 0 and pad_hi:
            rolled = jnp.where(t_idx < R - off, rolled, 0.0)
        acc = acc + w_ref[j:j + 1, cols] * rolled
    return acc * _sigmoid(acc)


def _ssd_kernel(xbc_ref, z_ref, dt_ref, cw_ref, cb_ref, dtb_ref, alog_ref, dsk_ref, nw_ref, tri_ref,
                o_ref, xc_ref, ylo_ref, yhi_ref, st_ref, bt_ref, dtt_ref, cumrow_ref, cumcol_ref, gm_ref):
    y_refs = (ylo_ref, yhi_ref)
    S = xbc_ref.shape[0]
    n_chunks = S // CHUNK
    pair_tiles = SSD_WIDTH // LANES
    tiles_per_group = pair_tiles // SSD_GROUPS
    n_heads_all = N_DIRS * SSD_HEADS
    b_off = SSD_WIDTH
    c_off = SSD_WIDTH + SSD_GROUPS * D_STATE
    lane = lax.broadcasted_iota(jnp.int32, (1, LANES), 1)
    low = lane < HALF_LANES

    edge = 2 * SUBLANES
    for c in range(CONV_CH // LANES):
        cols = slice(c * LANES, (c + 1) * LANES)
        xc_ref[:, cols] = _conv_tile(xbc_ref[:, cols], cw_ref, cb_ref, cols, False, False)
        xc_ref[0:edge // 2, cols] = _conv_tile(xbc_ref[0:edge, cols], cw_ref, cb_ref, cols, True, False)[0:edge // 2]
        xc_ref[S - edge // 2:S, cols] = _conv_tile(xbc_ref[S - edge:S, cols], cw_ref, cb_ref, cols,
                                                   False, True)[edge // 2:edge]

    for y_half in y_refs:
        y_half[...] = jnp.zeros_like(y_half)
    st_ref[...] = jnp.zeros_like(st_ref)

    a_col = -jnp.exp(alog_ref[...])
    dtb_col = dtb_ref[...]
    r_i = lax.broadcasted_iota(jnp.int32, (CHUNK, CHUNK), 0)
    c_i = lax.broadcasted_iota(jnp.int32, (CHUNK, CHUNK), 1)
    masks = (c_i <= r_i, c_i >= r_i)
    zero_rows = jnp.zeros((CHUNK - n_heads_all, CHUNK), BF16)
    fwd_rows = lax.broadcasted_iota(jnp.int32, (n_heads_all, 1), 0) < SSD_HEADS
    fwd_lanes = lane < SSD_HEADS

    def prep_chunk(c):
        rows = pl.ds(pl.multiple_of(c * CHUNK, CHUNK), CHUNK)
        dt_all = _softplus(dt_ref[rows, :].T[0:n_heads_all, :] + dtb_col)
        dtt_ref[c] = dt_all
        pieces = _split3(dt_all * a_col)
        padded = [jnp.concatenate([pc, zero_rows], axis=0) for pc in pieces]
        cumrow_ref[c] = jnp.where(fwd_rows, _dot3(pieces, tri_ref[1], "piece_lhs"),
                                  _dot3(pieces, tri_ref[0], "piece_lhs"))
        cumcol_ref[c] = jnp.where(fwd_lanes, _dot3(padded, tri_ref[0], "piece_rhs"),
                                  _dot3(padded, tri_ref[1], "piece_rhs"))
        for g in range(SSD_GROUPS):
            bt = xc_ref[rows, b_off + g * D_STATE:b_off + (g + 1) * D_STATE].T
            bt_ref[c, g * D_STATE:(g + 1) * D_STATE, :] = bt
            c_b = xc_ref[rows, c_off + g * D_STATE:c_off + (g + 1) * D_STATE].astype(BF16)
            gmat = jnp.dot(c_b, bt.astype(BF16), preferred_element_type=F32)
            for d in range(N_DIRS):
                k = d * SSD_GROUPS + g
                gm_ref[c, k * CHUNK:(k + 1) * CHUNK, :] = jnp.where(masks[d], gmat, 0.0).astype(BF16)

    prep_unroll = 4

    def prep(i, carry):
        for u in range(prep_unroll):
            prep_chunk(i * prep_unroll + u)
        return carry

    lax.fori_loop(0, n_chunks // prep_unroll, prep, 0)

    def one_chunk(c, d, y_half, c_local):
        rows = pl.ds(pl.multiple_of(c * CHUNK, CHUNK), CHUNK)
        rows_local = pl.ds(pl.multiple_of(c_local * CHUNK, CHUNK), CHUNK)
        hs = slice(d * SSD_HEADS, (d + 1) * SSD_HEADS)
        dt_d = dtt_ref[c][hs]
        cum_row = cumrow_ref[c][hs]
        cum_col = cumcol_ref[c]
        tot = jnp.sum(dt_d * a_col[hs], axis=1, keepdims=True)
        w_row = dt_d * jnp.exp(tot - cum_row)
        for g in range(SSD_GROUPS):
            c_b = xc_ref[rows, c_off + g * D_STATE:c_off + (g + 1) * D_STATE].astype(BF16)
            bt_g = bt_ref[c, g * D_STATE:(g + 1) * D_STATE, :]
            k = d * SSD_GROUPS + g
            gm_b = gm_ref[c, k * CHUNK:(k + 1) * CHUNK, :]
            for tt in range(tiles_per_group):
                t = g * tiles_per_group + tt
                cols = slice(t * LANES, (t + 1) * LANES)
                x_t = xc_ref[rows, cols].astype(BF16)
                st = st_ref[d, :, cols]
                rhs = jnp.concatenate([x_t, st.astype(BF16)], axis=0)
                zero_x = jnp.zeros_like(x_t)
                x_by_head = jnp.concatenate([jnp.where(low, x_t, zero_x), jnp.where(low, zero_x, x_t)], axis=0)
                ys, bts = [], []
                for hh in (2 * t, 2 * t + 1):
                    col = d * SSD_HEADS + hh
                    colb = jnp.broadcast_to(cum_col[:, col:col + 1], (CHUNK, CHUNK))
                    seg = jnp.minimum(colb - cum_row[hh:hh + 1, :], 0.0)
                    w = (jnp.exp(seg) * dt_d[hh:hh + 1, :]).astype(BF16) * gm_b
                    a = jnp.exp(colb).astype(BF16) * c_b
                    lhs = jnp.concatenate([w, a], axis=1)
                    ys.append(jnp.dot(lhs, rhs, preferred_element_type=F32))
                    bts.append((bt_g * w_row[hh:hh + 1, :]).astype(BF16))
                y_half[rows_local, cols] = y_half[rows_local, cols] + jnp.where(low, ys[0], ys[1])
                decay = jnp.exp(jnp.where(low, tot[2 * t:2 * t + 1, :], tot[2 * t + 1:2 * t + 2, :]))
                st_ref[d, :, cols] = st * decay + jnp.dot(jnp.concatenate(bts, axis=1), x_by_head,
                                                          preferred_element_type=F32)

    half_chunks = n_chunks // 2

    def body_first(c, carry):
        one_chunk(c, 0, y_refs[0], c)
        one_chunk(n_chunks - 1 - c, 1, y_refs[1], half_chunks - 1 - c)
        return carry

    def body_second(c, carry):
        one_chunk(c, 0, y_refs[1], c - half_chunks)
        one_chunk(n_chunks - 1 - c, 1, y_refs[0], n_chunks - 1 - c)
        return carry

    lax.fori_loop(0, half_chunks, body_first, 0)
    lax.fori_loop(half_chunks, n_chunks, body_second, 0)

    group_w = SSD_WIDTH // SSD_GROUPS
    fin_unroll = 2
    for half, y_half in enumerate(y_refs):
        def fin(i, carry, half=half, y_half=y_half):
            for u in range(fin_unroll):
                c = i * fin_unroll + u
                rows = pl.ds(pl.multiple_of((half * half_chunks + c) * CHUNK, CHUNK), CHUNK)
                rows_local = pl.ds(pl.multiple_of(c * CHUNK, CHUNK), CHUNK)
                zz = z_ref[rows, :]
                y = (y_half[rows_local, :] + dsk_ref[...] * xc_ref[rows, 0:SSD_WIDTH]) * (zz * _sigmoid(zz))
                outs = []
                for g in range(SSD_GROUPS):
                    yg = y[:, g * group_w:(g + 1) * group_w]
                    ms = jnp.mean(yg * yg, axis=-1, keepdims=True)
                    outs.append(yg * lax.rsqrt(ms + RMS_EPS))
                o_ref[rows, :] = (jnp.concatenate(outs, axis=-1) * nw_ref[...]).astype(BF16)
            return carry

        lax.fori_loop(0, half_chunks // fin_unroll, fin, 0)


def _ssd_call(xbc, z, dt, cw, cb, dtb, alog, dsk, nw, tri, batch):
    T = xbc.shape[0]
    n_chunks = SEQ // CHUNK
    seq = lambda b: (b, 0)
    return pl.pallas_call(
        _ssd_kernel, grid=(batch,),
        in_specs=[pl.BlockSpec((SEQ, CONV_CH), seq), pl.BlockSpec((SEQ, SSD_WIDTH), seq),
                  pl.BlockSpec((SEQ, DT_PAD), seq),
                  _const_spec(cw.shape), _const_spec(cb.shape), _const_spec(dtb.shape),
                  _const_spec(alog.shape), _const_spec(dsk.shape), _const_spec(nw.shape),
                  _const_spec(tri.shape)],
        out_specs=pl.BlockSpec((SEQ, SSD_WIDTH), seq),
        out_shape=jax.ShapeDtypeStruct((T, SSD_WIDTH), BF16),
        scratch_shapes=[pltpu.VMEM((SEQ, CONV_CH), F32),
                        pltpu.VMEM((SEQ // 2, SSD_WIDTH), F32),
                        pltpu.VMEM((SEQ // 2, SSD_WIDTH), F32),
                        pltpu.VMEM((N_DIRS, D_STATE, SSD_WIDTH), F32),
                        pltpu.VMEM((n_chunks, SSD_GROUPS * D_STATE, CHUNK), F32),
                        pltpu.VMEM((n_chunks, N_DIRS * SSD_HEADS, CHUNK), F32),
                        pltpu.VMEM((n_chunks, N_DIRS * SSD_HEADS, CHUNK), F32),
                        pltpu.VMEM((n_chunks, CHUNK, LANES), F32),
                        pltpu.VMEM((n_chunks, N_DIRS * SSD_GROUPS * CHUNK, CHUNK), BF16)],
        name="ssd_bidir",
        compiler_params=pltpu.CompilerParams(dimension_semantics=("arbitrary",),
                                             vmem_limit_bytes=VMEM_LIMIT),
    )(xbc, z, dt, cw, cb, dtb, alog, dsk, nw, tri)


def _back_kernel(x1_ref, att_ref, ssd_ref, p_ref, wout_ref, wgu_ref, wo_ref,
                 lng_ref, lnb_ref, wp_ref, wgate_ref, bgate_ref, o_ref, pre_ref):
    i = pl.program_id(0)
    last = pl.num_programs(0) - 1
    subs = [slice(r0, r0 + NORM_SUB) for r0 in range(0, x1_ref.shape[0], NORM_SUB)]

    def embed(rows):
        x3 = _layer_norm(pre_ref[rows, :], lng_ref[1:2, :], lnb_ref[1:2, :])
        e = jnp.dot(p_ref[rows, :].astype(BF16), wp_ref[...], preferred_element_type=F32)
        gate = _sigmoid(jnp.dot(x3.astype(BF16), wgate_ref[...], preferred_element_type=F32) + bgate_ref[...])
        o_ref[rows, :] = _layer_norm(ALPHA * x3 + gate * e, lng_ref[2:3, :], lnb_ref[2:3, :])

    late_steps = [functools.partial(embed, rows) for rows in subs]

    def ffn_stage():
        x2_parts = []
        for rows in subs:
            mix = (jnp.dot(att_ref[rows, :], wout_ref[0:ATT_WIDTH, :], preferred_element_type=F32)
                   + jnp.dot(ssd_ref[rows, :], wout_ref[ATT_WIDTH:, :], preferred_element_type=F32))
            x2_parts.append(_layer_norm(ALPHA * x1_ref[rows, :] + mix, lng_ref[0:1, :], lnb_ref[0:1, :]))
        x2 = jnp.concatenate(x2_parts, axis=0)
        f = _swiglu(x2.astype(BF16), wgu_ref, wo_ref, FF_CHUNKS_BACK, before=late_steps)
        pre_ref[...] = ALPHA * x2 + 0.5 * f

    @pl.when(i == 0)
    def _():
        pre_ref[...] = jnp.zeros_like(pre_ref)

    @pl.when(i < last)
    def _():
        ffn_stage()

    @pl.when(i == last)
    def _():
        for step in late_steps:
            step()


def _back_call(x1, att, ssd, p2d, wout, wgu, wo, lng, lnb, wp, wgate, bgate):
    T = x1.shape[0]
    tm = TOKEN_TILE
    n_tiles = T // tm
    cur = lambda i: (jnp.minimum(i, n_tiles - 1), 0)
    late = lambda i: (jnp.maximum(i - 1, 0), 0)
    consts = (wout, wgu, wo, lng, lnb, wp, wgate, bgate)
    return pl.pallas_call(
        _back_kernel, grid=(n_tiles + 1,),
        in_specs=[pl.BlockSpec((tm, D_MODEL), cur), pl.BlockSpec((tm, ATT_WIDTH), cur),
                  pl.BlockSpec((tm, SSD_WIDTH), cur), pl.BlockSpec((tm, PLE_DIM), late)]
                 + [_const_spec(c.shape) for c in consts],
        out_specs=pl.BlockSpec((tm, D_MODEL), late),
        out_shape=jax.ShapeDtypeStruct((T, D_MODEL), F32),
        scratch_shapes=[pltpu.VMEM((tm, D_MODEL), F32)],
        name="back_outproj_ffn_ple",
        compiler_params=pltpu.CompilerParams(dimension_semantics=("arbitrary",),
                                             vmem_limit_bytes=VMEM_LIMIT),
    )(x1, att, ssd, p2d, *consts)


def _in_proj_weight(w):
    half_blk = HEAD_DIM // 2
    n_q = N_Q_HEADS * HEAD_DIM
    n_kv = N_KV_HEADS * HEAD_DIM
    n_dt = N_DIRS * SSD_HEADS
    cols = np.arange(w.shape[1])
    q = cols[:n_q].reshape(N_Q_HEADS // 2, 2, half_blk, 2).transpose(0, 3, 1, 2).reshape(-1)
    k = cols[n_q:n_q + n_kv].reshape(N_KV_HEADS, half_blk, 2).transpose(0, 2, 1)
    k = np.broadcast_to(k[:, :, None, :], (N_KV_HEADS, 2, 2, half_blk)).reshape(-1)
    v = np.broadcast_to(cols[n_q + n_kv:n_q + 2 * n_kv].reshape(N_KV_HEADS, 1, HEAD_DIM),
                        (N_KV_HEADS, 2, HEAD_DIM)).reshape(-1)
    rest = cols[n_q + 2 * n_kv:]
    pad = np.tile(cols[-n_dt:], (DT_PAD - n_dt) // n_dt)
    index = np.concatenate([q, k, v, rest, pad]).astype(np.int32)
    return w.astype(BF16)[:, index]


def _rope_tiles():
    rows = SEQ // GRID_W
    row = jnp.repeat(jnp.arange(rows, dtype=F32), GRID_W)
    col = jnp.tile(jnp.arange(GRID_W, dtype=F32), rows)
    inv = ROPE_THETA ** (-jnp.arange(0, ROPE_AXIS_DIM, 2, dtype=F32) / ROPE_AXIS_DIM)
    ang = jnp.concatenate([row[:, None] * inv, col[:, None] * inv], axis=-1)
    cos, sin = jnp.cos(ang), jnp.sin(ang)
    cos_t = jnp.concatenate([cos, cos, cos, cos], axis=-1)
    sin_t = jnp.concatenate([-sin, -sin, sin, sin], axis=-1)
    return cos_t, sin_t


def _norm_tile(w):
    lanes = np.arange(LANES)
    idx = 2 * (lanes % (HEAD_DIM // 2)) + lanes // HALF_LANES
    return w[idx][None, :]


def kernel(x, p, ln_g, ln_b, ffn1_w_in, ffn1_w_out, w_in, q_norm, k_norm, conv_w, conv_b, dt_bias,
           a_log, d_skip, ssd_norm, w_out, ffn2_w_in, ffn2_w_out, ple_w, ple_gate_w, ple_gate_b):
    B, S, D = x.shape
    assert (S, D) == (SEQ, D_MODEL) and ln_g.shape[0] == DEPTH == 1
    T = B * S
    x2d = x.reshape(T, D)
    p2d = p[0].reshape(T, PLE_DIM)

    cos_t, sin_t = _rope_tiles()
    lanes = np.arange(LANES)
    same_head = ((lanes[:, None] // (HEAD_DIM // 2)) % 2) == ((lanes[None, :] // (HEAD_DIM // 2)) % 2)
    two_tiles = np.eye(2, dtype=np.float32)
    mq = jnp.asarray(np.kron(two_tiles, np.where(same_head, 1.0 / HEAD_DIM, 0.0)), BF16)
    mk = jnp.asarray(np.kron(two_tiles, np.full((LANES, LANES), 1.0 / LANES)), BF16)
    ones = np.ones((CHUNK, CHUNK), np.float32)
    tri = jnp.asarray(np.stack([np.tril(ones), np.triu(ones)]), BF16)

    i = 0
    wcat = _in_proj_weight(w_in[i])
    qw = _norm_tile(q_norm[i]) * (HEAD_DIM ** -0.5 * LOG2E)
    kw = _norm_tile(k_norm[i])
    x1, qh, kh, vh, z, xbc, dtr = _front_call(
        x2d, ffn1_w_in[i].astype(BF16), ffn1_w_out[i].astype(BF16), ln_g[i, 0:1], ln_b[i, 0:1],
        wcat, cos_t, sin_t, qw, kw, mq, mk)

    att = _attn_call(qh, kh, vh, B)

    dsk = jnp.repeat(d_skip[i], SSD_HEAD_DIM)[None, :]
    ssd = _ssd_call(xbc, z, dtr, conv_w[i], conv_b[i][None, :], dt_bias[i].reshape(-1, 1), a_log[i].reshape(-1, 1),
                    dsk, ssd_norm[i][None, :], tri, B)

    out = _back_call(x1, att, ssd, p2d, w_out[i].astype(BF16),
                     ffn2_w_in[i].astype(BF16), ffn2_w_out[i].astype(BF16), ln_g[i, 1:4], ln_b[i, 1:4],
                     ple_w[i].astype(BF16), ple_gate_w[i].astype(BF16), ple_gate_b[i][None, :])
    return out.reshape(B, S, D)
```

```python
import functools
import math

import numpy as np
import jax
import jax.numpy as jnp
from jax import lax
from jax.experimental import pallas as pl
from jax.experimental.pallas import tpu as pltpu

F32 = jnp.float32
BF16 = jnp.bfloat16

D_MODEL = 1024
SEQ = 2048
DEPTH = 1
HEAD_DIM = 64
N_Q_HEADS = 8
N_KV_HEADS = 2
ROPE_AXIS_DIM = HEAD_DIM // 2
ROPE_THETA = 10000.0
GRID_W = 64
ATT_WIDTH = N_Q_HEADS * HEAD_DIM
SSD_WIDTH = 512
SSD_HEAD_DIM = 64
SSD_HEADS = 8
SSD_GROUPS = 2
D_STATE = 128
CONV_WIDTH = 5
CONV_CH = SSD_WIDTH + 2 * SSD_GROUPS * D_STATE
CHUNK = 128
N_DIRS = 2
D_FF = 2816
PLE_DIM = 256
ALPHA = (2.0 * DEPTH) ** 0.25
LN_EPS = 1e-5
RMS_EPS = 1e-6
LOG2E = 1.4426950408889634

LANES = 128
HALF_LANES = LANES // 2
SUBLANES = 8
TOKEN_TILE = 512
NORM_SUB = 256
Q_TILE = 1024
Q_UNIT = 512
FF_CHUNKS_FRONT = ((0, 512), (512, 512), (1024, 512), (1536, 512), (2048, 768))
FF_CHUNKS_BACK = ((0, 768), (768, 768), (1536, 768), (2304, 512))
DT_PAD = LANES
Q_OFF = 0
K_OFF = Q_OFF + ATT_WIDTH
V_OFF = K_OFF + N_KV_HEADS * LANES
Z_OFF = V_OFF + N_KV_HEADS * LANES
XBC_OFF = Z_OFF + SSD_WIDTH
DT_OFF = XBC_OFF + CONV_CH
IN_COLS = DT_OFF + DT_PAD
VMEM_LIMIT = 56 * 1024 * 1024


def _const_spec(shape):
    nd = len(shape)
    return pl.BlockSpec(shape, lambda *_: (0,) * nd, pipeline_mode=pl.Buffered(1))


def _layer_norm(y, g, b):
    mu = jnp.mean(y, axis=-1, keepdims=True)
    yc = y - mu
    var = jnp.mean(yc * yc, axis=-1, keepdims=True)
    return yc * lax.rsqrt(var + LN_EPS) * g + b


def _sigmoid(x):
    return 1.0 / (1.0 + jnp.exp(-x))


def _swiglu(xb, wgu_ref, wo_ref, chunks, before=()):
    acc = None
    for k, (c0, cw) in enumerate(chunks):
        if k < len(before):
            before[k]()
        hg = jnp.dot(xb, wgu_ref[:, c0:c0 + cw], preferred_element_type=F32)
        hu = jnp.dot(xb, wgu_ref[:, D_FF + c0:D_FF + c0 + cw], preferred_element_type=F32)
        act = (hg * _sigmoid(hg) * hu).astype(BF16)
        part = jnp.dot(act, wo_ref[c0:c0 + cw, :], preferred_element_type=F32)
        acc = part if acc is None else acc + part
    return acc


def _split3(x):
    hi = x.astype(BF16)
    r1 = x - hi.astype(F32)
    mid = r1.astype(BF16)
    lo = (r1 - mid.astype(F32)).astype(BF16)
    return hi, mid, lo


def _norm_rope(t2, seg_mean, w, cos, sin):
    ms = jnp.dot((t2 * t2).astype(BF16), seg_mean, preferred_element_type=F32)
    tn2 = t2 * lax.rsqrt(ms + RMS_EPS)
    outs = []
    for j in range(2):
        tn = tn2[:, j * LANES:(j + 1) * LANES] * w
        outs.append(tn * cos + pltpu.roll(tn, HALF_LANES, axis=1) * sin)
    return outs


def _front_kernel(x_ref, wgu_ref, wo_ref, g_ref, b_ref, win_ref,
                  cos_ref, sin_ref, qw_ref, kw_ref, mq_ref, mk_ref,
                  x1_ref, q_ref, k_ref, v_ref, z_ref, xbc_ref, dt_ref, pre_ref):
    i = pl.program_id(0)
    last = pl.num_programs(0) - 1

    def project(r0):
        rows = slice(r0, r0 + NORM_SUB)
        pair = 2 * LANES
        x1 = _layer_norm(pre_ref[rows, :], g_ref[...], b_ref[...])
        x1_ref[rows, :] = x1
        u = jnp.dot(x1.astype(BF16), win_ref[...], preferred_element_type=F32)
        cos = cos_ref[rows, :]
        sin = sin_ref[rows, :]
        for jp in range(ATT_WIDTH // pair):
            tiles = _norm_rope(u[:, Q_OFF + jp * pair:Q_OFF + (jp + 1) * pair], mq_ref[...], qw_ref[...], cos, sin)
            for j, tile in enumerate(tiles):
                c0 = jp * pair + j * LANES
                q_ref[rows, c0:c0 + LANES] = tile.astype(BF16)
        tiles = _norm_rope(u[:, K_OFF:K_OFF + pair], mk_ref[...], kw_ref[...], cos, sin)
        for j, tile in enumerate(tiles):
            k_ref[rows, j * LANES:(j + 1) * LANES] = tile.astype(BF16)
        v_ref[rows, :] = u[:, V_OFF:Z_OFF].astype(BF16)
        z_ref[rows, :] = u[:, Z_OFF:XBC_OFF]
        xbc_ref[rows, :] = u[:, XBC_OFF:DT_OFF]
        dt_ref[rows, :] = u[:, DT_OFF:IN_COLS]

    late_steps = [functools.partial(project, r0) for r0 in range(0, pre_ref.shape[0], NORM_SUB)]

    @pl.when(i == 0)
    def _():
        pre_ref[...] = jnp.zeros_like(pre_ref)

    @pl.when(i < last)
    def _():
        x = x_ref[...]
        f = _swiglu(x.astype(BF16), wgu_ref, wo_ref, FF_CHUNKS_FRONT, before=late_steps)
        pre_ref[...] = ALPHA * x + 0.5 * f

    @pl.when(i == last)
    def _():
        for step in late_steps:
            step()


def _front_call(x2d, wgu, wo, g, b, wcat, cos_t, sin_t, qw, kw, mq, mk):
    T = x2d.shape[0]
    tm = TOKEN_TILE
    n_tiles = T // tm
    pos_blocks = SEQ // tm
    cur = lambda i: (jnp.minimum(i, n_tiles - 1), 0)
    late = lambda i: (jnp.maximum(i - 1, 0), 0)
    pos = lambda i: (jnp.maximum(i - 1, 0) % pos_blocks, 0)
    out_shape = (
        jax.ShapeDtypeStruct((T, D_MODEL), F32),
        jax.ShapeDtypeStruct((T, ATT_WIDTH), BF16),
        jax.ShapeDtypeStruct((T, N_KV_HEADS * LANES), BF16),
        jax.ShapeDtypeStruct((T, N_KV_HEADS * LANES), BF16),
        jax.ShapeDtypeStruct((T, SSD_WIDTH), F32),
        jax.ShapeDtypeStruct((T, CONV_CH), F32),
        jax.ShapeDtypeStruct((T, DT_PAD), F32),
    )
    in_specs = [
        pl.BlockSpec((tm, D_MODEL), cur),
        _const_spec(wgu.shape), _const_spec(wo.shape),
        _const_spec(g.shape), _const_spec(b.shape), _const_spec(wcat.shape),
        pl.BlockSpec((tm, LANES), pos), pl.BlockSpec((tm, LANES), pos),
        _const_spec(qw.shape), _const_spec(kw.shape), _const_spec(mq.shape), _const_spec(mk.shape),
    ]
    out_specs = tuple(pl.BlockSpec((tm, s.shape[1]), late) for s in out_shape)
    return pl.pallas_call(
        _front_kernel, grid=(n_tiles + 1,), in_specs=in_specs, out_specs=out_specs, out_shape=out_shape,
        scratch_shapes=[pltpu.VMEM((tm, D_MODEL), F32)],
        name="front_ffn_inproj",
        compiler_params=pltpu.CompilerParams(dimension_semantics=("arbitrary",),
                                             vmem_limit_bytes=VMEM_LIMIT),
    )(x2d, wgu, wo, g, b, wcat, cos_t, sin_t, qw, kw, mq, mk)


def _attn_kernel(q_ref, k_ref, v_ref, o_ref):
    tq = q_ref.shape[0]
    lane = lax.broadcasted_iota(jnp.int32, (1, LANES), 1)
    low = lane < HALF_LANES
    q_per_kv_tiles = (N_Q_HEADS // N_KV_HEADS) * HEAD_DIM // LANES
    for kv in range(N_KV_HEADS):
        kt = k_ref[:, kv * LANES:(kv + 1) * LANES]
        vt = v_ref[:, kv * LANES:(kv + 1) * LANES]
        zero = jnp.zeros_like(vt)
        v_lo = jnp.where(low, vt, zero)
        v_hi = jnp.where(low, zero, vt)
        blk = (lane // (HEAD_DIM // 2)) % 2
        for jj in range(q_per_kv_tiles):
            j = kv * q_per_kv_tiles + jj
            cols = slice(j * LANES, (j + 1) * LANES)
            for r0 in range(0, tq, Q_UNIT):
                rows = slice(r0, r0 + Q_UNIT)
                qt = q_ref[rows, cols]
                zq = jnp.zeros_like(qt)
                out = None
                for a, v_a in ((0, v_lo), (1, v_hi)):
                    qa = jnp.where(blk == a, qt, zq)
                    s = lax.dot_general(qa, kt, (((1,), (1,)), ((), ())), preferred_element_type=F32)
                    m = jnp.max(s, axis=-1, keepdims=True)
                    p = jnp.exp2(s - m)
                    l = jnp.sum(p, axis=-1, keepdims=True)
                    o = jnp.dot(p.astype(BF16), v_a, preferred_element_type=F32) * (1.0 / l)
                    out = o if out is None else out + o
                o_ref[rows, cols] = out.astype(BF16)


def _attn_call(q, k, v, batch):
    T = q.shape[0]
    nq = SEQ // Q_TILE
    return pl.pallas_call(
        _attn_kernel, grid=(batch, nq),
        in_specs=[pl.BlockSpec((Q_TILE, ATT_WIDTH), lambda b, i: (b * nq + i, 0)),
                  pl.BlockSpec((SEQ, N_KV_HEADS * LANES), lambda b, i: (b, 0)),
                  pl.BlockSpec((SEQ, N_KV_HEADS * LANES), lambda b, i: (b, 0))],
        out_specs=pl.BlockSpec((Q_TILE, ATT_WIDTH), lambda b, i: (b * nq + i, 0)),
        out_shape=jax.ShapeDtypeStruct((T, ATT_WIDTH), BF16),
        name="attention",
        compiler_params=pltpu.CompilerParams(dimension_semantics=("arbitrary", "arbitrary"),
                                             vmem_limit_bytes=VMEM_LIMIT),
    )(q, k, v)


def _softplus(x):
    return jnp.maximum(x, 0.0) + jnp.log1p(jnp.exp(-jnp.abs(x)))


def _dot3(a_pieces, b, dims):
    out = None
    for piece in a_pieces:
        lhs, rhs = (piece, b) if dims == "piece_lhs" else (b, piece)
        contract = (((1,), (0,)), ((), ())) if dims == "piece_lhs" else (((1,), (1,)), ((), ()))
        term = lax.dot_general(lhs, rhs, contract, preferred_element_type=F32)
        out = term if out is None else out + term
    return out


def _conv_tile(xin, w_ref, b_ref, cols, pad_lo, pad_hi):
    R = xin.shape[0]
    half = CONV_WIDTH // 2
    t_idx = lax.broadcasted_iota(jnp.int32, (R, 1), 0)
    acc = b_ref[:, cols] + w_ref[half:half + 1, cols] * xin
    for j in range(CONV_WIDTH):
        off = j - half
        if off == 0:
            continue
        rolled = pltpu.roll(xin, (-off) % R, axis=0)
        if off < 0 and pad_lo:
            rolled = jnp.where(t_idx >= -off, rolled, 0.0)
        if off > 0 and pad_hi:
            rolled = jnp.where(t_idx < R - off, rolled, 0.0)
        acc = acc + w_ref[j:j + 1, cols] * rolled
    return acc * _sigmoid(acc)


def _ssd_kernel(xbc_ref, z_ref, dt_ref, cw_ref, cb_ref, dtb_ref, alog_ref, dsk_ref, nw_ref, tri_ref,
                o_ref, xc_ref, ylo_ref, yhi_ref, st_ref, bt_ref, dtt_ref, cumrow_ref, cumcol_ref, gm_ref):
    y_refs = (ylo_ref, yhi_ref)
    S = xbc_ref.shape[0]
    n_chunks = S // CHUNK
    pair_tiles = SSD_WIDTH // LANES
    tiles_per_group = pair_tiles // SSD_GROUPS
    n_heads_all = N_DIRS * SSD_HEADS
    b_off = SSD_WIDTH
    c_off = SSD_WIDTH + SSD_GROUPS * D_STATE
    lane = lax.broadcasted_iota(jnp.int32, (1, LANES), 1)
    low = lane < HALF_LANES

    edge = 2 * SUBLANES
    for c in range(CONV_CH // LANES):
        cols = slice(c * LANES, (c + 1) * LANES)
        xc_ref[:, cols] = _conv_tile(xbc_ref[:, cols], cw_ref, cb_ref, cols, False, False)
        xc_ref[0:edge // 2, cols] = _conv_tile(xbc_ref[0:edge, cols], cw_ref, cb_ref, cols, True, False)[0:edge // 2]
        xc_ref[S - edge // 2:S, cols] = _conv_tile(xbc_ref[S - edge:S, cols], cw_ref, cb_ref, cols,
                                                   False, True)[edge // 2:edge]

    st_ref[...] = jnp.zeros_like(st_ref)

    a_col = -jnp.exp(alog_ref[...])
    dtb_col = dtb_ref[...]
    r_i = lax.broadcasted_iota(jnp.int32, (CHUNK, CHUNK), 0)
    c_i = lax.broadcasted_iota(jnp.int32, (CHUNK, CHUNK), 1)
    masks = (c_i <= r_i, c_i >= r_i)
    zero_rows = jnp.zeros((CHUNK - n_heads_all, CHUNK), BF16)
    fwd_rows = lax.broadcasted_iota(jnp.int32, (n_heads_all, 1), 0) < SSD_HEADS
    fwd_lanes = lane < SSD_HEADS

    def prep_chunk(c):
        rows = pl.ds(pl.multiple_of(c * CHUNK, CHUNK), CHUNK)
        dt_all = _softplus(dt_ref[rows, :].T[0:n_heads_all, :] + dtb_col)
        dtt_ref[c] = dt_all
        pieces = _split3(dt_all * a_col)
        padded = [jnp.concatenate([pc, zero_rows], axis=0) for pc in pieces]
        cumrow_ref[c] = jnp.where(fwd_rows, _dot3(pieces, tri_ref[1], "piece_lhs"),
                                  _dot3(pieces, tri_ref[0], "piece_lhs"))
        cumcol_ref[c] = jnp.where(fwd_lanes, _dot3(padded, tri_ref[0], "piece_rhs"),
                                  _dot3(padded, tri_ref[1], "piece_rhs"))
        for g in range(SSD_GROUPS):
            bt = xc_ref[rows, b_off + g * D_STATE:b_off + (g + 1) * D_STATE].T
            bt_ref[c, g * D_STATE:(g + 1) * D_STATE, :] = bt
            c_b = xc_ref[rows, c_off + g * D_STATE:c_off + (g + 1) * D_STATE].astype(BF16)
            gmat = jnp.dot(c_b, bt.astype(BF16), preferred_element_type=F32)
            for d in range(N_DIRS):
                k = d * SSD_GROUPS + g
                gm_ref[c, k * CHUNK:(k + 1) * CHUNK, :] = jnp.where(masks[d], gmat, 0.0).astype(BF16)

    prep_unroll = 4

    def prep(i, carry):
        for u in range(prep_unroll):
            prep_chunk(i * prep_unroll + u)
        return carry

    lax.fori_loop(0, n_chunks // prep_unroll, prep, 0)

    def one_chunk(c, d, y_half, c_local, first_visit):
        rows = pl.ds(pl.multiple_of(c * CHUNK, CHUNK), CHUNK)
        rows_local = pl.ds(pl.multiple_of(c_local * CHUNK, CHUNK), CHUNK)
        hs = slice(d * SSD_HEADS, (d + 1) * SSD_HEADS)
        dt_d = dtt_ref[c][hs]
        cum_row = cumrow_ref[c][hs]
        cum_col = cumcol_ref[c]
        tot = jnp.sum(dt_d * a_col[hs], axis=1, keepdims=True)
        w_row = dt_d * jnp.exp(tot - cum_row)
        for g in range(SSD_GROUPS):
            c_b = xc_ref[rows, c_off + g * D_STATE:c_off + (g + 1) * D_STATE].astype(BF16)
            bt_g = bt_ref[c, g * D_STATE:(g + 1) * D_STATE, :]
            k = d * SSD_GROUPS + g
            gm_b = gm_ref[c, k * CHUNK:(k + 1) * CHUNK, :]
            for tt in range(tiles_per_group):
                t = g * tiles_per_group + tt
                cols = slice(t * LANES, (t + 1) * LANES)
                x_t = xc_ref[rows, cols].astype(BF16)
                st = st_ref[d, :, cols]
                rhs = jnp.concatenate([x_t, st.astype(BF16)], axis=0)
                zero_x = jnp.zeros_like(x_t)
                x_by_head = jnp.concatenate([jnp.where(low, x_t, zero_x), jnp.where(low, zero_x, x_t)], axis=0)
                ys, bts = [], []
                for hh in (2 * t, 2 * t + 1):
                    col = d * SSD_HEADS + hh
                    colb = jnp.broadcast_to(cum_col[:, col:col + 1], (CHUNK, CHUNK))
                    seg = jnp.minimum(colb - cum_row[hh:hh + 1, :], 0.0)
                    w = (jnp.exp(seg) * dt_d[hh:hh + 1, :]).astype(BF16) * gm_b
                    a = jnp.exp(colb).astype(BF16) * c_b
                    lhs = jnp.concatenate([w, a], axis=1)
                    ys.append(jnp.dot(lhs, rhs, preferred_element_type=F32))
                    bts.append((bt_g * w_row[hh:hh + 1, :]).astype(BF16))
                y_new = jnp.where(low, ys[0], ys[1])
                y_half[rows_local, cols] = y_new if first_visit else y_half[rows_local, cols] + y_new
                decay = jnp.exp(jnp.where(low, tot[2 * t:2 * t + 1, :], tot[2 * t + 1:2 * t + 2, :]))
                st_ref[d, :, cols] = st * decay + jnp.dot(jnp.concatenate(bts, axis=1), x_by_head,
                                                          preferred_element_type=F32)

    half_chunks = n_chunks // 2

    def body_first(c, carry):
        one_chunk(c, 0, y_refs[0], c, True)
        one_chunk(n_chunks - 1 - c, 1, y_refs[1], half_chunks - 1 - c, True)
        return carry

    def body_second(c, carry):
        one_chunk(c, 0, y_refs[1], c - half_chunks, False)
        one_chunk(n_chunks - 1 - c, 1, y_refs[0], n_chunks - 1 - c, False)
        return carry

    lax.fori_loop(0, half_chunks, body_first, 0)
    lax.fori_loop(half_chunks, n_chunks, body_second, 0)

    group_w = SSD_WIDTH // SSD_GROUPS
    fin_unroll = 2
    for half, y_half in enumerate(y_refs):
        def fin(i, carry, half=half, y_half=y_half):
            for u in range(fin_unroll):
                c = i * fin_unroll + u
                rows = pl.ds(pl.multiple_of((half * half_chunks + c) * CHUNK, CHUNK), CHUNK)
                rows_local = pl.ds(pl.multiple_of(c * CHUNK, CHUNK), CHUNK)
                zz = z_ref[rows, :]
                y = (y_half[rows_local, :] + dsk_ref[...] * xc_ref[rows, 0:SSD_WIDTH]) * (zz * _sigmoid(zz))
                outs = []
                for g in range(SSD_GROUPS):
                    yg = y[:, g * group_w:(g + 1) * group_w]
                    ms = jnp.mean(yg * yg, axis=-1, keepdims=True)
                    outs.append(yg * lax.rsqrt(ms + RMS_EPS))
                o_ref[rows, :] = (jnp.concatenate(outs, axis=-1) * nw_ref[...]).astype(BF16)
            return carry

        lax.fori_loop(0, half_chunks // fin_unroll, fin, 0)


def _ssd_call(xbc, z, dt, cw, cb, dtb, alog, dsk, nw, tri, batch):
    T = xbc.shape[0]
    n_chunks = SEQ // CHUNK
    seq = lambda b: (b, 0)
    return pl.pallas_call(
        _ssd_kernel, grid=(batch,),
        in_specs=[pl.BlockSpec((SEQ, CONV_CH), seq), pl.BlockSpec((SEQ, SSD_WIDTH), seq),
                  pl.BlockSpec((SEQ, DT_PAD), seq),
                  _const_spec(cw.shape), _const_spec(cb.shape), _const_spec(dtb.shape),
                  _const_spec(alog.shape), _const_spec(dsk.shape), _const_spec(nw.shape),
                  _const_spec(tri.shape)],
        out_specs=pl.BlockSpec((SEQ, SSD_WIDTH), seq),
        out_shape=jax.ShapeDtypeStruct((T, SSD_WIDTH), BF16),
        scratch_shapes=[pltpu.VMEM((SEQ, CONV_CH), F32),
                        pltpu.VMEM((SEQ // 2, SSD_WIDTH), F32),
                        pltpu.VMEM((SEQ // 2, SSD_WIDTH), F32),
                        pltpu.VMEM((N_DIRS, D_STATE, SSD_WIDTH), F32),
                        pltpu.VMEM((n_chunks, SSD_GROUPS * D_STATE, CHUNK), F32),
                        pltpu.VMEM((n_chunks, N_DIRS * SSD_HEADS, CHUNK), F32),
                        pltpu.VMEM((n_chunks, N_DIRS * SSD_HEADS, CHUNK), F32),
                        pltpu.VMEM((n_chunks, CHUNK, LANES), F32),
                        pltpu.VMEM((n_chunks, N_DIRS * SSD_GROUPS * CHUNK, CHUNK), BF16)],
        name="ssd_bidir",
        compiler_params=pltpu.CompilerParams(dimension_semantics=("arbitrary",),
                                             vmem_limit_bytes=VMEM_LIMIT),
    )(xbc, z, dt, cw, cb, dtb, alog, dsk, nw, tri)


def _back_kernel(x1_ref, att_ref, ssd_ref, p_ref, wout_ref, wgu_ref, wo_ref,
                 lng_ref, lnb_ref, wp_ref, wgate_ref, bgate_ref, o_ref, pre_ref):
    i = pl.program_id(0)
    last = pl.num_programs(0) - 1
    subs = [slice(r0, r0 + NORM_SUB) for r0 in range(0, x1_ref.shape[0], NORM_SUB)]

    def embed(rows):
        x3 = _layer_norm(pre_ref[rows, :], lng_ref[1:2, :], lnb_ref[1:2, :])
        e = jnp.dot(p_ref[rows, :].astype(BF16), wp_ref[...], preferred_element_type=F32)
        gate = _sigmoid(jnp.dot(x3.astype(BF16), wgate_ref[...], preferred_element_type=F32) + bgate_ref[...])
        o_ref[rows, :] = _layer_norm(ALPHA * x3 + gate * e, lng_ref[2:3, :], lnb_ref[2:3, :])

    late_steps = [functools.partial(embed, rows) for rows in subs]

    def ffn_stage():
        x2_parts = []
        for rows in subs:
            mix = (jnp.dot(att_ref[rows, :], wout_ref[0:ATT_WIDTH, :], preferred_element_type=F32)
                   + jnp.dot(ssd_ref[rows, :], wout_ref[ATT_WIDTH:, :], preferred_element_type=F32))
            x2_parts.append(_layer_norm(ALPHA * x1_ref[rows, :] + mix, lng_ref[0:1, :], lnb_ref[0:1, :]))
        x2 = jnp.concatenate(x2_parts, axis=0)
        f = _swiglu(x2.astype(BF16), wgu_ref, wo_ref, FF_CHUNKS_BACK, before=late_steps)
        pre_ref[...] = ALPHA * x2 + 0.5 * f

    @pl.when(i == 0)
    def _():
        pre_ref[...] = jnp.zeros_like(pre_ref)

    @pl.when(i < last)
    def _():
        ffn_stage()

    @pl.when(i == last)
    def _():
        for step in late_steps:
            step()


def _back_call(x1, att, ssd, p2d, wout, wgu, wo, lng, lnb, wp, wgate, bgate):
    T = x1.shape[0]
    tm = TOKEN_TILE
    n_tiles = T // tm
    cur = lambda i: (jnp.minimum(i, n_tiles - 1), 0)
    late = lambda i: (jnp.maximum(i - 1, 0), 0)
    consts = (wout, wgu, wo, lng, lnb, wp, wgate, bgate)
    return pl.pallas_call(
        _back_kernel, grid=(n_tiles + 1,),
        in_specs=[pl.BlockSpec((tm, D_MODEL), cur), pl.BlockSpec((tm, ATT_WIDTH), cur),
                  pl.BlockSpec((tm, SSD_WIDTH), cur), pl.BlockSpec((tm, PLE_DIM), late)]
                 + [_const_spec(c.shape) for c in consts],
        out_specs=pl.BlockSpec((tm, D_MODEL), late),
        out_shape=jax.ShapeDtypeStruct((T, D_MODEL), F32),
        scratch_shapes=[pltpu.VMEM((tm, D_MODEL), F32)],
        name="back_outproj_ffn_ple",
        compiler_params=pltpu.CompilerParams(dimension_semantics=("arbitrary",),
                                             vmem_limit_bytes=VMEM_LIMIT),
    )(x1, att, ssd, p2d, *consts)


def _in_proj_weight(w):
    half_blk = HEAD_DIM // 2
    n_q = N_Q_HEADS * HEAD_DIM
    n_kv = N_KV_HEADS * HEAD_DIM
    n_dt = N_DIRS * SSD_HEADS
    cols = np.arange(w.shape[1])
    q = cols[:n_q].reshape(N_Q_HEADS // 2, 2, half_blk, 2).transpose(0, 3, 1, 2).reshape(-1)
    k = cols[n_q:n_q + n_kv].reshape(N_KV_HEADS, half_blk, 2).transpose(0, 2, 1)
    k = np.broadcast_to(k[:, :, None, :], (N_KV_HEADS, 2, 2, half_blk)).reshape(-1)
    v = np.broadcast_to(cols[n_q + n_kv:n_q + 2 * n_kv].reshape(N_KV_HEADS, 1, HEAD_DIM),
                        (N_KV_HEADS, 2, HEAD_DIM)).reshape(-1)
    rest = cols[n_q + 2 * n_kv:]
    pad = np.tile(cols[-n_dt:], (DT_PAD - n_dt) // n_dt)
    index = np.concatenate([q, k, v, rest, pad]).astype(np.int32)
    return w.astype(BF16)[:, index]


def _rope_tiles():
    rows = SEQ // GRID_W
    row = jnp.repeat(jnp.arange(rows, dtype=F32), GRID_W)
    col = jnp.tile(jnp.arange(GRID_W, dtype=F32), rows)
    inv = ROPE_THETA ** (-jnp.arange(0, ROPE_AXIS_DIM, 2, dtype=F32) / ROPE_AXIS_DIM)
    ang = jnp.concatenate([row[:, None] * inv, col[:, None] * inv], axis=-1)
    cos, sin = jnp.cos(ang), jnp.sin(ang)
    cos_t = jnp.concatenate([cos, cos, cos, cos], axis=-1)
    sin_t = jnp.concatenate([-sin, -sin, sin, sin], axis=-1)
    return cos_t, sin_t


def _norm_tile(w):
    lanes = np.arange(LANES)
    idx = 2 * (lanes % (HEAD_DIM // 2)) + lanes // HALF_LANES
    return w[idx][None, :]


def kernel(x, p, ln_g, ln_b, ffn1_w_in, ffn1_w_out, w_in, q_norm, k_norm, conv_w, conv_b, dt_bias,
           a_log, d_skip, ssd_norm, w_out, ffn2_w_in, ffn2_w_out, ple_w, ple_gate_w, ple_gate_b):
    B, S, D = x.shape
    assert (S, D) == (SEQ, D_MODEL) and ln_g.shape[0] == DEPTH == 1
    T = B * S
    x2d = x.reshape(T, D)
    p2d = p[0].reshape(T, PLE_DIM)

    cos_t, sin_t = _rope_tiles()
    lanes = np.arange(LANES)
    same_head = ((lanes[:, None] // (HEAD_DIM // 2)) % 2) == ((lanes[None, :] // (HEAD_DIM // 2)) % 2)
    two_tiles = np.eye(2, dtype=np.float32)
    mq = jnp.asarray(np.kron(two_tiles, np.where(same_head, 1.0 / HEAD_DIM, 0.0)), BF16)
    mk = jnp.asarray(np.kron(two_tiles, np.full((LANES, LANES), 1.0 / LANES)), BF16)
    ones = np.ones((CHUNK, CHUNK), np.float32)
    tri = jnp.asarray(np.stack([np.tril(ones), np.triu(ones)]), BF16)

    i = 0
    wcat = _in_proj_weight(w_in[i])
    qw = _norm_tile(q_norm[i]) * (HEAD_DIM ** -0.5 * LOG2E)
    kw = _norm_tile(k_norm[i])
    x1, qh, kh, vh, z, xbc, dtr = _front_call(
        x2d, ffn1_w_in[i].astype(BF16), ffn1_w_out[i].astype(BF16), ln_g[i, 0:1], ln_b[i, 0:1],
        wcat, cos_t, sin_t, qw, kw, mq, mk)

    att = _attn_call(qh, kh, vh, B)

    dsk = jnp.repeat(d_skip[i], SSD_HEAD_DIM)[None, :]
    ssd = _ssd_call(xbc, z, dtr, conv_w[i], conv_b[i][None, :], dt_bias[i].reshape(-1, 1), a_log[i].reshape(-1, 1),
                    dsk, ssd_norm[i][None, :], tri, B)

    out = _back_call(x1, att, ssd, p2d, w_out[i].astype(BF16),
                     ffn2_w_in[i].astype(BF16), ffn2_w_out[i].astype(BF16), ln_g[i, 1:4], ln_b[i, 1:4],
                     ple_w[i].astype(BF16), ple_gate_w[i].astype(BF16), ple_gate_b[i][None, :])
    return out.reshape(B, S, D)
```
